```python
import jax
import jax.numpy as jnp
from jax import lax
import numpy as np

D_MODEL = 1024
BATCH = 8
SEQ = 4096
DEPTH = 2

HEAD_DIM = 64
EPS = 1e-6
MLA_HEADS = 8
MLA_NOPE = 64
MLA_ROPE = 32
MLA_V = 64
Q_LORA = 384
KV_LORA = 256
ROPE_THETA = 10000.0
Q_BLOCK = 128
DIL_HEADS = 8
DIL_PATTERNS = ((128, 1), (512, 4), (2048, 16))
DIL_BLOCK = max(w // d for (w, d) in DIL_PATTERNS)
MOBA_HEADS = 16
MOBA_BLOCK = 256
MOBA_TOPK = 3
MOBA_QCHUNK = 16
D_FF = 4 * D_MODEL
PLE_DIM = 256
N_EVEN = (DEPTH + 1) // 2
N_ODD = DEPTH // 2
EVEN_IN = Q_LORA + KV_LORA + MLA_ROPE + 3 * DIL_HEADS * HEAD_DIM
EVEN_MIX = MLA_HEADS * MLA_V + DIL_HEADS * HEAD_DIM
ODD_MIX = MOBA_HEADS * HEAD_DIM

kernel_name = 'hybrid_mla_dilated_moba_trunk'


def rms_norm(x, g):
    xf = x.astype(jnp.float32)
    y = xf * lax.rsqrt(jnp.mean(xf * xf, axis=-1, keepdims=True) + EPS)
    return (y * g.astype(jnp.float32)).astype(x.dtype)


def alibi_slopes(n):
    return 2.0 ** (-8.0 * jnp.arange(1, n + 1, dtype=jnp.float32) / n)


def apply_rope(x, positions):
    half = x.shape[-1] // 2
    inv_freq = ROPE_THETA ** (-jnp.arange(half, dtype=jnp.float32) / half)
    ang = positions.astype(jnp.float32)[..., None] * inv_freq
    ang = ang.reshape(ang.shape[:2] + (1,) * (x.ndim - 3) + (half,))
    cos, sin = jnp.cos(ang), jnp.sin(ang)
    xf = x.astype(jnp.float32)
    x1, x2 = xf[..., :half], xf[..., half:]
    return jnp.concatenate([x1 * cos - x2 * sin, x1 * sin + x2 * cos], axis=-1).astype(x.dtype)


def causal_attention_blocks(q, k, v, scale):
    B, S, H, Dk = q.shape
    nq = S // Q_BLOCK
    qb = q.reshape(B, nq, Q_BLOCK, H, Dk).swapaxes(0, 1)
    kpos = jnp.arange(S)

    def one(args):
        qi, i = args
        s = jnp.einsum('bqhd,bkhd->bhqk', qi, k, preferred_element_type=jnp.float32) * scale
        qpos = i * Q_BLOCK + jnp.arange(Q_BLOCK)
        s = jnp.where(kpos[None, :] <= qpos[:, None], s, -jnp.inf)
        pr = jax.nn.softmax(s, axis=-1).astype(v.dtype)
        return jnp.einsum('bhqk,bkhd->bqhd', pr, v)

    out = lax.map(one, (qb, jnp.arange(nq)))
    return out.swapaxes(0, 1).reshape(B, S, H, v.shape[-1])


def dilated_branch(q, k, v, slopes, scale, window, dil):
    B, S, H, Dh = q.shape
    L = S // dil
    nb = -(-L // DIL_BLOCK)
    Lp = nb * DIL_BLOCK
    reach = window // dil

    def to_class(a):
        a = a.reshape(B, L, dil, H, Dh).transpose(0, 2, 1, 3, 4)
        a = jnp.pad(a, ((0, 0), (0, 0), (0, Lp - L), (0, 0), (0, 0)))
        return a.reshape(B, dil, nb, DIL_BLOCK, H, Dh)

    def with_prev(a):
        prev = jnp.pad(a, ((0, 0), (0, 0), (1, 0), (0, 0), (0, 0), (0, 0)))[:, :, :nb]
        return jnp.concatenate([prev, a], axis=3)

    qc = to_class(q)
    kk = with_prev(to_class(k))
    vv = with_prev(to_class(v))
    s = jnp.einsum('bgnqhd,bgnkhd->bghnqk', qc, kk, preferred_element_type=jnp.float32) * scale
    qi = jnp.arange(DIL_BLOCK)[:, None]
    ki = jnp.arange(2 * DIL_BLOCK)[None, :]
    step = DIL_BLOCK + qi - ki
    key_idx = jnp.arange(nb)[:, None, None] * DIL_BLOCK - DIL_BLOCK + ki[None]
    mask = (step >= 0) & (step <= reach) & (key_idx >= 0)
    bias = slopes[:, None, None, None] * (step * dil).astype(jnp.float32)
    s = jnp.where(mask, s - bias, -jnp.inf)
    m = jnp.max(s, axis=-1, keepdims=True)
    e = jnp.exp(s - m)
    den = jnp.sum(e, axis=-1, keepdims=True)
    o = jnp.einsum('bghnqk,bgnkhd->bgnqhd', (e / den).astype(v.dtype), vv)
    lse = (m + jnp.log(den))[..., 0]
    o = o.reshape(B, dil, Lp, H, Dh)[:, :, :L].transpose(0, 2, 1, 3, 4).reshape(B, S, H, Dh)
    lse = lse.reshape(B, dil, H, Lp)[..., :L].transpose(0, 3, 1, 2).reshape(B, S, H)
    return o, lse


def dilated_attention(q, k, v, slopes, scale):
    outs, lses = [], []
    for (w, d) in DIL_PATTERNS:
        o, l = dilated_branch(q, k, v, slopes, scale, w, d)
        outs.append(o)
        lses.append(l)
    wts = jax.nn.softmax(jnp.stack(lses, axis=0), axis=0)
    out = jnp.einsum('pbsh,pbshd->bshd', wts, jnp.stack(outs, axis=0).astype(jnp.float32))
    return out.astype(q.dtype)


def moba_attention(q, k, v, slopes, scale):
    B, S, H, Dh = q.shape
    nb = -(-S // MOBA_BLOCK)
    Sp = nb * MOBA_BLOCK

    def to_blocks(a):
        a = jnp.pad(a, ((0, 0), (0, Sp - S), (0, 0), (0, 0)))
        return a.reshape(B, nb, MOBA_BLOCK, H, Dh).transpose(0, 3, 1, 2, 4)

    kb, vb = to_blocks(k), to_blocks(v)
    k_mean = jnp.mean(kb.astype(jnp.float32), axis=3)
    t = jnp.arange(S)
    own = t // MOBA_BLOCK
    gate = jnp.einsum('bshd,bhnd->bshn', q.astype(jnp.float32), k_mean)
    fully_past = jnp.arange(nb)[None, :] < own[:, None]
    gate = jnp.where(fully_past[None, :, None, :], gate, -jnp.inf)
    n_sel = min(MOBA_TOPK, nb)
    _, sel = lax.top_k(gate, n_sel)
    sel_ok = sel < own[None, :, None, None]
    nc = S // MOBA_QCHUNK

    def chunk(a):
        return a.reshape((B, nc, MOBA_QCHUNK) + a.shape[2:]).swapaxes(0, 1)

    b_ix = jnp.arange(B)[:, None, None, None]
    h_ix = jnp.arange(H)[None, None, :, None]
    k_in = jnp.arange(MOBA_BLOCK)
    n_sel_keys = n_sel * MOBA_BLOCK

    def one(args):
        qi, si, oki, ti = args
        ob = ti[0] // MOBA_BLOCK
        k_own = lax.dynamic_index_in_dim(kb, ob, axis=2, keepdims=False)
        v_own = lax.dynamic_index_in_dim(vb, ob, axis=2, keepdims=False)
        k_sel = kb[b_ix, h_ix, si]
        v_sel = vb[b_ix, h_ix, si]
        s_sel = jnp.einsum('bqhd,bqhjkd->bqhjk', qi, k_sel, preferred_element_type=jnp.float32) * scale
        s_own = jnp.einsum('bqhd,bhkd->bqhk', qi, k_own, preferred_element_type=jnp.float32) * scale
        d_sel = (ti[None, :, None, None, None] - (si[..., None] * MOBA_BLOCK + k_in)).astype(jnp.float32)
        s_sel = jnp.where(oki[..., None], s_sel - slopes[:, None, None] * d_sel, -jnp.inf)
        d_own = ti[:, None] - (ob * MOBA_BLOCK + k_in)[None, :]
        s_own = jnp.where((d_own >= 0)[None, :, None, :],
                          s_own - slopes[None, None, :, None] * d_own.astype(jnp.float32)[None, :, None, :],
                          -jnp.inf)
        s_all = jnp.concatenate([s_sel.reshape(s_sel.shape[:3] + (n_sel_keys,)), s_own], axis=-1)
        probs = jax.nn.softmax(s_all, axis=-1).astype(v.dtype)
        p_sel = probs[..., :n_sel_keys].reshape(s_sel.shape)
        p_own = probs[..., n_sel_keys:]
        return (jnp.einsum('bqhjk,bqhjkd->bqhd', p_sel, v_sel)
                + jnp.einsum('bqhk,bhkd->bqhd', p_own, v_own))

    out = lax.map(one, (chunk(q), chunk(sel), chunk(sel_ok), t.reshape(nc, MOBA_QCHUNK)))
    return out.swapaxes(0, 1).reshape(B, S, H, Dh)


def even_mixer(hn, positions, w_in, cq_norm, ckv_norm, w_uq, w_ukv, qn_nope, qn_rope,
               kn_nope, kn_rope, dil_qn, dil_kn, w_out):
    B, S, _ = hn.shape
    z = hn @ w_in
    o1 = Q_LORA
    o2 = o1 + KV_LORA
    o3 = o2 + MLA_ROPE
    dw = DIL_HEADS * HEAD_DIM
    c_q = rms_norm(z[..., :o1], cq_norm)
    c_kv = rms_norm(z[..., o1:o2], ckv_norm)
    q = (c_q @ w_uq).reshape(B, S, MLA_HEADS, MLA_NOPE + MLA_ROPE)
    q_nope = rms_norm(q[..., :MLA_NOPE], qn_nope)
    q_rope = apply_rope(rms_norm(q[..., MLA_NOPE:], qn_rope), positions)
    kv = (c_kv @ w_ukv).reshape(B, S, MLA_HEADS, MLA_NOPE + MLA_V)
    k_nope = rms_norm(kv[..., :MLA_NOPE], kn_nope)
    v_a = kv[..., MLA_NOPE:]
    k_rope = apply_rope(rms_norm(z[..., o2:o3], kn_rope), positions)
    k_rope = jnp.broadcast_to(k_rope[:, :, None, :], (B, S, MLA_HEADS, MLA_ROPE))
    q_a = jnp.concatenate([q_nope, q_rope], axis=-1)
    k_a = jnp.concatenate([k_nope, k_rope], axis=-1)
    out_a = causal_attention_blocks(q_a, k_a, v_a, (MLA_NOPE + MLA_ROPE) ** -0.5)
    q_b = rms_norm(z[..., o3:o3 + dw].reshape(B, S, DIL_HEADS, HEAD_DIM), dil_qn)
    k_b = rms_norm(z[..., o3 + dw:o3 + 2 * dw].reshape(B, S, DIL_HEADS, HEAD_DIM), dil_kn)
    v_b = z[..., o3 + 2 * dw:o3 + 3 * dw].reshape(B, S, DIL_HEADS, HEAD_DIM)
    out_b = dilated_attention(q_b, k_b, v_b, alibi_slopes(DIL_HEADS), HEAD_DIM ** -0.5)
    mixed = jnp.concatenate([out_a.reshape(B, S, MLA_HEADS * MLA_V),
                             out_b.reshape(B, S, dw)], axis=-1)
    return mixed @ w_out


def odd_mixer(hn, w_in, qn, kn, w_out):
    B, S, _ = hn.shape
    z = (hn @ w_in).reshape(B, S, 3, MOBA_HEADS, HEAD_DIM)
    q = rms_norm(z[:, :, 0], qn)
    k = rms_norm(z[:, :, 1], kn)
    v = z[:, :, 2]
    o = moba_attention(q, k, v, alibi_slopes(MOBA_HEADS), HEAD_DIM ** -0.5)
    return o.reshape(B, S, ODD_MIX) @ w_out


def setup_inputs(seed: int = 0) -> dict:
    key = jax.random.key(seed)
    ks = iter(jax.random.split(key, 32))

    def w(shape, fan_in):
        return jax.random.normal(next(ks), shape, jnp.float32) * (fan_in ** -0.5)

    def gain(shape):
        return 1.0 + 0.05 * jax.random.normal(next(ks), shape, jnp.float32)

    x = jax.random.normal(next(ks), (BATCH, SEQ, D_MODEL), jnp.float32)
    p = jax.random.normal(next(ks), (DEPTH, BATCH, SEQ, PLE_DIM), jnp.float32)
    positions = jnp.broadcast_to(jnp.arange(SEQ, dtype=jnp.int32), (BATCH, SEQ))
    return {
        'x': x,
        'p': p,
        'positions': positions,
        'e_w_in': w((N_EVEN, D_MODEL, EVEN_IN), D_MODEL),
        'e_cq_norm': gain((N_EVEN, Q_LORA)),
        'e_ckv_norm': gain((N_EVEN, KV_LORA)),
        'e_w_uq': w((N_EVEN, Q_LORA, MLA_HEADS * (MLA_NOPE + MLA_ROPE)), Q_LORA),
        'e_w_ukv': w((N_EVEN, KV_LORA, MLA_HEADS * (MLA_NOPE + MLA_V)), KV_LORA),
        'e_qn_nope': gain((N_EVEN, MLA_NOPE)),
        'e_qn_rope': gain((N_EVEN, MLA_ROPE)),
        'e_kn_nope': gain((N_EVEN, MLA_NOPE)),
        'e_kn_rope': gain((N_EVEN, MLA_ROPE)),
        'e_dil_qn': gain((N_EVEN, HEAD_DIM)),
        'e_dil_kn': gain((N_EVEN, HEAD_DIM)),
        'e_w_out': w((N_EVEN, EVEN_MIX, D_MODEL), EVEN_MIX),
        'o_w_in': w((N_ODD, D_MODEL, 3 * ODD_MIX), D_MODEL),
        'o_qn': gain((N_ODD, HEAD_DIM)),
        'o_kn': gain((N_ODD, HEAD_DIM)),
        'o_w_out': w((N_ODD, ODD_MIX, D_MODEL), ODD_MIX),
        'mix_norm': gain((DEPTH, D_MODEL)),
        'ff_norm': gain((DEPTH, D_MODEL)),
        'w_ff1': w((DEPTH, D_MODEL, D_FF), D_MODEL),
        'w_ff2': w((DEPTH, D_FF, D_MODEL), D_FF),
        'ple_norm': gain((DEPTH, D_MODEL)),
        'w_ple_gate': w((DEPTH, D_MODEL, D_MODEL), D_MODEL),
        'w_ple_proj': w((DEPTH, PLE_DIM, D_MODEL), PLE_DIM),
    }


def reference(x, p, positions, e_w_in, e_cq_norm, e_ckv_norm, e_w_uq, e_w_ukv, e_qn_nope,
              e_qn_rope, e_kn_nope, e_kn_rope, e_dil_qn, e_dil_kn, e_w_out, o_w_in, o_qn,
              o_kn, o_w_out, mix_norm, ff_norm, w_ff1, w_ff2, ple_norm, w_ple_gate, w_ple_proj):
    h = x
    for i in range(DEPTH):
        j = i // 2
        hn = rms_norm(h, mix_norm[i])
        if i % 2 == 0:
            mix = even_mixer(hn, positions, e_w_in[j], e_cq_norm[j], e_ckv_norm[j], e_w_uq[j],
                             e_w_ukv[j], e_qn_nope[j], e_qn_rope[j], e_kn_nope[j], e_kn_rope[j],
                             e_dil_qn[j], e_dil_kn[j], e_w_out[j])
        else:
            mix = odd_mixer(hn, o_w_in[j], o_qn[j], o_kn[j], o_w_out[j])
        h = h + mix
        u = jax.nn.relu(rms_norm(h, ff_norm[i]) @ w_ff1[i])
        h = h + (u * u) @ w_ff2[i]
        g = jax.nn.sigmoid(rms_norm(h, ple_norm[i]) @ w_ple_gate[i])
        h = h + g * (p[i] @ w_ple_proj[i])
    return h
```

```python
import functools

import numpy as np
import jax
import jax.numpy as jnp
from jax import lax
from jax.experimental import pallas as pl
from jax.experimental.pallas import tpu as pltpu

D_MODEL = 1024
BATCH = 8
SEQ = 4096
DEPTH = 2
HEAD_DIM = 64
EPS = 1e-6
MLA_HEADS = 8
MLA_NOPE = 64
MLA_ROPE = 32
MLA_V = 64
Q_LORA = 384
KV_LORA = 256
ROPE_THETA = 10000.0
DIL_HEADS = 8
DIL_PATTERNS = ((128, 1), (512, 4), (2048, 16))
DIL_BLOCK = 128
MOBA_HEADS = 16
MOBA_BLOCK = 256
MOBA_TOPK = 3
D_FF = 4 * D_MODEL
PLE_DIM = 256

LANES = 128
MXU_DIM = 256
VMEM_LIMIT = 56 * 1024 * 1024
ROW_TILE = 512
ATT_TILE = 256

F32 = jnp.float32
BF16 = jnp.bfloat16
NT_DIMS = (((1,), (1,)), ((), ()))


def _const_spec(shape):
    nd = len(shape)
    return pl.BlockSpec(shape, lambda *_: (0,) * nd)


def _params(sem):
    return pltpu.CompilerParams(dimension_semantics=sem, vmem_limit_bytes=VMEM_LIMIT)


def _rms(x, g):
    return x * lax.rsqrt(jnp.mean(x * x, axis=-1, keepdims=True) + EPS) * g


def _group_rms(x, gmat):
    ms = jnp.dot((x * x).astype(BF16), gmat, preferred_element_type=F32)
    return x * lax.rsqrt(ms + EPS)


def _even_proj_kernel(x_ref, pos_ref, gmix_ref, win_ref, gcq_ref, gckv_ref, wuq_ref, wukv_ref,
                      gq_ref, gkn_ref, gkr_ref, gqb_ref, gkb_ref, gmq_ref, gm64_ref, gm32_ref,
                      invf_ref, qa_ref, ka_ref, va_ref, qb_ref, kb_ref, vb_ref):
    hn = _rms(x_ref[...], gmix_ref[...]).astype(BF16)
    z = jnp.dot(hn, win_ref[...], preferred_element_type=F32)
    o1 = Q_LORA
    o2 = o1 + KV_LORA
    o3 = o2 + LANES
    dw = DIL_HEADS * HEAD_DIM
    c_q = _rms(z[:, :o1], gcq_ref[...]).astype(BF16)
    c_kv = _rms(z[:, o1:o2], gckv_ref[...]).astype(BF16)

    ang = pos_ref[...].astype(F32) * invf_ref[...]
    cos = jnp.cos(ang)
    sin = jnp.sin(ang)
    lane = lax.broadcasted_iota(jnp.int32, (1, LANES), 1)
    first_half = (lane % MLA_ROPE) < (MLA_ROPE // 2)

    def rope(xr):
        rot = jnp.where(first_half, -pltpu.roll(xr, LANES - MLA_ROPE // 2, 1),
                        pltpu.roll(xr, MLA_ROPE // 2, 1))
        return xr * cos + rot * sin

    q = jnp.dot(c_q, wuq_ref[...], preferred_element_type=F32)
    gq = gq_ref[...]
    for c in range(MLA_HEADS // 2):
        lo = c * MXU_DIM
        qc = _group_rms(q[:, lo:lo + MXU_DIM], gmq_ref[...]) * gq[:, lo:lo + MXU_DIM]
        qa_ref[:, lo:lo + LANES] = qc[:, :LANES].astype(BF16)
        qa_ref[:, lo + LANES:lo + MXU_DIM] = rope(qc[:, LANES:]).astype(BF16)

    kv = jnp.dot(c_kv, wukv_ref[...], preferred_element_type=F32)
    k_rope = rope(_group_rms(z[:, o2:o3], gm32_ref[...]) * gkr_ref[...]).astype(BF16)
    gkn = gkn_ref[...]
    for c in range(MLA_HEADS // 2):
        lo = c * LANES
        if c % 2 == 0:
            kn2 = _group_rms(kv[:, lo:lo + MXU_DIM], gm64_ref[...]) * gkn[:, lo:lo + MXU_DIM]
        kn = kn2[:, (c % 2) * LANES:(c % 2 + 1) * LANES]
        ka_ref[:, c * MXU_DIM:c * MXU_DIM + LANES] = kn.astype(BF16)
        ka_ref[:, c * MXU_DIM + LANES:(c + 1) * MXU_DIM] = k_rope
    va_ref[...] = kv[:, dw:].astype(BF16)

    gqb = gqb_ref[...]
    gkb = gkb_ref[...]
    for c in range(dw // MXU_DIM):
        lo = c * MXU_DIM
        qb_ref[:, lo:lo + MXU_DIM] = (
            _group_rms(z[:, o3 + lo:o3 + lo + MXU_DIM], gm64_ref[...]) * gqb[:, lo:lo + MXU_DIM]
        ).astype(BF16)
        kb_ref[:, lo:lo + MXU_DIM] = (
            _group_rms(z[:, o3 + dw + lo:o3 + dw + lo + MXU_DIM], gm64_ref[...])
            * gkb[:, lo:lo + MXU_DIM]).astype(BF16)
    vb_ref[...] = z[:, o3 + 2 * dw:o3 + 3 * dw].astype(BF16)


def _even_proj(h2d, pos2d, gmix, win, gcq, gckv, wuq, wukv, gq, gkn, gkr, gqb, gkb,
               gmq, gm64, gm32, invf):
    t = h2d.shape[0]
    tm = ROW_TILE
    consts = (gmix, win, gcq, gckv, wuq, wukv, gq, gkn, gkr, gqb, gkb, gmq, gm64, gm32, invf)
    widths = (1024, 1024, 512, 512, 512, 512)
    return pl.pallas_call(
        _even_proj_kernel,
        grid=(t // tm,),
        in_specs=[pl.BlockSpec((tm, D_MODEL), lambda i: (i, 0)),
                  pl.BlockSpec((tm, 1), lambda i: (i, 0))] + [_const_spec(c.shape) for c in consts],
        out_specs=[pl.BlockSpec((tm, w), lambda i: (i, 0)) for w in widths],
        out_shape=[jax.ShapeDtypeStruct((t, w), BF16) for w in widths],
        compiler_params=_params(("parallel",)),
        name="even_proj",
    )(h2d, pos2d, *consts)


def _mla_kernel(q_ref, k_ref, v_ref, o_ref):
    i = pl.program_id(2)
    tq = ATT_TILE
    lane_q = lax.broadcasted_iota(jnp.int32, (1, MXU_DIM), 1)
    lane_o = lax.broadcasted_iota(jnp.int32, (1, LANES), 1)
    qi = lax.broadcasted_iota(jnp.int32, (tq, tq), 0)
    ki = lax.broadcasted_iota(jnp.int32, (tq, tq), 1)
    q = q_ref[...]
    outs = []
    for hh in range(2):
        nope = (lane_q >= MLA_NOPE * hh) & (lane_q < MLA_NOPE * (hh + 1))
        rope = (lane_q >= LANES + MLA_ROPE * hh) & (lane_q < LANES + MLA_ROPE * (hh + 1))
        qh = jnp.where(nope | rope, q, jnp.zeros_like(q))

        def tile(j, carry, diagonal):
            m, l, acc = carry
            r0 = pl.multiple_of(j * tq, tq)
            kt = k_ref[0, pl.ds(r0, tq), :]
            vt = v_ref[0, pl.ds(r0, tq), :]
            s = lax.dot_general(qh, kt, NT_DIMS, preferred_element_type=F32)
            if diagonal:
                s = jnp.where(ki <= qi, s, -jnp.inf)
            m_new = jnp.maximum(m, jnp.max(s, axis=1, keepdims=True))
            alpha = jnp.exp(m - m_new)
            p = jnp.exp(s - m_new)
            l = alpha * l + jnp.sum(p, axis=1, keepdims=True)
            acc = alpha * acc + jnp.dot(p.astype(BF16), vt, preferred_element_type=F32)
            return m_new, l, acc

        carry = (jnp.full((tq, 1), -jnp.inf, F32), jnp.zeros((tq, 1), F32),
                 jnp.zeros((tq, LANES), F32))
        carry = tile(i, carry, True)
        m, l, acc = lax.fori_loop(0, i, functools.partial(tile, diagonal=False), carry)
        outs.append(acc / l)
    o_ref[...] = jnp.where(lane_o < MLA_V, outs[0], outs[1]).astype(BF16)


def _mla_attention(qa, ka, va):
    t = qa.shape[0]
    nq = SEQ // ATT_TILE
    ka3 = ka.reshape(BATCH, SEQ, ka.shape[1])
    va3 = va.reshape(BATCH, SEQ, va.shape[1])
    return pl.pallas_call(
        _mla_kernel,
        grid=(BATCH, MLA_HEADS // 2, nq),
        in_specs=[pl.BlockSpec((ATT_TILE, MXU_DIM), lambda b, c, i: (b * nq + i, c)),
                  pl.BlockSpec((1, SEQ, MXU_DIM), lambda b, c, i: (b, 0, c)),
                  pl.BlockSpec((1, SEQ, LANES), lambda b, c, i: (b, 0, c))],
        out_specs=pl.BlockSpec((ATT_TILE, LANES), lambda b, c, i: (b * nq + i, c)),
        out_shape=jax.ShapeDtypeStruct((t, MLA_HEADS * MLA_V), BF16),
        compiler_params=_params(("parallel", "parallel", "arbitrary")),
        name="mla_attention",
    )(qa, ka3, va3)


def _dil_kernel(q_ref, k_ref, v_ref, mb_ref, o_ref, lse_ref, *, nqb):
    blk = DIL_BLOCK
    lane = lax.broadcasted_iota(jnp.int32, (1, LANES), 1)
    low = lane < HEAD_DIM

    def block(qb, first):
        r0 = pl.multiple_of(qb * blk, blk)
        q = q_ref[0, pl.ds(r0, blk), :]
        if first:
            k = k_ref[0, 0:blk, :]
            v = v_ref[0, 0:blk, :]
        else:
            p0 = pl.multiple_of(r0 - blk, blk)
            k = k_ref[0, pl.ds(p0, 2 * blk), :]
            v = v_ref[0, pl.ds(p0, 2 * blk), :]
        os, ls = [], []
        for hh in range(2):
            qh = jnp.where(low if hh == 0 else ~low, q, jnp.zeros_like(q))
            s = lax.dot_general(qh, k, NT_DIMS, preferred_element_type=F32)
            mb = mb_ref[0, hh]
            s = s - (mb[:, blk:] if first else mb)
            m = jnp.max(s, axis=1, keepdims=True)
            e = jnp.exp(s - m)
            den = jnp.sum(e, axis=1, keepdims=True)
            os.append(jnp.dot(e.astype(BF16), v, preferred_element_type=F32) / den)
            ls.append(m + jnp.log(den))
        o_ref[0, pl.ds(r0, blk), :] = jnp.where(low, os[0], os[1])
        lse_ref[0, pl.ds(r0, blk), :] = jnp.where(low, ls[0], ls[1])

    block(0, True)

    def body(qb, carry):
        block(qb, False)
        return carry

    lax.fori_loop(1, nqb, body, 0)


def _dilated_branch(qb, kb, vb, mb, dil):
    l = SEQ // dil
    cols = DIL_HEADS * HEAD_DIM * dil
    view = lambda a: a.reshape(BATCH, l, cols)
    npair = DIL_HEADS // 2
    blk3 = pl.BlockSpec((1, l, LANES), lambda b, c: (b, 0, c))
    o, lse = pl.pallas_call(
        functools.partial(_dil_kernel, nqb=l // DIL_BLOCK),
        grid=(BATCH, cols // LANES),
        in_specs=[blk3, blk3, blk3,
                  pl.BlockSpec((1, 2, DIL_BLOCK, 2 * DIL_BLOCK), lambda b, c: (c % npair, 0, 0, 0))],
        out_specs=[blk3, blk3],
        out_shape=[jax.ShapeDtypeStruct((BATCH, l, cols), F32)] * 2,
        compiler_params=_params(("parallel", "parallel")),
        name=f"dilated_d{dil}",
    )(view(qb), view(kb), view(vb), mb)
    t = BATCH * SEQ
    return o.reshape(t, DIL_HEADS * HEAD_DIM), lse.reshape(t, DIL_HEADS * HEAD_DIM)


def _dil_bias_table(dil, window):
    reach = window // dil
    qi = np.arange(DIL_BLOCK)[:, None]
    ki = np.arange(2 * DIL_BLOCK)[None, :]
    step = DIL_BLOCK + qi - ki
    valid = (step >= 0) & (step <= reach)
    slopes = 2.0 ** (-8.0 * np.arange(1, DIL_HEADS + 1, dtype=np.float32) / DIL_HEADS)
    bias = slopes.astype(np.float32)[:, None, None] * (step * dil).astype(np.float32)[None]
    tab = np.where(valid[None], bias, np.inf).astype(np.float32)
    return jnp.asarray(tab.reshape(DIL_HEADS // 2, 2, DIL_BLOCK, 2 * DIL_BLOCK))


def _even_out_kernel(h_ref, oa_ref, o1_ref, o2_ref, o3_ref, l1_ref, l2_ref, l3_ref, w_ref, out_ref):
    l1, l2, l3 = l1_ref[...], l2_ref[...], l3_ref[...]
    lm = jnp.maximum(jnp.maximum(l1, l2), l3)
    w1, w2, w3 = jnp.exp(l1 - lm), jnp.exp(l2 - lm), jnp.exp(l3 - lm)
    mixed_b = (w1 * o1_ref[...] + w2 * o2_ref[...] + w3 * o3_ref[...]) / (w1 + w2 + w3)
    half = MLA_HEADS * MLA_V
    acc = jnp.dot(oa_ref[...], w_ref[:half, :], preferred_element_type=F32)
    acc += jnp.dot(mixed_b.astype(BF16), w_ref[half:, :], preferred_element_type=F32)
    out_ref[...] = h_ref[...] + acc


def _even_out(h2d, oa, obs, lses, w_out):
    t = h2d.shape[0]
    tm = ROW_TILE
    half = pl.BlockSpec((tm, 512), lambda i: (i, 0))
    full = pl.BlockSpec((tm, D_MODEL), lambda i: (i, 0))
    return pl.pallas_call(
        _even_out_kernel,
        grid=(t // tm,),
        in_specs=[full] + [half] * 7 + [_const_spec(w_out.shape)],
        out_specs=full,
        out_shape=jax.ShapeDtypeStruct((t, D_MODEL), F32),
        compiler_params=_params(("parallel",)),
        name="even_out_proj",
    )(h2d, oa, *obs, *lses, w_out)


def _odd_out_kernel(h_ref, o_ref, w_ref, out_ref):
    out_ref[...] = h_ref[...] + jnp.dot(o_ref[...], w_ref[...], preferred_element_type=F32)


def _odd_out(h2d, o, w_out):
    t = h2d.shape[0]
    tm = ROW_TILE
    full = pl.BlockSpec((tm, D_MODEL), lambda i: (i, 0))
    return pl.pallas_call(
        _odd_out_kernel,
        grid=(t // tm,),
        in_specs=[full, full, _const_spec(w_out.shape)],
        out_specs=full,
        out_shape=jax.ShapeDtypeStruct((t, D_MODEL), F32),
        compiler_params=_params(("parallel",)),
        name="odd_out_proj",
    )(h2d, o, w_out)


def _ffn_kernel(h_ref, g_ref, w1_ref, w2_ref, out_ref):
    h = h_ref[...]
    n = _rms(h, g_ref[...]).astype(BF16)
    acc = h
    chunk = D_MODEL
    for c in range(D_FF // chunk):
        u = jnp.maximum(jnp.dot(n, w1_ref[:, c * chunk:(c + 1) * chunk],
                                preferred_element_type=F32), 0.0)
        acc = acc + jnp.dot((u * u).astype(BF16), w2_ref[c * chunk:(c + 1) * chunk, :],
                            preferred_element_type=F32)
    out_ref[...] = acc


def _ffn(h2d, g, w1, w2):
    t = h2d.shape[0]
    tm = ROW_TILE
    full = pl.BlockSpec((tm, D_MODEL), lambda i: (i, 0))
    return pl.pallas_call(
        _ffn_kernel,
        grid=(t // tm,),
        in_specs=[full, _const_spec(g.shape), _const_spec(w1.shape), _const_spec(w2.shape)],
        out_specs=full,
        out_shape=jax.ShapeDtypeStruct((t, D_MODEL), F32),
        compiler_params=_params(("parallel",)),
        name="ffn",
    )(h2d, g, w1, w2)


def _ple_kernel(h_ref, p_ref, g_ref, wg_ref, wp_ref, out_ref):
    h = h_ref[...]
    n = _rms(h, g_ref[...]).astype(BF16)
    gate = jax.nn.sigmoid(jnp.dot(n, wg_ref[...], preferred_element_type=F32))
    proj = jnp.dot(p_ref[...].astype(BF16), wp_ref[...], preferred_element_type=F32)
    out_ref[...] = h + gate * proj


def _ple(h2d, p2d, g, wg, wp):
    t = h2d.shape[0]
    tm = ROW_TILE
    full = pl.BlockSpec((tm, D_MODEL), lambda i: (i, 0))
    return pl.pallas_call(
        _ple_kernel,
        grid=(t // tm,),
        in_specs=[full, pl.BlockSpec((tm, PLE_DIM), lambda i: (i, 0)), _const_spec(g.shape),
                  _const_spec(wg.shape), _const_spec(wp.shape)],
        out_specs=full,
        out_shape=jax.ShapeDtypeStruct((t, D_MODEL), F32),
        compiler_params=_params(("parallel",)),
        name="ple",
    )(h2d, p2d, g, wg, wp)


def _odd_proj_kernel(x_ref, gmix_ref, win_ref, gq_ref, gk_ref, gm64_ref,
                     q_ref, k_ref, v_ref, km_ref):
    hn = _rms(x_ref[...], gmix_ref[...]).astype(BF16)
    z = jnp.dot(hn, win_ref[...], preferred_element_type=F32)
    dw = MOBA_HEADS * HEAD_DIM
    gq = gq_ref[...]
    gk = gk_ref[...]
    tm = x_ref.shape[0]
    for c in range(dw // MXU_DIM):
        lo = c * MXU_DIM
        q_ref[:, lo:lo + MXU_DIM] = (
            _group_rms(z[:, lo:lo + MXU_DIM], gm64_ref[...]) * gq[:, lo:lo + MXU_DIM]).astype(BF16)
        k = _group_rms(z[:, dw + lo:dw + lo + MXU_DIM], gm64_ref[...]) * gk[:, lo:lo + MXU_DIM]
        k_ref[:, lo:lo + MXU_DIM] = k.astype(BF16)
        for r in range(tm // MOBA_BLOCK):
            km_ref[r, :, lo:lo + MXU_DIM] = jnp.mean(
                k[r * MOBA_BLOCK:(r + 1) * MOBA_BLOCK], axis=0, keepdims=True)
    v_ref[...] = z[:, 2 * dw:].astype(BF16)


def _odd_proj(h2d, gmix, win, gq, gk, gm64):
    t = h2d.shape[0]
    tm = ROW_TILE
    dw = MOBA_HEADS * HEAD_DIM
    consts = (gmix, win, gq, gk, gm64)
    full = pl.BlockSpec((tm, dw), lambda i: (i, 0))
    return pl.pallas_call(
        _odd_proj_kernel,
        grid=(t // tm,),
        in_specs=[pl.BlockSpec((tm, D_MODEL), lambda i: (i, 0))] + [_const_spec(c.shape) for c in consts],
        out_specs=[full, full, full,
                   pl.BlockSpec((tm // MOBA_BLOCK, 1, dw), lambda i: (i, 0, 0))],
        out_shape=[jax.ShapeDtypeStruct((t, dw), BF16)] * 3
        + [jax.ShapeDtypeStruct((t // MOBA_BLOCK, 1, dw), F32)],
        compiler_params=_params(("parallel",)),
        name="odd_proj",
    )(h2d, *consts)


def _moba_kernel(q_ref, k_ref, v_ref, km_ref, sl_ref, o_ref):
    i = pl.program_id(2)
    tq = MOBA_BLOCK
    nb = SEQ // MOBA_BLOCK
    lane = lax.broadcasted_iota(jnp.int32, (1, LANES), 1)
    low = lane < HEAD_DIM
    qi = lax.broadcasted_iota(jnp.int32, (tq, tq), 0)
    ki = lax.broadcasted_iota(jnp.int32, (tq, tq), 1)
    rel = (qi - ki).astype(F32)
    blk_id = lax.broadcasted_iota(jnp.int32, (tq, nb), 1)
    q = q_ref[...]
    km = km_ref[0]
    outs = []
    for hh in range(2):
        mine = low if hh == 0 else ~low
        qh = jnp.where(mine, q, jnp.zeros_like(q))
        slope = sl_ref[0, :, hh * HEAD_DIM:hh * HEAD_DIM + 1]
        relb = rel * slope

        gate = lax.dot_general(qh.astype(F32), km, NT_DIMS, preferred_element_type=F32,
                               precision=lax.Precision.HIGHEST)
        g = jnp.where(blk_id < i, gate, -jnp.inf)
        sel = []
        for _ in range(MOBA_TOPK):
            mx = jnp.max(g, axis=1, keepdims=True)
            idx = jnp.min(jnp.where(g == mx, blk_id, nb), axis=1, keepdims=True)
            sel.append(jnp.where(mx > -jnp.inf, idx, -1))
            g = jnp.where(blk_id == idx, -jnp.inf, g)

        r_own = pl.multiple_of(i * tq, tq)
        s = lax.dot_general(qh, k_ref[0, pl.ds(r_own, tq), :], NT_DIMS, preferred_element_type=F32)
        s = jnp.where(ki <= qi, s - relb, -jnp.inf)
        m = jnp.max(s, axis=1, keepdims=True)
        p = jnp.exp(s - m)
        l = jnp.sum(p, axis=1, keepdims=True)
        acc = jnp.dot(p.astype(BF16), v_ref[0, pl.ds(r_own, tq), :], preferred_element_type=F32)

        def tile(j, carry):
            m, l, acc = carry
            r0 = pl.multiple_of(j * tq, tq)
            picked = (sel[0] == j) | (sel[1] == j) | (sel[2] == j)
            off = slope * ((i - j) * tq).astype(F32)
            s = lax.dot_general(qh, k_ref[0, pl.ds(r0, tq), :], NT_DIMS,
                                preferred_element_type=F32) - relb
            cand = jnp.max(s, axis=1, keepdims=True) - off
            m_new = jnp.where(picked, jnp.maximum(m, cand), m)
            p = jnp.exp(s - jnp.where(picked, m_new + off, jnp.inf))
            alpha = jnp.exp(m - m_new)
            l = alpha * l + jnp.sum(p, axis=1, keepdims=True)
            acc = alpha * acc + jnp.dot(p.astype(BF16), v_ref[0, pl.ds(r0, tq), :],
                                        preferred_element_type=F32)
            return m_new, l, acc

        m, l, acc = lax.fori_loop(0, i, tile, (m, l, acc))
        outs.append(acc / l)
    o_ref[...] = jnp.where(low, outs[0], outs[1]).astype(BF16)


def _moba_attention(q, k, v, km, slopes):
    t = q.shape[0]
    nq = SEQ // MOBA_BLOCK
    dw = MOBA_HEADS * HEAD_DIM
    k3 = k.reshape(BATCH, SEQ, dw)
    v3 = v.reshape(BATCH, SEQ, dw)
    km3 = km.reshape(BATCH, nq, dw)
    qspec = pl.BlockSpec((MOBA_BLOCK, LANES), lambda b, c, i: (b * nq + i, c))
    kvspec = pl.BlockSpec((1, SEQ, LANES), lambda b, c, i: (b, 0, c))
    return pl.pallas_call(
        _moba_kernel,
        grid=(BATCH, MOBA_HEADS // 2, nq),
        in_specs=[qspec, kvspec, kvspec,
                  pl.BlockSpec((1, nq, LANES), lambda b, c, i: (b, 0, c)),
                  pl.BlockSpec((1, 1, LANES), lambda b, c, i: (c, 0, 0))],
        out_specs=qspec,
        out_shape=jax.ShapeDtypeStruct((t, dw), BF16),
        compiler_params=_params(("parallel", "parallel", "arbitrary")),
        name="moba_attention",
    )(q, k3, v3, km3, slopes)


def _group_matrix(sizes, width):
    g = np.zeros((width, width), np.float32)
    lo = 0
    for n in sizes:
        g[lo:lo + n, lo:lo + n] = 1.0 / n
        lo += n
    return jnp.asarray(g, BF16)


def _pair_slopes(n_heads):
    slopes = 2.0 ** (-8.0 * np.arange(1, n_heads + 1, dtype=np.float32) / n_heads)
    tab = np.repeat(slopes.astype(np.float32), HEAD_DIM).reshape(n_heads // 2, 1, LANES)
    return jnp.asarray(tab)


def _even_weights(w_in, w_uq, w_ukv, qn_nope, qn_rope, kn_nope, kn_rope, dil_qn, dil_kn):
    o1 = Q_LORA
    o2 = o1 + KV_LORA
    o3 = o2 + MLA_ROPE
    kr = w_in[:, o2:o3]
    pad64 = jnp.zeros((D_MODEL, LANES - 2 * MLA_ROPE), F32)
    win = jnp.concatenate([w_in[:, :o2], kr, kr, pad64, w_in[:, o3:]], axis=1).astype(BF16)

    qd = MLA_NOPE + MLA_ROPE
    a_scale = qd ** -0.5
    zq = jnp.zeros((Q_LORA, LANES - 2 * MLA_ROPE), F32)
    zg = jnp.zeros((LANES - 2 * MLA_ROPE,), F32)
    cols, gains = [], []
    for c in range(MLA_HEADS // 2):
        h0, h1 = 2 * c, 2 * c + 1
        cols += [w_uq[:, h0 * qd:h0 * qd + MLA_NOPE], w_uq[:, h1 * qd:h1 * qd + MLA_NOPE],
                 w_uq[:, h0 * qd + MLA_NOPE:(h0 + 1) * qd], w_uq[:, h1 * qd + MLA_NOPE:(h1 + 1) * qd], zq]
        gains += [qn_nope, qn_nope, qn_rope, qn_rope, zg]
    wuq = jnp.concatenate(cols, axis=1).astype(BF16)
    gq = (jnp.concatenate(gains) * a_scale)[None, :]

    kvd = MLA_NOPE + MLA_V
    kcols = [w_ukv[:, h * kvd:h * kvd + MLA_NOPE] for h in range(MLA_HEADS)]
    vcols = [w_ukv[:, h * kvd + MLA_NOPE:(h + 1) * kvd] for h in range(MLA_HEADS)]
    wukv = jnp.concatenate(kcols + vcols, axis=1).astype(BF16)
    gkn = jnp.tile(kn_nope, MLA_HEADS)[None, :]
    gkr = jnp.concatenate([kn_rope, kn_rope, zg])[None, :]
    gqb = (jnp.tile(dil_qn, DIL_HEADS) * HEAD_DIM ** -0.5)[None, :]
    gkb = jnp.tile(dil_kn, DIL_HEADS)[None, :]
    return win, wuq, wukv, gq, gkn, gkr, gqb, gkb


def kernel(x, p, positions, e_w_in, e_cq_norm, e_ckv_norm, e_w_uq, e_w_ukv, e_qn_nope, e_qn_rope, e_kn_nope, e_kn_rope, e_dil_qn, e_dil_kn, e_w_out, o_w_in, o_qn, o_kn, o_w_out, mix_norm, ff_norm, w_ff1, w_ff2, ple_norm, w_ple_gate, w_ple_proj):
    t = BATCH * SEQ
    h = x.reshape(t, D_MODEL)
    pos2d = positions.reshape(t, 1)
    p2d = p.reshape(DEPTH, t, PLE_DIM)

    gmq = _group_matrix((MLA_NOPE, MLA_NOPE, MLA_ROPE, MLA_ROPE, LANES - 2 * MLA_ROPE), MXU_DIM)
    gm64 = _group_matrix((HEAD_DIM,) * (MXU_DIM // HEAD_DIM), MXU_DIM)
    gm32 = _group_matrix((MLA_ROPE,) * (LANES // MLA_ROPE), LANES)
    half = MLA_ROPE // 2
    invf = ROPE_THETA ** (-(jnp.arange(LANES) % half).astype(F32) / half)
    invf = invf[None, :]

    for i in range(DEPTH):
        j = i // 2
        gmix = mix_norm[i][None, :]
        if i % 2 == 0:
            win, wuq, wukv, gq, gkn, gkr, gqb, gkb = _even_weights(
                e_w_in[j], e_w_uq[j], e_w_ukv[j], e_qn_nope[j], e_qn_rope[j], e_kn_nope[j],
                e_kn_rope[j], e_dil_qn[j], e_dil_kn[j])
            qa, ka, va, qb, kb, vb = _even_proj(
                h, pos2d, gmix, win, e_cq_norm[j][None, :], e_ckv_norm[j][None, :], wuq, wukv,
                gq, gkn, gkr, gqb, gkb, gmq, gm64, gm32, invf)
            oa = _mla_attention(qa, ka, va)
            obs, lses = [], []
            for (w, d) in DIL_PATTERNS:
                o, lse = _dilated_branch(qb, kb, vb, _dil_bias_table(d, w), d)
                obs.append(o)
                lses.append(lse)
            h = _even_out(h, oa, obs, lses, e_w_out[j].astype(BF16))
        else:
            gq = (jnp.tile(o_qn[j], MOBA_HEADS) * HEAD_DIM ** -0.5)[None, :]
            gk = jnp.tile(o_kn[j], MOBA_HEADS)[None, :]
            q, k, v, km = _odd_proj(h, gmix, o_w_in[j].astype(BF16), gq, gk, gm64)
            o = _moba_attention(q, k, v, km, _pair_slopes(MOBA_HEADS))
            h = _odd_out(h, o, o_w_out[j].astype(BF16))
        h = _ffn(h, ff_norm[i][None, :], w_ff1[i].astype(BF16), w_ff2[i].astype(BF16))
        h = _ple(h, p2d[i], ple_norm[i][None, :], w_ple_gate[i].astype(BF16),
                 w_ple_proj[i].astype(BF16))
    return h.reshape(BATCH, SEQ, D_MODEL)
```

```python
import functools

import numpy as np
import jax
import jax.numpy as jnp
from jax import lax
from jax.experimental import pallas as pl
from jax.experimental.pallas import tpu as pltpu

D_MODEL = 1024
BATCH = 8
SEQ = 4096
DEPTH = 2
HEAD_DIM = 64
EPS = 1e-6
MLA_HEADS = 8
MLA_NOPE = 64
MLA_ROPE = 32
MLA_V = 64
Q_LORA = 384
KV_LORA = 256
ROPE_THETA = 10000.0
DIL_HEADS = 8
DIL_PATTERNS = ((128, 1), (512, 4), (2048, 16))
DIL_BLOCK = 128
MOBA_HEADS = 16
MOBA_BLOCK = 256
MOBA_TOPK = 3
D_FF = 4 * D_MODEL
PLE_DIM = 256

LANES = 128
MXU_DIM = 256
VMEM_LIMIT = 56 * 1024 * 1024
ROW_TILE = 512
ATT_TILE = 256

F32 = jnp.float32
BF16 = jnp.bfloat16
NT_DIMS = (((1,), (1,)), ((), ()))


def _const_spec(shape):
    nd = len(shape)
    return pl.BlockSpec(shape, lambda *_: (0,) * nd)


def _params(sem):
    return pltpu.CompilerParams(dimension_semantics=sem, vmem_limit_bytes=VMEM_LIMIT)


def _rms(x, g):
    return x * lax.rsqrt(jnp.mean(x * x, axis=-1, keepdims=True) + EPS) * g


def _group_rms(x, gmat):
    ms = jnp.dot((x * x).astype(BF16), gmat, preferred_element_type=F32)
    return x * lax.rsqrt(ms + EPS)


def _even_proj_kernel(x_ref, pos_ref, gmix_ref, win_ref, gcq_ref, gckv_ref, wuq_ref, wukv_ref,
                      gq_ref, gkn_ref, gkr_ref, gqb_ref, gkb_ref, gmq_ref, gm64_ref, gm32_ref,
                      invf_ref, qa_ref, ka_ref, va_ref, qb_ref, kb_ref, vb_ref):
    hn = _rms(x_ref[...], gmix_ref[...]).astype(BF16)
    z = jnp.dot(hn, win_ref[...], preferred_element_type=F32)
    o1 = Q_LORA
    o2 = o1 + KV_LORA
    o3 = o2 + LANES
    dw = DIL_HEADS * HEAD_DIM
    c_q = _rms(z[:, :o1], gcq_ref[...]).astype(BF16)
    c_kv = _rms(z[:, o1:o2], gckv_ref[...]).astype(BF16)

    ang = pos_ref[...].astype(F32) * invf_ref[...]
    cos = jnp.cos(ang)
    sin = jnp.sin(ang)
    lane = lax.broadcasted_iota(jnp.int32, (1, LANES), 1)
    first_half = (lane % MLA_ROPE) < (MLA_ROPE // 2)

    def rope(xr):
        rot = jnp.where(first_half, -pltpu.roll(xr, LANES - MLA_ROPE // 2, 1),
                        pltpu.roll(xr, MLA_ROPE // 2, 1))
        return xr * cos + rot * sin

    q = jnp.dot(c_q, wuq_ref[...], preferred_element_type=F32)
    gq = gq_ref[...]
    for c in range(MLA_HEADS // 2):
        lo = c * MXU_DIM
        qc = _group_rms(q[:, lo:lo + MXU_DIM], gmq_ref[...]) * gq[:, lo:lo + MXU_DIM]
        qa_ref[:, lo:lo + LANES] = qc[:, :LANES].astype(BF16)
        qa_ref[:, lo + LANES:lo + MXU_DIM] = rope(qc[:, LANES:]).astype(BF16)

    kv = jnp.dot(c_kv, wukv_ref[...], preferred_element_type=F32)
    k_rope = rope(_group_rms(z[:, o2:o3], gm32_ref[...]) * gkr_ref[...]).astype(BF16)
    gkn = gkn_ref[...]
    for c in range(MLA_HEADS // 2):
        lo = c * LANES
        if c % 2 == 0:
            kn2 = _group_rms(kv[:, lo:lo + MXU_DIM], gm64_ref[...]) * gkn[:, lo:lo + MXU_DIM]
        kn = kn2[:, (c % 2) * LANES:(c % 2 + 1) * LANES]
        ka_ref[:, c * MXU_DIM:c * MXU_DIM + LANES] = kn.astype(BF16)
        ka_ref[:, c * MXU_DIM + LANES:(c + 1) * MXU_DIM] = k_rope
    va_ref[...] = kv[:, dw:].astype(BF16)

    gqb = gqb_ref[...]
    gkb = gkb_ref[...]
    for c in range(dw // MXU_DIM):
        lo = c * MXU_DIM
        qb_ref[:, lo:lo + MXU_DIM] = (
            _group_rms(z[:, o3 + lo:o3 + lo + MXU_DIM], gm64_ref[...]) * gqb[:, lo:lo + MXU_DIM]
        ).astype(BF16)
        kb_ref[:, lo:lo + MXU_DIM] = (
            _group_rms(z[:, o3 + dw + lo:o3 + dw + lo + MXU_DIM], gm64_ref[...])
            * gkb[:, lo:lo + MXU_DIM]).astype(BF16)
    vb_ref[...] = z[:, o3 + 2 * dw:o3 + 3 * dw].astype(BF16)


def _even_proj(h2d, pos2d, gmix, win, gcq, gckv, wuq, wukv, gq, gkn, gkr, gqb, gkb,
               gmq, gm64, gm32, invf):
    t = h2d.shape[0]
    tm = ROW_TILE
    consts = (gmix, win, gcq, gckv, wuq, wukv, gq, gkn, gkr, gqb, gkb, gmq, gm64, gm32, invf)
    widths = (1024, 1024, 512, 512, 512, 512)
    return pl.pallas_call(
        _even_proj_kernel,
        grid=(t // tm,),
        in_specs=[pl.BlockSpec((tm, D_MODEL), lambda i: (i, 0)),
                  pl.BlockSpec((tm, 1), lambda i: (i, 0))] + [_const_spec(c.shape) for c in consts],
        out_specs=[pl.BlockSpec((tm, w), lambda i: (i, 0)) for w in widths],
        out_shape=[jax.ShapeDtypeStruct((t, w), BF16) for w in widths],
        compiler_params=_params(("parallel",)),
        name="even_proj",
    )(h2d, pos2d, *consts)


def _mla_kernel(q_ref, k_ref, vt_ref, o_ref, s_a, s_b):
    i = pl.program_id(2)
    tq = ATT_TILE
    lane_q = lax.broadcasted_iota(jnp.int32, (1, MXU_DIM), 1)
    kpos = lax.broadcasted_iota(jnp.int32, (tq, tq), 0)
    qpos = lax.broadcasted_iota(jnp.int32, (tq, tq), 1)
    q = q_ref[...].astype(F32)
    qts = []
    for hh in range(2):
        nope = (lane_q >= MLA_NOPE * hh) & (lane_q < MLA_NOPE * (hh + 1))
        rope = (lane_q >= LANES + MLA_ROPE * hh) & (lane_q < LANES + MLA_ROPE * (hh + 1))
        qts.append(jnp.where(nope | rope, q, 0.0).T.astype(BF16))

    def scores(j, diagonal, slot):
        kt = k_ref[0, pl.ds(pl.multiple_of(j * tq, tq), tq), :]
        cms = []
        for hh in range(2):
            s = jnp.dot(kt, qts[hh], preferred_element_type=F32)
            if diagonal:
                s = jnp.where(kpos <= qpos, s, -jnp.inf)
            slot[hh] = s
            cms.append(jnp.max(s, axis=0, keepdims=True))
        return tuple(cms)

    def consume(j, slot, cms, carry):
        vt = vt_ref[0, :, pl.ds(pl.multiple_of(j * tq, tq), tq)]
        ps, stats = [], []
        for hh in range(2):
            m, l, _ = carry[hh]
            m_new = jnp.maximum(m, cms[hh])
            alpha = jnp.exp(m - m_new)
            p = jnp.exp(slot[hh] - m_new)
            stats.append((m_new, alpha * l + jnp.sum(p, axis=0, keepdims=True), alpha))
            ps.append(p.astype(BF16))
        new = []
        for hh in range(2):
            m_new, l, alpha = stats[hh]
            acc = alpha * carry[hh][2] + jnp.dot(vt[hh * MLA_V:(hh + 1) * MLA_V, :], ps[hh],
                                                 preferred_element_type=F32)
            new.append((m_new, l, acc))
        return tuple(new)

    init = (jnp.full((1, tq), -jnp.inf, F32), jnp.zeros((1, tq), F32), jnp.zeros((MLA_V, tq), F32))

    def step(j, state, src, dst):
        cms, carry = state
        nxt = scores(j, False, dst)
        return nxt, consume(jnp.where(j == 0, i, j - 1), src, cms, carry)

    def body(j, state):
        return lax.cond(j % 2 == 0, lambda st: step(j, st, s_a, s_b),
                        lambda st: step(j, st, s_b, s_a), state)

    state = lax.fori_loop(0, i, body, (scores(i, True, s_a), (init, init)))
    last = jnp.where(i == 0, i, i - 1)
    carry = lax.cond(i % 2 == 0, lambda st: consume(last, s_a, *st),
                     lambda st: consume(last, s_b, *st), state)
    o_t = jnp.concatenate([acc / l for (_, l, acc) in carry], axis=0)
    o_ref[...] = o_t.T.astype(BF16)


def _mla_attention(qa, ka, va):
    t = qa.shape[0]
    nq = SEQ // ATT_TILE
    ka3 = ka.reshape(BATCH, SEQ, ka.shape[1])
    vat = va.reshape(BATCH, SEQ, va.shape[1]).transpose(0, 2, 1)
    return pl.pallas_call(
        _mla_kernel,
        grid=(BATCH, MLA_HEADS // 2, nq),
        in_specs=[pl.BlockSpec((ATT_TILE, MXU_DIM), lambda b, c, i: (b * nq + i, c)),
                  pl.BlockSpec((1, SEQ, MXU_DIM), lambda b, c, i: (b, 0, c)),
                  pl.BlockSpec((1, LANES, SEQ), lambda b, c, i: (b, c, 0))],
        out_specs=pl.BlockSpec((ATT_TILE, LANES), lambda b, c, i: (b * nq + i, c)),
        out_shape=jax.ShapeDtypeStruct((t, MLA_HEADS * MLA_V), BF16),
        scratch_shapes=[pltpu.VMEM((2, ATT_TILE, ATT_TILE), F32)] * 2,
        compiler_params=_params(("parallel", "parallel", "arbitrary")),
        name="mla_attention",
    )(qa, ka3, vat)


def _dil_kernel(q_ref, k_ref, v_ref, mb_ref, o_ref, lse_ref, *, nqb):
    blk = DIL_BLOCK
    lane = lax.broadcasted_iota(jnp.int32, (1, LANES), 1)
    low = lane < HEAD_DIM

    def block(qb, first):
        r0 = pl.multiple_of(qb * blk, blk)
        q = q_ref[0, pl.ds(r0, blk), :]
        if first:
            k = k_ref[0, 0:blk, :]
            v = v_ref[0, 0:blk, :]
        else:
            p0 = pl.multiple_of(r0 - blk, blk)
            k = k_ref[0, pl.ds(p0, 2 * blk), :]
            v = v_ref[0, pl.ds(p0, 2 * blk), :]
        os, ls = [], []
        for hh in range(2):
            qh = jnp.where(low if hh == 0 else ~low, q, jnp.zeros_like(q))
            s = lax.dot_general(qh, k, NT_DIMS, preferred_element_type=F32)
            mb = mb_ref[0, hh]
            s = s - (mb[:, blk:] if first else mb)
            m = jnp.max(s, axis=1, keepdims=True)
            e = jnp.exp(s - m)
            den = jnp.sum(e, axis=1, keepdims=True)
            os.append(jnp.dot(e.astype(BF16), v, preferred_element_type=F32) / den)
            ls.append(m + jnp.log(den))
        o_ref[0, pl.ds(r0, blk), :] = jnp.where(low, os[0], os[1])
        lse_ref[0, pl.ds(r0, blk), :] = jnp.where(low, ls[0], ls[1])

    block(0, True)

    def body(qb, carry):
        block(qb, False)
        return carry

    lax.fori_loop(1, nqb, body, 0)


def _dilated_branch(qb, kb, vb, mb, dil):
    l = SEQ // dil
    cols = DIL_HEADS * HEAD_DIM * dil
    view = lambda a: a.reshape(BATCH, l, cols)
    npair = DIL_HEADS // 2
    blk3 = pl.BlockSpec((1, l, LANES), lambda b, c: (b, 0, c))
    o, lse = pl.pallas_call(
        functools.partial(_dil_kernel, nqb=l // DIL_BLOCK),
        grid=(BATCH, cols // LANES),
        in_specs=[blk3, blk3, blk3,
                  pl.BlockSpec((1, 2, DIL_BLOCK, 2 * DIL_BLOCK), lambda b, c: (c % npair, 0, 0, 0))],
        out_specs=[blk3, blk3],
        out_shape=[jax.ShapeDtypeStruct((BATCH, l, cols), F32)] * 2,
        compiler_params=_params(("parallel", "parallel")),
        name=f"dilated_d{dil}",
    )(view(qb), view(kb), view(vb), mb)
    t = BATCH * SEQ
    return o.reshape(t, DIL_HEADS * HEAD_DIM), lse.reshape(t, DIL_HEADS * HEAD_DIM)


def _dil_bias_table(dil, window):
    reach = window // dil
    qi = np.arange(DIL_BLOCK)[:, None]
    ki = np.arange(2 * DIL_BLOCK)[None, :]
    step = DIL_BLOCK + qi - ki
    valid = (step >= 0) & (step <= reach)
    slopes = 2.0 ** (-8.0 * np.arange(1, DIL_HEADS + 1, dtype=np.float32) / DIL_HEADS)
    bias = slopes.astype(np.float32)[:, None, None] * (step * dil).astype(np.float32)[None]
    tab = np.where(valid[None], bias, np.inf).astype(np.float32)
    return jnp.asarray(tab.reshape(DIL_HEADS // 2, 2, DIL_BLOCK, 2 * DIL_BLOCK))


def _even_out_kernel(h_ref, oa_ref, o1_ref, o2_ref, o3_ref, l1_ref, l2_ref, l3_ref, w_ref, out_ref):
    l1, l2, l3 = l1_ref[...], l2_ref[...], l3_ref[...]
    lm = jnp.maximum(jnp.maximum(l1, l2), l3)
    w1, w2, w3 = jnp.exp(l1 - lm), jnp.exp(l2 - lm), jnp.exp(l3 - lm)
    mixed_b = (w1 * o1_ref[...] + w2 * o2_ref[...] + w3 * o3_ref[...]) / (w1 + w2 + w3)
    half = MLA_HEADS * MLA_V
    acc = jnp.dot(oa_ref[...], w_ref[:half, :], preferred_element_type=F32)
    acc += jnp.dot(mixed_b.astype(BF16), w_ref[half:, :], preferred_element_type=F32)
    out_ref[...] = h_ref[...] + acc


def _even_out(h2d, oa, obs, lses, w_out):
    t = h2d.shape[0]
    tm = ROW_TILE
    half = pl.BlockSpec((tm, 512), lambda i: (i, 0))
    full = pl.BlockSpec((tm, D_MODEL), lambda i: (i, 0))
    return pl.pallas_call(
        _even_out_kernel,
        grid=(t // tm,),
        in_specs=[full] + [half] * 7 + [_const_spec(w_out.shape)],
        out_specs=full,
        out_shape=jax.ShapeDtypeStruct((t, D_MODEL), F32),
        compiler_params=_params(("parallel",)),
        name="even_out_proj",
    )(h2d, oa, *obs, *lses, w_out)


def _odd_out_kernel(h_ref, o_ref, w_ref, out_ref):
    out_ref[...] = h_ref[...] + jnp.dot(o_ref[...], w_ref[...], preferred_element_type=F32)


def _odd_out(h2d, o, w_out):
    t = h2d.shape[0]
    tm = ROW_TILE
    full = pl.BlockSpec((tm, D_MODEL), lambda i: (i, 0))
    return pl.pallas_call(
        _odd_out_kernel,
        grid=(t // tm,),
        in_specs=[full, full, _const_spec(w_out.shape)],
        out_specs=full,
        out_shape=jax.ShapeDtypeStruct((t, D_MODEL), F32),
        compiler_params=_params(("parallel",)),
        name="odd_out_proj",
    )(h2d, o, w_out)


def _ffn_kernel(h_ref, g_ref, w1_ref, w2_ref, out_ref):
    h = h_ref[...]
    n = _rms(h, g_ref[...]).astype(BF16)
    acc = h
    chunk = D_MODEL
    for c in range(D_FF // chunk):
        u = jnp.maximum(jnp.dot(n, w1_ref[:, c * chunk:(c + 1) * chunk],
                                preferred_element_type=F32), 0.0)
        acc = acc + jnp.dot((u * u).astype(BF16), w2_ref[c * chunk:(c + 1) * chunk, :],
                            preferred_element_type=F32)
    out_ref[...] = acc


def _ffn(h2d, g, w1, w2):
    t = h2d.shape[0]
    tm = ROW_TILE
    full = pl.BlockSpec((tm, D_MODEL), lambda i: (i, 0))
    return pl.pallas_call(
        _ffn_kernel,
        grid=(t // tm,),
        in_specs=[full, _const_spec(g.shape), _const_spec(w1.shape), _const_spec(w2.shape)],
        out_specs=full,
        out_shape=jax.ShapeDtypeStruct((t, D_MODEL), F32),
        compiler_params=_params(("parallel",)),
        name="ffn",
    )(h2d, g, w1, w2)


def _ple_kernel(h_ref, p_ref, g_ref, wg_ref, wp_ref, out_ref):
    h = h_ref[...]
    n = _rms(h, g_ref[...]).astype(BF16)
    gate = jax.nn.sigmoid(jnp.dot(n, wg_ref[...], preferred_element_type=F32))
    proj = jnp.dot(p_ref[...].astype(BF16), wp_ref[...], preferred_element_type=F32)
    out_ref[...] = h + gate * proj


def _ple(h2d, p2d, g, wg, wp):
    t = h2d.shape[0]
    tm = ROW_TILE
    full = pl.BlockSpec((tm, D_MODEL), lambda i: (i, 0))
    return pl.pallas_call(
        _ple_kernel,
        grid=(t // tm,),
        in_specs=[full, pl.BlockSpec((tm, PLE_DIM), lambda i: (i, 0)), _const_spec(g.shape),
                  _const_spec(wg.shape), _const_spec(wp.shape)],
        out_specs=full,
        out_shape=jax.ShapeDtypeStruct((t, D_MODEL), F32),
        compiler_params=_params(("parallel",)),
        name="ple",
    )(h2d, p2d, g, wg, wp)


def _odd_proj_kernel(x_ref, gmix_ref, win_ref, gq_ref, gk_ref, gm64_ref,
                     q_ref, k_ref, v_ref, km_ref):
    hn = _rms(x_ref[...], gmix_ref[...]).astype(BF16)
    z = jnp.dot(hn, win_ref[...], preferred_element_type=F32)
    dw = MOBA_HEADS * HEAD_DIM
    gq = gq_ref[...]
    gk = gk_ref[...]
    tm = x_ref.shape[0]
    for c in range(dw // MXU_DIM):
        lo = c * MXU_DIM
        q_ref[:, lo:lo + MXU_DIM] = (
            _group_rms(z[:, lo:lo + MXU_DIM], gm64_ref[...]) * gq[:, lo:lo + MXU_DIM]).astype(BF16)
        k = _group_rms(z[:, dw + lo:dw + lo + MXU_DIM], gm64_ref[...]) * gk[:, lo:lo + MXU_DIM]
        k_ref[:, lo:lo + MXU_DIM] = k.astype(BF16)
        for r in range(tm // MOBA_BLOCK):
            km_ref[r, :, lo:lo + MXU_DIM] = jnp.mean(
                k[r * MOBA_BLOCK:(r + 1) * MOBA_BLOCK], axis=0, keepdims=True)
    v_ref[...] = z[:, 2 * dw:].astype(BF16)


def _odd_proj(h2d, gmix, win, gq, gk, gm64):
    t = h2d.shape[0]
    tm = ROW_TILE
    dw = MOBA_HEADS * HEAD_DIM
    consts = (gmix, win, gq, gk, gm64)
    full = pl.BlockSpec((tm, dw), lambda i: (i, 0))
    return pl.pallas_call(
        _odd_proj_kernel,
        grid=(t // tm,),
        in_specs=[pl.BlockSpec((tm, D_MODEL), lambda i: (i, 0))] + [_const_spec(c.shape) for c in consts],
        out_specs=[full, full, full,
                   pl.BlockSpec((tm // MOBA_BLOCK, 1, dw), lambda i: (i, 0, 0))],
        out_shape=[jax.ShapeDtypeStruct((t, dw), BF16)] * 3
        + [jax.ShapeDtypeStruct((t // MOBA_BLOCK, 1, dw), F32)],
        compiler_params=_params(("parallel",)),
        name="odd_proj",
    )(h2d, *consts)


def _moba_kernel(q_ref, k_ref, vt_ref, km_ref, sl_ref, o_ref, s_a, s_b):
    i = pl.program_id(2)
    tq = MOBA_BLOCK
    nb = SEQ // MOBA_BLOCK
    lane = lax.broadcasted_iota(jnp.int32, (1, LANES), 1)
    low = lane < HEAD_DIM
    kpos = lax.broadcasted_iota(jnp.int32, (tq, tq), 0)
    qpos = lax.broadcasted_iota(jnp.int32, (tq, tq), 1)
    rel = (qpos - kpos).astype(F32)
    blk_id = lax.broadcasted_iota(jnp.int32, (nb, tq), 0)
    q = q_ref[...].astype(F32)
    km = km_ref[0]
    heads = []
    for hh in range(2):
        qt = jnp.where(low if hh == 0 else ~low, q, 0.0).T
        slope = sl_ref[0, :, hh * HEAD_DIM:hh * HEAD_DIM + 1]
        gate = jnp.dot(km, qt, preferred_element_type=F32,
                       precision=lax.Precision.HIGHEST)
        g = jnp.where(blk_id < i, gate, -jnp.inf)
        sel = []
        for _ in range(MOBA_TOPK):
            mx = jnp.max(g, axis=0, keepdims=True)
            idx = jnp.min(jnp.where(g == mx, blk_id, nb), axis=0, keepdims=True)
            sel.append(jnp.where(mx > -jnp.inf, idx, -1))
            g = jnp.where(blk_id == idx, -jnp.inf, g)
        heads.append((qt.astype(BF16), slope, rel * slope, sel))

    def scores(j, own, slot):
        kt = k_ref[0, pl.ds(pl.multiple_of(j * tq, tq), tq), :]
        cms = []
        for hh in range(2):
            qt, _, relb, _ = heads[hh]
            s = jnp.dot(kt, qt, preferred_element_type=F32) - relb
            if own:
                s = jnp.where(kpos <= qpos, s, -jnp.inf)
            slot[hh] = s
            cms.append(jnp.max(s, axis=0, keepdims=True))
        return tuple(cms)

    def consume(j, slot, cms, carry):
        vt = vt_ref[0, :, pl.ds(pl.multiple_of(j * tq, tq), tq)]
        ps, stats = [], []
        for hh in range(2):
            _, slope, _, sel = heads[hh]
            m, l, _ = carry[hh]
            picked = (sel[0] == j) | (sel[1] == j) | (sel[2] == j) | (j == i)
            off = slope * ((i - j) * tq).astype(F32)
            m_new = jnp.where(picked, jnp.maximum(m, cms[hh] - off), m)
            alpha = jnp.exp(m - m_new)
            p = jnp.exp(slot[hh] - jnp.where(picked, m_new + off, jnp.inf))
            stats.append((m_new, alpha * l + jnp.sum(p, axis=0, keepdims=True), alpha))
            ps.append(p.astype(BF16))
        new = []
        for hh in range(2):
            m_new, l, alpha = stats[hh]
            acc = alpha * carry[hh][2] + jnp.dot(vt[hh * HEAD_DIM:(hh + 1) * HEAD_DIM, :], ps[hh],
                                                 preferred_element_type=F32)
            new.append((m_new, l, acc))
        return tuple(new)

    init = (jnp.full((1, tq), -jnp.inf, F32), jnp.zeros((1, tq), F32),
            jnp.zeros((HEAD_DIM, tq), F32))

    def step(j, state, src, dst):
        cms, carry = state
        nxt = scores(j, False, dst)
        return nxt, consume(jnp.where(j == 0, i, j - 1), src, cms, carry)

    def body(j, state):
        return lax.cond(j % 2 == 0, lambda st: step(j, st, s_a, s_b),
                        lambda st: step(j, st, s_b, s_a), state)

    state = lax.fori_loop(0, i, body, (scores(i, True, s_a), (init, init)))
    last = jnp.where(i == 0, i, i - 1)
    carry = lax.cond(i % 2 == 0, lambda st: consume(last, s_a, *st),
                     lambda st: consume(last, s_b, *st), state)
    o_t = jnp.concatenate([acc / l for (_, l, acc) in carry], axis=0)
    o_ref[...] = o_t.T.astype(BF16)


def _moba_attention(q, k, v, km, slopes):
    t = q.shape[0]
    nq = SEQ // MOBA_BLOCK
    dw = MOBA_HEADS * HEAD_DIM
    k3 = k.reshape(BATCH, SEQ, dw)
    vt3 = v.reshape(BATCH, SEQ, dw).transpose(0, 2, 1)
    km3 = km.reshape(BATCH, nq, dw)
    qspec = pl.BlockSpec((MOBA_BLOCK, LANES), lambda b, c, i: (b * nq + i, c))
    return pl.pallas_call(
        _moba_kernel,
        grid=(BATCH, MOBA_HEADS // 2, nq),
        in_specs=[qspec,
                  pl.BlockSpec((1, SEQ, LANES), lambda b, c, i: (b, 0, c)),
                  pl.BlockSpec((1, LANES, SEQ), lambda b, c, i: (b, c, 0)),
                  pl.BlockSpec((1, nq, LANES), lambda b, c, i: (b, 0, c)),
                  pl.BlockSpec((1, 1, LANES), lambda b, c, i: (c, 0, 0))],
        out_specs=qspec,
        out_shape=jax.ShapeDtypeStruct((t, dw), BF16),
        scratch_shapes=[pltpu.VMEM((2, MOBA_BLOCK, MOBA_BLOCK), F32)] * 2,
        compiler_params=_params(("parallel", "parallel", "arbitrary")),
        name="moba_attention",
    )(q, k3, vt3, km3, slopes)


def _group_matrix(sizes, width):
    g = np.zeros((width, width), np.float32)
    lo = 0
    for n in sizes:
        g[lo:lo + n, lo:lo + n] = 1.0 / n
        lo += n
    return jnp.asarray(g, BF16)


def _pair_slopes(n_heads):
    slopes = 2.0 ** (-8.0 * np.arange(1, n_heads + 1, dtype=np.float32) / n_heads)
    tab = np.repeat(slopes.astype(np.float32), HEAD_DIM).reshape(n_heads // 2, 1, LANES)
    return jnp.asarray(tab)


def _even_weights(w_in, w_uq, w_ukv, qn_nope, qn_rope, kn_nope, kn_rope, dil_qn, dil_kn):
    o1 = Q_LORA
    o2 = o1 + KV_LORA
    o3 = o2 + MLA_ROPE
    kr = w_in[:, o2:o3]
    pad64 = jnp.zeros((D_MODEL, LANES - 2 * MLA_ROPE), F32)
    win = jnp.concatenate([w_in[:, :o2], kr, kr, pad64, w_in[:, o3:]], axis=1).astype(BF16)

    qd = MLA_NOPE + MLA_ROPE
    a_scale = qd ** -0.5
    zq = jnp.zeros((Q_LORA, LANES - 2 * MLA_ROPE), F32)
    zg = jnp.zeros((LANES - 2 * MLA_ROPE,), F32)
    cols, gains = [], []
    for c in range(MLA_HEADS // 2):
        h0, h1 = 2 * c, 2 * c + 1
        cols += [w_uq[:, h0 * qd:h0 * qd + MLA_NOPE], w_uq[:, h1 * qd:h1 * qd + MLA_NOPE],
                 w_uq[:, h0 * qd + MLA_NOPE:(h0 + 1) * qd], w_uq[:, h1 * qd + MLA_NOPE:(h1 + 1) * qd], zq]
        gains += [qn_nope, qn_nope, qn_rope, qn_rope, zg]
    wuq = jnp.concatenate(cols, axis=1).astype(BF16)
    gq = (jnp.concatenate(gains) * a_scale)[None, :]

    kvd = MLA_NOPE + MLA_V
    kcols = [w_ukv[:, h * kvd:h * kvd + MLA_NOPE] for h in range(MLA_HEADS)]
    vcols = [w_ukv[:, h * kvd + MLA_NOPE:(h + 1) * kvd] for h in range(MLA_HEADS)]
    wukv = jnp.concatenate(kcols + vcols, axis=1).astype(BF16)
    gkn = jnp.tile(kn_nope, MLA_HEADS)[None, :]
    gkr = jnp.concatenate([kn_rope, kn_rope, zg])[None, :]
    gqb = (jnp.tile(dil_qn, DIL_HEADS) * HEAD_DIM ** -0.5)[None, :]
    gkb = jnp.tile(dil_kn, DIL_HEADS)[None, :]
    return win, wuq, wukv, gq, gkn, gkr, gqb, gkb


def kernel(x, p, positions, e_w_in, e_cq_norm, e_ckv_norm, e_w_uq, e_w_ukv, e_qn_nope, e_qn_rope, e_kn_nope, e_kn_rope, e_dil_qn, e_dil_kn, e_w_out, o_w_in, o_qn, o_kn, o_w_out, mix_norm, ff_norm, w_ff1, w_ff2, ple_norm, w_ple_gate, w_ple_proj):
    t = BATCH * SEQ
    h = x.reshape(t, D_MODEL)
    pos2d = positions.reshape(t, 1)
    p2d = p.reshape(DEPTH, t, PLE_DIM)

    gmq = _group_matrix((MLA_NOPE, MLA_NOPE, MLA_ROPE, MLA_ROPE, LANES - 2 * MLA_ROPE), MXU_DIM)
    gm64 = _group_matrix((HEAD_DIM,) * (MXU_DIM // HEAD_DIM), MXU_DIM)
    gm32 = _group_matrix((MLA_ROPE,) * (LANES // MLA_ROPE), LANES)
    half = MLA_ROPE // 2
    invf = ROPE_THETA ** (-(jnp.arange(LANES) % half).astype(F32) / half)
    invf = invf[None, :]

    for i in range(DEPTH):
        j = i // 2
        gmix = mix_norm[i][None, :]
        if i % 2 == 0:
            win, wuq, wukv, gq, gkn, gkr, gqb, gkb = _even_weights(
                e_w_in[j], e_w_uq[j], e_w_ukv[j], e_qn_nope[j], e_qn_rope[j], e_kn_nope[j],
                e_kn_rope[j], e_dil_qn[j], e_dil_kn[j])
            qa, ka, va, qb, kb, vb = _even_proj(
                h, pos2d, gmix, win, e_cq_norm[j][None, :], e_ckv_norm[j][None, :], wuq, wukv,
                gq, gkn, gkr, gqb, gkb, gmq, gm64, gm32, invf)
            oa = _mla_attention(qa, ka, va)
            obs, lses = [], []
            for (w, d) in DIL_PATTERNS:
                o, lse = _dilated_branch(qb, kb, vb, _dil_bias_table(d, w), d)
                obs.append(o)
                lses.append(lse)
            h = _even_out(h, oa, obs, lses, e_w_out[j].astype(BF16))
        else:
            gq = (jnp.tile(o_qn[j], MOBA_HEADS) * HEAD_DIM ** -0.5)[None, :]
            gk = jnp.tile(o_kn[j], MOBA_HEADS)[None, :]
            q, k, v, km = _odd_proj(h, gmix, o_w_in[j].astype(BF16), gq, gk, gm64)
            o = _moba_attention(q, k, v, km, _pair_slopes(MOBA_HEADS))
            h = _odd_out(h, o, o_w_out[j].astype(BF16))
        h = _ffn(h, ff_norm[i][None, :], w_ff1[i].astype(BF16), w_ff2[i].astype(BF16))
        h = _ple(h, p2d[i], ple_norm[i][None, :], w_ple_gate[i].astype(BF16),
                 w_ple_proj[i].astype(BF16))
    return h.reshape(BATCH, SEQ, D_MODEL)
```

```python
import functools

import numpy as np
import jax
import jax.numpy as jnp
from jax import lax
from jax.experimental import pallas as pl
from jax.experimental.pallas import tpu as pltpu

D_MODEL = 1024
BATCH = 8
SEQ = 4096
DEPTH = 2
HEAD_DIM = 64
EPS = 1e-6
MLA_HEADS = 8
MLA_NOPE = 64
MLA_ROPE = 32
MLA_V = 64
Q_LORA = 384
KV_LORA = 256
ROPE_THETA = 10000.0
DIL_HEADS = 8
DIL_PATTERNS = ((128, 1), (512, 4), (2048, 16))
DIL_BLOCK = 128
MOBA_HEADS = 16
MOBA_BLOCK = 256
MOBA_TOPK = 3
D_FF = 4 * D_MODEL
PLE_DIM = 256

LANES = 128
MXU_DIM = 256
VMEM_LIMIT = 56 * 1024 * 1024
ROW_TILE = 512
ATT_TILE = 256

F32 = jnp.float32
BF16 = jnp.bfloat16


def _const_spec(shape):
    nd = len(shape)
    return pl.BlockSpec(shape, lambda *_: (0,) * nd)


def _params(sem):
    return pltpu.CompilerParams(dimension_semantics=sem, vmem_limit_bytes=VMEM_LIMIT)


def _rms(x, g):
    return x * lax.rsqrt(jnp.mean(x * x, axis=-1, keepdims=True) + EPS) * g


def _group_rms(x, gmat):
    ms = jnp.dot((x * x).astype(BF16), gmat, preferred_element_type=F32)
    return x * lax.rsqrt(ms + EPS)


def _even_proj_kernel(x_ref, pos_ref, gmix_ref, win_ref, gcq_ref, gckv_ref, wuq_ref, wukv_ref,
                      gq_ref, gkn_ref, gkr_ref, gqb_ref, gkb_ref, gmq_ref, gm64_ref, gm32_ref,
                      invf_ref, qa_ref, ka_ref, vat_ref, *rest):
    dil_refs, stage_ref = rest[:-1], rest[-1]
    hn = _rms(x_ref[...], gmix_ref[...]).astype(BF16)
    z = jnp.dot(hn, win_ref[...], preferred_element_type=F32)
    o1 = Q_LORA
    o2 = o1 + KV_LORA
    o3 = o2 + LANES
    dw = DIL_HEADS * HEAD_DIM
    c_q = _rms(z[:, :o1], gcq_ref[...]).astype(BF16)
    c_kv = _rms(z[:, o1:o2], gckv_ref[...]).astype(BF16)

    ang = pos_ref[...].astype(F32) * invf_ref[...]
    cos = jnp.cos(ang)
    sin = jnp.sin(ang)
    lane = lax.broadcasted_iota(jnp.int32, (1, LANES), 1)
    first_half = (lane % MLA_ROPE) < (MLA_ROPE // 2)

    def rope(xr):
        rot = jnp.where(first_half, -pltpu.roll(xr, LANES - MLA_ROPE // 2, 1),
                        pltpu.roll(xr, MLA_ROPE // 2, 1))
        return xr * cos + rot * sin

    q = jnp.dot(c_q, wuq_ref[...], preferred_element_type=F32)
    gq = gq_ref[...]
    for c in range(MLA_HEADS // 2):
        lo = c * MXU_DIM
        qc = _group_rms(q[:, lo:lo + MXU_DIM], gmq_ref[...]) * gq[:, lo:lo + MXU_DIM]
        qa_ref[:, lo:lo + LANES] = qc[:, :LANES].astype(BF16)
        qa_ref[:, lo + LANES:lo + MXU_DIM] = rope(qc[:, LANES:]).astype(BF16)

    kv = jnp.dot(c_kv, wukv_ref[...], preferred_element_type=F32)
    k_rope = rope(_group_rms(z[:, o2:o3], gm32_ref[...]) * gkr_ref[...]).astype(BF16)
    gkn = gkn_ref[...]
    for c in range(MLA_HEADS // 2):
        lo = c * LANES
        if c % 2 == 0:
            kn2 = _group_rms(kv[:, lo:lo + MXU_DIM], gm64_ref[...]) * gkn[:, lo:lo + MXU_DIM]
        kn = kn2[:, (c % 2) * LANES:(c % 2 + 1) * LANES]
        ka_ref[:, c * MXU_DIM:c * MXU_DIM + LANES] = kn.astype(BF16)
        ka_ref[:, c * MXU_DIM + LANES:(c + 1) * MXU_DIM] = k_rope
    vat_ref[0] = kv[:, dw:].T.astype(BF16)

    gqb = gqb_ref[...]
    gkb = gkb_ref[...]
    qkv = []
    for c in range(dw // MXU_DIM):
        lo = c * MXU_DIM
        qkv.append((
            _group_rms(z[:, o3 + lo:o3 + lo + MXU_DIM], gm64_ref[...]) * gqb[:, lo:lo + MXU_DIM],
            _group_rms(z[:, o3 + dw + lo:o3 + dw + lo + MXU_DIM], gm64_ref[...])
            * gkb[:, lo:lo + MXU_DIM],
            z[:, o3 + 2 * dw + lo:o3 + 2 * dw + lo + MXU_DIM]))
    tm = x_ref.shape[0]
    for a in range(3):
        for c in range(dw // LANES):
            half = (c % 2) * LANES
            stage_ref[a, c] = qkv[c // 2][a][:, half:half + LANES]
            for (_, dil), out_ref in zip(DIL_PATTERNS, dil_refs[a::3]):
                for r in range(dil):
                    out_ref[0, r, :, c * LANES:(c + 1) * LANES] = stage_ref[
                        a, c, pl.ds(r, tm // dil, stride=dil), :].astype(BF16)


def _even_proj(h2d, pos2d, gmix, win, gcq, gckv, wuq, wukv, gq, gkn, gkr, gqb, gkb,
               gmq, gm64, gm32, invf):
    t = h2d.shape[0]
    tm = ROW_TILE
    consts = (gmix, win, gcq, gckv, wuq, wukv, gq, gkn, gkr, gqb, gkb, gmq, gm64, gm32, invf)
    tpb = SEQ // tm
    dw = DIL_HEADS * HEAD_DIM
    out_specs = [pl.BlockSpec((tm, 1024), lambda i: (i, 0)),
                 pl.BlockSpec((tm, 1024), lambda i: (i, 0)),
                 pl.BlockSpec((1, MLA_HEADS * MLA_V, tm), lambda i: (i // tpb, 0, i % tpb))]
    out_shape = [jax.ShapeDtypeStruct((t, 1024), BF16), jax.ShapeDtypeStruct((t, 1024), BF16),
                 jax.ShapeDtypeStruct((BATCH, MLA_HEADS * MLA_V, SEQ), BF16)]
    for (_, dil) in DIL_PATTERNS:
        for _ in range(3):
            out_specs.append(pl.BlockSpec((1, dil, tm // dil, dw),
                                          lambda i: (i // tpb, 0, i % tpb, 0)))
            out_shape.append(jax.ShapeDtypeStruct((BATCH, dil, SEQ // dil, dw), BF16))
    return pl.pallas_call(
        _even_proj_kernel,
        grid=(t // tm,),
        in_specs=[pl.BlockSpec((tm, D_MODEL), lambda i: (i, 0)),
                  pl.BlockSpec((tm, 1), lambda i: (i, 0))] + [_const_spec(c.shape) for c in consts],
        out_specs=out_specs,
        out_shape=out_shape,
        scratch_shapes=[pltpu.VMEM((3, dw // LANES, tm, LANES), F32)],
        compiler_params=_params(("parallel",)),
        name="even_proj",
    )(h2d, pos2d, *consts)


def _mla_kernel(q_ref, k_ref, vt_ref, o_ref, s_a, s_b):
    i = pl.program_id(2)
    tq = ATT_TILE
    lane_q = lax.broadcasted_iota(jnp.int32, (1, MXU_DIM), 1)
    kpos = lax.broadcasted_iota(jnp.int32, (tq, tq), 0)
    qpos = lax.broadcasted_iota(jnp.int32, (tq, tq), 1)
    q = q_ref[...].astype(F32)
    qts = []
    for hh in range(2):
        nope = (lane_q >= MLA_NOPE * hh) & (lane_q < MLA_NOPE * (hh + 1))
        rope = (lane_q >= LANES + MLA_ROPE * hh) & (lane_q < LANES + MLA_ROPE * (hh + 1))
        qts.append(jnp.where(nope | rope, q, 0.0).T.astype(BF16))

    def scores(j, diagonal, slot):
        kt = k_ref[0, pl.ds(pl.multiple_of(j * tq, tq), tq), :]
        cms = []
        for hh in range(2):
            s = jnp.dot(kt, qts[hh], preferred_element_type=F32)
            if diagonal:
                s = jnp.where(kpos <= qpos, s, -jnp.inf)
            slot[hh] = s
            cms.append(jnp.max(s, axis=0, keepdims=True))
        return tuple(cms)

    def consume(j, slot, cms, carry):
        vt = vt_ref[0, :, pl.ds(pl.multiple_of(j * tq, tq), tq)]
        ps, stats = [], []
        for hh in range(2):
            m, l, _ = carry[hh]
            m_new = jnp.maximum(m, cms[hh])
            alpha = jnp.exp(m - m_new)
            p = jnp.exp(slot[hh] - m_new)
            stats.append((m_new, alpha * l + jnp.sum(p, axis=0, keepdims=True), alpha))
            ps.append(p.astype(BF16))
        new = []
        for hh in range(2):
            m_new, l, alpha = stats[hh]
            acc = alpha * carry[hh][2] + jnp.dot(vt[hh * MLA_V:(hh + 1) * MLA_V, :], ps[hh],
                                                 preferred_element_type=F32)
            new.append((m_new, l, acc))
        return tuple(new)

    init = (jnp.full((1, tq), -jnp.inf, F32), jnp.zeros((1, tq), F32), jnp.zeros((MLA_V, tq), F32))

    def step(j, state, src, dst):
        cms, carry = state
        nxt = scores(j, False, dst)
        return nxt, consume(jnp.where(j == 0, i, j - 1), src, cms, carry)

    def body(j, state):
        return lax.cond(j % 2 == 0, lambda st: step(j, st, s_a, s_b),
                        lambda st: step(j, st, s_b, s_a), state)

    state = lax.fori_loop(0, i, body, (scores(i, True, s_a), (init, init)))
    last = jnp.where(i == 0, i, i - 1)
    carry = lax.cond(i % 2 == 0, lambda st: consume(last, s_a, *st),
                     lambda st: consume(last, s_b, *st), state)
    o_t = jnp.concatenate([acc / l for (_, l, acc) in carry], axis=0)
    o_ref[...] = o_t.T.astype(BF16)


def _mla_attention(qa, ka, vat):
    t = qa.shape[0]
    nq = SEQ // ATT_TILE
    ka3 = ka.reshape(BATCH, SEQ, ka.shape[1])
    return pl.pallas_call(
        _mla_kernel,
        grid=(BATCH, MLA_HEADS // 2, nq),
        in_specs=[pl.BlockSpec((ATT_TILE, MXU_DIM), lambda b, c, i: (b * nq + i, c)),
                  pl.BlockSpec((1, SEQ, MXU_DIM), lambda b, c, i: (b, 0, c)),
                  pl.BlockSpec((1, LANES, SEQ), lambda b, c, i: (b, c, 0))],
        out_specs=pl.BlockSpec((ATT_TILE, LANES), lambda b, c, i: (b * nq + i, c)),
        out_shape=jax.ShapeDtypeStruct((t, MLA_HEADS * MLA_V), BF16),
        scratch_shapes=[pltpu.VMEM((2, ATT_TILE, ATT_TILE), F32)] * 2,
        compiler_params=_params(("parallel", "parallel", "arbitrary")),
        name="mla_attention",
    )(qa, ka3, vat)


DIL_QBLK = 2 * DIL_BLOCK
DIL_HALO = DIL_BLOCK


def _dil_kernel(q_ref, k_ref, v_ref, mb_ref, o_ref, lse_ref, vt_scr, *, nblk):
    qb, halo = DIL_QBLK, DIL_HALO
    ncls = q_ref.shape[1]
    lane = lax.broadcasted_iota(jnp.int32, (1, LANES), 1)
    low = lane < HEAD_DIM
    for g in range(ncls):
        vt_scr[g] = v_ref[0, g].astype(F32).T.astype(BF16)

    def blocks(items):
        work = []
        for g, r0, first in items:
            q = q_ref[0, g, pl.ds(r0, qb), :].astype(F32)
            if first:
                k0, nk = 0, qb
            else:
                k0 = r0 - halo if isinstance(r0, int) else pl.multiple_of(r0 - halo, halo)
                nk = qb + halo
            k = k_ref[0, g, pl.ds(k0, nk), :]
            for hh in range(2):
                qt = jnp.where(low if hh == 0 else ~low, q, 0.0).T.astype(BF16)
                s = jnp.dot(k, qt, preferred_element_type=F32) - (
                    mb_ref[0, hh, halo:, :] if first else mb_ref[0, hh])
                work.append((g, r0, k0, nk, hh, s))
        outs = []
        for g, r0, k0, nk, hh, s in work:
            m = jnp.max(s, axis=0, keepdims=True)
            p = jnp.exp(s - m)
            den = jnp.sum(p, axis=0, keepdims=True)
            vt = vt_scr[g, hh * HEAD_DIM:(hh + 1) * HEAD_DIM, pl.ds(k0, nk)]
            o = jnp.dot(vt, p.astype(BF16), preferred_element_type=F32) / den
            outs.append((o, jnp.broadcast_to(m + jnp.log(den), (HEAD_DIM, qb))))
        for n, (g, r0, _) in enumerate(items):
            pair = outs[2 * n:2 * n + 2]
            o_ref[0, g, pl.ds(r0, qb), :] = jnp.concatenate([o for o, _ in pair], axis=0).T
            lse_ref[0, g, pl.ds(r0, qb), :] = jnp.concatenate([l for _, l in pair], axis=0).T

    if nblk == 1:
        blocks([(g, 0, True) for g in range(ncls)])
    else:
        for g in range(ncls):
            blocks([(g, 0, True), (g, qb, False)])

            def body(n, carry, g=g):
                r0 = pl.multiple_of(2 * n * qb, 2 * qb)
                blocks([(g, r0, False), (g, pl.multiple_of(r0 + qb, qb), False)])
                return carry

            lax.fori_loop(1, nblk // 2, body, 0)


def _dilated_branch(qc, kc, vc, mb, dil):
    l = SEQ // dil
    dw = DIL_HEADS * HEAD_DIM
    nblk = l // DIL_QBLK
    assert nblk == 1 or nblk % 2 == 0
    ncls = min(dil, 4) if nblk == 1 else 1
    blk = pl.BlockSpec((1, ncls, l, LANES), lambda c, b, r: (b, r, 0, c))
    return pl.pallas_call(
        functools.partial(_dil_kernel, nblk=nblk),
        grid=(DIL_HEADS // 2, BATCH, dil // ncls),
        in_specs=[blk, blk, blk,
                  pl.BlockSpec((1, 2, DIL_QBLK + DIL_HALO, DIL_QBLK), lambda c, b, r: (c, 0, 0, 0))],
        out_specs=[blk, blk],
        out_shape=[jax.ShapeDtypeStruct((BATCH, dil, l, dw), F32)] * 2,
        scratch_shapes=[pltpu.VMEM((ncls, LANES, l), BF16)],
        compiler_params=_params(("parallel", "parallel", "parallel")),
        name=f"dilated_d{dil}",
    )(qc, kc, vc, mb)


def _dil_bias_table(dil, window):
    reach = window // dil
    ki = np.arange(DIL_QBLK + DIL_HALO)[:, None]
    qi = np.arange(DIL_QBLK)[None, :]
    step = DIL_HALO + qi - ki
    valid = (step >= 0) & (step <= reach)
    slopes = 2.0 ** (-8.0 * np.arange(1, DIL_HEADS + 1, dtype=np.float32) / DIL_HEADS)
    bias = slopes.astype(np.float32)[:, None, None] * (step * dil).astype(np.float32)[None]
    tab = np.where(valid[None], bias, np.inf).astype(np.float32)
    return jnp.asarray(tab.reshape(DIL_HEADS // 2, 2, DIL_QBLK + DIL_HALO, DIL_QBLK))


def _even_out_kernel(h_ref, oa_ref, *rest):
    branch_refs, w_ref, out_ref, stage_ref = rest[:-3], rest[-3], rest[-2], rest[-1]
    tm = h_ref.shape[0]
    vals = []
    for n, ((_, dil), ref) in enumerate(zip([pt for pt in DIL_PATTERNS for _ in range(2)],
                                            branch_refs)):
        if dil == 1:
            vals.append(ref[0, 0])
        else:
            slot = n - 2
            for r in range(dil):
                for c in range(ref.shape[-1] // LANES):
                    stage_ref[slot, c, pl.ds(r, tm // dil, stride=dil), :] = ref[
                        0, r, :, c * LANES:(c + 1) * LANES]
            vals.append(jnp.concatenate(
                [stage_ref[slot, c] for c in range(ref.shape[-1] // LANES)], axis=1))
    o1, l1, o2, l2, o3, l3 = vals
    lm = jnp.maximum(jnp.maximum(l1, l2), l3)
    w1, w2, w3 = jnp.exp(l1 - lm), jnp.exp(l2 - lm), jnp.exp(l3 - lm)
    mixed_b = (w1 * o1 + w2 * o2 + w3 * o3) / (w1 + w2 + w3)
    half = MLA_HEADS * MLA_V
    acc = jnp.dot(oa_ref[...], w_ref[:half, :], preferred_element_type=F32)
    acc += jnp.dot(mixed_b.astype(BF16), w_ref[half:, :], preferred_element_type=F32)
    out_ref[...] = h_ref[...] + acc


def _even_out(h2d, oa, branches, w_out):
    t = h2d.shape[0]
    tm = ROW_TILE
    tpb = SEQ // tm
    dw = DIL_HEADS * HEAD_DIM
    full = pl.BlockSpec((tm, D_MODEL), lambda i: (i, 0))
    specs, args = [], []
    for (_, dil), pair in zip(DIL_PATTERNS, branches):
        for a in pair:
            specs.append(pl.BlockSpec((1, dil, tm // dil, dw), lambda i: (i // tpb, 0, i % tpb, 0)))
            args.append(a)
    return pl.pallas_call(
        _even_out_kernel,
        grid=(t // tm,),
        in_specs=[full, pl.BlockSpec((tm, dw), lambda i: (i, 0))] + specs
        + [_const_spec(w_out.shape)],
        out_specs=full,
        out_shape=jax.ShapeDtypeStruct((t, D_MODEL), F32),
        scratch_shapes=[pltpu.VMEM((4, dw // LANES, tm, LANES), F32)],
        compiler_params=_params(("parallel",)),
        name="even_out_proj",
    )(h2d, oa, *args, w_out)


def _odd_out_kernel(h_ref, o_ref, w_ref, out_ref):
    out_ref[...] = h_ref[...] + jnp.dot(o_ref[...], w_ref[...], preferred_element_type=F32)


def _odd_out(h2d, o, w_out):
    t = h2d.shape[0]
    tm = ROW_TILE
    full = pl.BlockSpec((tm, D_MODEL), lambda i: (i, 0))
    return pl.pallas_call(
        _odd_out_kernel,
        grid=(t // tm,),
        in_specs=[full, full, _const_spec(w_out.shape)],
        out_specs=full,
        out_shape=jax.ShapeDtypeStruct((t, D_MODEL), F32),
        compiler_params=_params(("parallel",)),
        name="odd_out_proj",
    )(h2d, o, w_out)


def _ffn_kernel(h_ref, g_ref, w1_ref, w2_ref, out_ref):
    h = h_ref[...]
    n = _rms(h, g_ref[...]).astype(BF16)
    acc = h
    chunk = D_MODEL
    for c in range(D_FF // chunk):
        u = jnp.maximum(jnp.dot(n, w1_ref[:, c * chunk:(c + 1) * chunk],
                                preferred_element_type=F32), 0.0)
        acc = acc + jnp.dot((u * u).astype(BF16), w2_ref[c * chunk:(c + 1) * chunk, :],
                            preferred_element_type=F32)
    out_ref[...] = acc


def _ffn(h2d, g, w1, w2):
    t = h2d.shape[0]
    tm = ROW_TILE
    full = pl.BlockSpec((tm, D_MODEL), lambda i: (i, 0))
    return pl.pallas_call(
        _ffn_kernel,
        grid=(t // tm,),
        in_specs=[full, _const_spec(g.shape), _const_spec(w1.shape), _const_spec(w2.shape)],
        out_specs=full,
        out_shape=jax.ShapeDtypeStruct((t, D_MODEL), F32),
        compiler_params=_params(("parallel",)),
        name="ffn",
    )(h2d, g, w1, w2)


def _ple_kernel(h_ref, p_ref, g_ref, wg_ref, wp_ref, out_ref):
    h = h_ref[...]
    n = _rms(h, g_ref[...]).astype(BF16)
    gate = jax.nn.sigmoid(jnp.dot(n, wg_ref[...], preferred_element_type=F32))
    proj = jnp.dot(p_ref[...].astype(BF16), wp_ref[...], preferred_element_type=F32)
    out_ref[...] = h + gate * proj


def _ple(h2d, p2d, g, wg, wp):
    t = h2d.shape[0]
    tm = ROW_TILE
    full = pl.BlockSpec((tm, D_MODEL), lambda i: (i, 0))
    return pl.pallas_call(
        _ple_kernel,
        grid=(t // tm,),
        in_specs=[full, pl.BlockSpec((tm, PLE_DIM), lambda i: (i, 0)), _const_spec(g.shape),
                  _const_spec(wg.shape), _const_spec(wp.shape)],
        out_specs=full,
        out_shape=jax.ShapeDtypeStruct((t, D_MODEL), F32),
        compiler_params=_params(("parallel",)),
        name="ple",
    )(h2d, p2d, g, wg, wp)


def _odd_proj_kernel(x_ref, gmix_ref, win_ref, gq_ref, gk_ref, gm64_ref,
                     q_ref, k_ref, vt_ref, km_ref):
    hn = _rms(x_ref[...], gmix_ref[...]).astype(BF16)
    z = jnp.dot(hn, win_ref[...], preferred_element_type=F32)
    dw = MOBA_HEADS * HEAD_DIM
    gq = gq_ref[...]
    gk = gk_ref[...]
    tm = x_ref.shape[0]
    for c in range(dw // MXU_DIM):
        lo = c * MXU_DIM
        q_ref[:, lo:lo + MXU_DIM] = (
            _group_rms(z[:, lo:lo + MXU_DIM], gm64_ref[...]) * gq[:, lo:lo + MXU_DIM]).astype(BF16)
        k = _group_rms(z[:, dw + lo:dw + lo + MXU_DIM], gm64_ref[...]) * gk[:, lo:lo + MXU_DIM]
        k_ref[:, lo:lo + MXU_DIM] = k.astype(BF16)
        for r in range(tm // MOBA_BLOCK):
            km_ref[r, :, lo:lo + MXU_DIM] = jnp.mean(
                k[r * MOBA_BLOCK:(r + 1) * MOBA_BLOCK], axis=0, keepdims=True)
    vt_ref[0] = z[:, 2 * dw:].T.astype(BF16)


def _odd_proj(h2d, gmix, win, gq, gk, gm64):
    t = h2d.shape[0]
    tm = ROW_TILE
    dw = MOBA_HEADS * HEAD_DIM
    consts = (gmix, win, gq, gk, gm64)
    full = pl.BlockSpec((tm, dw), lambda i: (i, 0))
    tpb = SEQ // tm
    return pl.pallas_call(
        _odd_proj_kernel,
        grid=(t // tm,),
        in_specs=[pl.BlockSpec((tm, D_MODEL), lambda i: (i, 0))] + [_const_spec(c.shape) for c in consts],
        out_specs=[full, full, pl.BlockSpec((1, dw, tm), lambda i: (i // tpb, 0, i % tpb)),
                   pl.BlockSpec((tm // MOBA_BLOCK, 1, dw), lambda i: (i, 0, 0))],
        out_shape=[jax.ShapeDtypeStruct((t, dw), BF16)] * 2
        + [jax.ShapeDtypeStruct((BATCH, dw, SEQ), BF16),
           jax.ShapeDtypeStruct((t // MOBA_BLOCK, 1, dw), F32)],
        compiler_params=_params(("parallel",)),
        name="odd_proj",
    )(h2d, *consts)


def _moba_kernel(q_ref, k_ref, vt_ref, km_ref, sl_ref, o_ref, s_a, s_b):
    i = pl.program_id(2)
    tq = MOBA_BLOCK
    nb = SEQ // MOBA_BLOCK
    lane = lax.broadcasted_iota(jnp.int32, (1, LANES), 1)
    low = lane < HEAD_DIM
    kpos = lax.broadcasted_iota(jnp.int32, (tq, tq), 0)
    qpos = lax.broadcasted_iota(jnp.int32, (tq, tq), 1)
    rel = (qpos - kpos).astype(F32)
    blk_id = lax.broadcasted_iota(jnp.int32, (nb, tq), 0)
    q = q_ref[...].astype(F32)
    km = km_ref[0]
    heads = []
    for hh in range(2):
        qt = jnp.where(low if hh == 0 else ~low, q, 0.0).T
        slope = sl_ref[0, :, hh * HEAD_DIM:hh * HEAD_DIM + 1]
        gate = jnp.dot(km, qt, preferred_element_type=F32,
                       precision=lax.Precision.HIGHEST)
        g = jnp.where(blk_id < i, gate, -jnp.inf)
        sel = []
        for _ in range(MOBA_TOPK):
            mx = jnp.max(g, axis=0, keepdims=True)
            idx = jnp.min(jnp.where(g == mx, blk_id, nb), axis=0, keepdims=True)
            sel.append(jnp.where(mx > -jnp.inf, idx, -1))
            g = jnp.where(blk_id == idx, -jnp.inf, g)
        heads.append((qt.astype(BF16), slope, rel * slope, sel))

    def scores(j, own, slot):
        kt = k_ref[0, pl.ds(pl.multiple_of(j * tq, tq), tq), :]
        cms = []
        for hh in range(2):
            qt, _, relb, _ = heads[hh]
            s = jnp.dot(kt, qt, preferred_element_type=F32) - relb
            if own:
                s = jnp.where(kpos <= qpos, s, -jnp.inf)
            slot[hh] = s
            cms.append(jnp.max(s, axis=0, keepdims=True))
        return tuple(cms)

    def consume(j, slot, cms, carry):
        vt = vt_ref[0, :, pl.ds(pl.multiple_of(j * tq, tq), tq)]
        ps, stats = [], []
        for hh in range(2):
            _, slope, _, sel = heads[hh]
            m, l, _ = carry[hh]
            picked = (sel[0] == j) | (sel[1] == j) | (sel[2] == j) | (j == i)
            off = slope * ((i - j) * tq).astype(F32)
            m_new = jnp.where(picked, jnp.maximum(m, cms[hh] - off), m)
            alpha = jnp.exp(m - m_new)
            p = jnp.exp(slot[hh] - jnp.where(picked, m_new + off, jnp.inf))
            stats.append((m_new, alpha * l + jnp.sum(p, axis=0, keepdims=True), alpha))
            ps.append(p.astype(BF16))
        new = []
        for hh in range(2):
            m_new, l, alpha = stats[hh]
            acc = alpha * carry[hh][2] + jnp.dot(vt[hh * HEAD_DIM:(hh + 1) * HEAD_DIM, :], ps[hh],
                                                 preferred_element_type=F32)
            new.append((m_new, l, acc))
        return tuple(new)

    init = (jnp.full((1, tq), -jnp.inf, F32), jnp.zeros((1, tq), F32),
            jnp.zeros((HEAD_DIM, tq), F32))

    def step(j, state, src, dst):
        cms, carry = state
        nxt = scores(j, False, dst)
        return nxt, consume(jnp.where(j == 0, i, j - 1), src, cms, carry)

    def body(j, state):
        return lax.cond(j % 2 == 0, lambda st: step(j, st, s_a, s_b),
                        lambda st: step(j, st, s_b, s_a), state)

    state = lax.fori_loop(0, i, body, (scores(i, True, s_a), (init, init)))
    last = jnp.where(i == 0, i, i - 1)
    carry = lax.cond(i % 2 == 0, lambda st: consume(last, s_a, *st),
                     lambda st: consume(last, s_b, *st), state)
    o_t = jnp.concatenate([acc / l for (_, l, acc) in carry], axis=0)
    o_ref[...] = o_t.T.astype(BF16)


def _moba_attention(q, k, vt3, km, slopes):
    t = q.shape[0]
    nq = SEQ // MOBA_BLOCK
    dw = MOBA_HEADS * HEAD_DIM
    k3 = k.reshape(BATCH, SEQ, dw)
    km3 = km.reshape(BATCH, nq, dw)
    qspec = pl.BlockSpec((MOBA_BLOCK, LANES), lambda b, c, i: (b * nq + i, c))
    return pl.pallas_call(
        _moba_kernel,
        grid=(BATCH, MOBA_HEADS // 2, nq),
        in_specs=[qspec,
                  pl.BlockSpec((1, SEQ, LANES), lambda b, c, i: (b, 0, c)),
                  pl.BlockSpec((1, LANES, SEQ), lambda b, c, i: (b, c, 0)),
                  pl.BlockSpec((1, nq, LANES), lambda b, c, i: (b, 0, c)),
                  pl.BlockSpec((1, 1, LANES), lambda b, c, i: (c, 0, 0))],
        out_specs=qspec,
        out_shape=jax.ShapeDtypeStruct((t, dw), BF16),
        scratch_shapes=[pltpu.VMEM((2, MOBA_BLOCK, MOBA_BLOCK), F32)] * 2,
        compiler_params=_params(("parallel", "parallel", "arbitrary")),
        name="moba_attention",
    )(q, k3, vt3, km3, slopes)


def _group_matrix(sizes, width):
    g = np.zeros((width, width), np.float32)
    lo = 0
    for n in sizes:
        g[lo:lo + n, lo:lo + n] = 1.0 / n
        lo += n
    return jnp.asarray(g, BF16)


def _pair_slopes(n_heads):
    slopes = 2.0 ** (-8.0 * np.arange(1, n_heads + 1, dtype=np.float32) / n_heads)
    tab = np.repeat(slopes.astype(np.float32), HEAD_DIM).reshape(n_heads // 2, 1, LANES)
    return jnp.asarray(tab)


def _even_weights(w_in, w_uq, w_ukv, qn_nope, qn_rope, kn_nope, kn_rope, dil_qn, dil_kn):
    o1 = Q_LORA
    o2 = o1 + KV_LORA
    o3 = o2 + MLA_ROPE
    kr = w_in[:, o2:o3]
    pad64 = jnp.zeros((D_MODEL, LANES - 2 * MLA_ROPE), F32)
    win = jnp.concatenate([w_in[:, :o2], kr, kr, pad64, w_in[:, o3:]], axis=1).astype(BF16)

    qd = MLA_NOPE + MLA_ROPE
    a_scale = qd ** -0.5
    zq = jnp.zeros((Q_LORA, LANES - 2 * MLA_ROPE), F32)
    zg = jnp.zeros((LANES - 2 * MLA_ROPE,), F32)
    cols, gains = [], []
    for c in range(MLA_HEADS // 2):
        h0, h1 = 2 * c, 2 * c + 1
        cols += [w_uq[:, h0 * qd:h0 * qd + MLA_NOPE], w_uq[:, h1 * qd:h1 * qd + MLA_NOPE],
                 w_uq[:, h0 * qd + MLA_NOPE:(h0 + 1) * qd], w_uq[:, h1 * qd + MLA_NOPE:(h1 + 1) * qd], zq]
        gains += [qn_nope, qn_nope, qn_rope, qn_rope, zg]
    wuq = jnp.concatenate(cols, axis=1).astype(BF16)
    gq = (jnp.concatenate(gains) * a_scale)[None, :]

    kvd = MLA_NOPE + MLA_V
    kcols = [w_ukv[:, h * kvd:h * kvd + MLA_NOPE] for h in range(MLA_HEADS)]
    vcols = [w_ukv[:, h * kvd + MLA_NOPE:(h + 1) * kvd] for h in range(MLA_HEADS)]
    wukv = jnp.concatenate(kcols + vcols, axis=1).astype(BF16)
    gkn = jnp.tile(kn_nope, MLA_HEADS)[None, :]
    gkr = jnp.concatenate([kn_rope, kn_rope, zg])[None, :]
    gqb = (jnp.tile(dil_qn, DIL_HEADS) * HEAD_DIM ** -0.5)[None, :]
    gkb = jnp.tile(dil_kn, DIL_HEADS)[None, :]
    return win, wuq, wukv, gq, gkn, gkr, gqb, gkb


def kernel(x, p, positions, e_w_in, e_cq_norm, e_ckv_norm, e_w_uq, e_w_ukv, e_qn_nope, e_qn_rope, e_kn_nope, e_kn_rope, e_dil_qn, e_dil_kn, e_w_out, o_w_in, o_qn, o_kn, o_w_out, mix_norm, ff_norm, w_ff1, w_ff2, ple_norm, w_ple_gate, w_ple_proj):
    t = BATCH * SEQ
    h = x.reshape(t, D_MODEL)
    pos2d = positions.reshape(t, 1)
    p2d = p.reshape(DEPTH, t, PLE_DIM)

    gmq = _group_matrix((MLA_NOPE, MLA_NOPE, MLA_ROPE, MLA_ROPE, LANES - 2 * MLA_ROPE), MXU_DIM)
    gm64 = _group_matrix((HEAD_DIM,) * (MXU_DIM // HEAD_DIM), MXU_DIM)
    gm32 = _group_matrix((MLA_ROPE,) * (LANES // MLA_ROPE), LANES)
    half = MLA_ROPE // 2
    invf = ROPE_THETA ** (-(jnp.arange(LANES) % half).astype(F32) / half)
    invf = invf[None, :]

    for i in range(DEPTH):
        j = i // 2
        gmix = mix_norm[i][None, :]
        if i % 2 == 0:
            win, wuq, wukv, gq, gkn, gkr, gqb, gkb = _even_weights(
                e_w_in[j], e_w_uq[j], e_w_ukv[j], e_qn_nope[j], e_qn_rope[j], e_kn_nope[j],
                e_kn_rope[j], e_dil_qn[j], e_dil_kn[j])
            qa, ka, vat, *qkv_b = _even_proj(
                h, pos2d, gmix, win, e_cq_norm[j][None, :], e_ckv_norm[j][None, :], wuq, wukv,
                gq, gkn, gkr, gqb, gkb, gmq, gm64, gm32, invf)
            oa = _mla_attention(qa, ka, vat)
            branches = [_dilated_branch(*qkv_b[3 * n:3 * n + 3], _dil_bias_table(d, w), d)
                        for n, (w, d) in enumerate(DIL_PATTERNS)]
            h = _even_out(h, oa, branches, e_w_out[j].astype(BF16))
        else:
            gq = (jnp.tile(o_qn[j], MOBA_HEADS) * HEAD_DIM ** -0.5)[None, :]
            gk = jnp.tile(o_kn[j], MOBA_HEADS)[None, :]
            q, k, vt, km = _odd_proj(h, gmix, o_w_in[j].astype(BF16), gq, gk, gm64)
            o = _moba_attention(q, k, vt, km, _pair_slopes(MOBA_HEADS))
            h = _odd_out(h, o, o_w_out[j].astype(BF16))
        h = _ffn(h, ff_norm[i][None, :], w_ff1[i].astype(BF16), w_ff2[i].astype(BF16))
        h = _ple(h, p2d[i], ple_norm[i][None, :], w_ple_gate[i].astype(BF16),
                 w_ple_proj[i].astype(BF16))
    return h.reshape(BATCH, SEQ, D_MODEL)
```

```python
import functools

import numpy as np
import jax
import jax.numpy as jnp
from jax import lax
from jax.experimental import pallas as pl
from jax.experimental.pallas import tpu as pltpu

D_MODEL = 1024
BATCH = 8
SEQ = 4096
DEPTH = 2
HEAD_DIM = 64
EPS = 1e-6
MLA_HEADS = 8
MLA_NOPE = 64
MLA_ROPE = 32
MLA_V = 64
Q_LORA = 384
KV_LORA = 256
ROPE_THETA = 10000.0
DIL_HEADS = 8
DIL_PATTERNS = ((128, 1), (512, 4), (2048, 16))
DIL_BLOCK = 128
MOBA_HEADS = 16
MOBA_BLOCK = 256
MOBA_TOPK = 3
D_FF = 4 * D_MODEL
PLE_DIM = 256

LANES = 128
MXU_DIM = 256
VMEM_LIMIT = 56 * 1024 * 1024
ROW_TILE = 512
ATT_TILE = 256
ATT_TILE_Q = 512
MAX_FLOOR = -1e30
KB_PARTS = 3
LOG2E = float(np.log2(np.e))
LN2 = float(np.log(2.0))

F32 = jnp.float32
BF16 = jnp.bfloat16


def _const_spec(shape):
    nd = len(shape)
    return pl.BlockSpec(shape, lambda *_: (0,) * nd)


def _params(sem):
    return pltpu.CompilerParams(dimension_semantics=sem, vmem_limit_bytes=VMEM_LIMIT)


def _rms(x, g):
    return x * lax.rsqrt(jnp.mean(x * x, axis=-1, keepdims=True) + EPS) * g


def _group_rms(x, gmat):
    ms = jnp.dot((x * x).astype(BF16), gmat, preferred_element_type=F32)
    return x * lax.rsqrt(ms + EPS)


def _even_proj_kernel(x_ref, pos_ref, gmix_ref, win_ref, gcq_ref, gckv_ref, wuq_ref, wukv_ref,
                      gq_ref, gkn_ref, gkr_ref, gqb_ref, gkb_ref, gmq_ref, gm64_ref, gm32_ref,
                      invf_ref, qa_ref, ka_ref, vat_ref, *rest):
    dil_refs, stage_ref = rest[:-1], rest[-1]
    hn = _rms(x_ref[...], gmix_ref[...]).astype(BF16)
    z = jnp.dot(hn, win_ref[...], preferred_element_type=F32)
    o1 = Q_LORA
    o2 = o1 + KV_LORA
    o3 = o2 + LANES
    dw = DIL_HEADS * HEAD_DIM
    c_q = _rms(z[:, :o1], gcq_ref[...]).astype(BF16)
    c_kv = _rms(z[:, o1:o2], gckv_ref[...]).astype(BF16)

    ang = pos_ref[...].astype(F32) * invf_ref[...]
    cos = jnp.cos(ang)
    sin = jnp.sin(ang)
    lane = lax.broadcasted_iota(jnp.int32, (1, LANES), 1)
    first_half = (lane % MLA_ROPE) < (MLA_ROPE // 2)

    def rope(xr):
        rot = jnp.where(first_half, -pltpu.roll(xr, LANES - MLA_ROPE // 2, 1),
                        pltpu.roll(xr, MLA_ROPE // 2, 1))
        return xr * cos + rot * sin

    q = jnp.dot(c_q, wuq_ref[...], preferred_element_type=F32)
    gq = gq_ref[...]
    for c in range(MLA_HEADS // 2):
        lo = c * MXU_DIM
        qc = _group_rms(q[:, lo:lo + MXU_DIM], gmq_ref[...]) * gq[:, lo:lo + MXU_DIM]
        qa_ref[:, lo:lo + LANES] = qc[:, :LANES].astype(BF16)
        qa_ref[:, lo + LANES:lo + MXU_DIM] = rope(qc[:, LANES:]).astype(BF16)

    kv = jnp.dot(c_kv, wukv_ref[...], preferred_element_type=F32)
    k_rope = rope(_group_rms(z[:, o2:o3], gm32_ref[...]) * gkr_ref[...]).astype(BF16)
    gkn = gkn_ref[...]
    for c in range(MLA_HEADS // 2):
        lo = c * LANES
        if c % 2 == 0:
            kn2 = _group_rms(kv[:, lo:lo + MXU_DIM], gm64_ref[...]) * gkn[:, lo:lo + MXU_DIM]
        kn = kn2[:, (c % 2) * LANES:(c % 2 + 1) * LANES]
        ka_ref[:, c * MXU_DIM:c * MXU_DIM + LANES] = kn.astype(BF16)
        ka_ref[:, c * MXU_DIM + LANES:(c + 1) * MXU_DIM] = k_rope
    vat_ref[0] = kv[:, dw:].T.astype(BF16)

    gqb = gqb_ref[...]
    gkb = gkb_ref[...]
    qkv = []
    for c in range(dw // MXU_DIM):
        lo = c * MXU_DIM
        qkv.append((
            _group_rms(z[:, o3 + lo:o3 + lo + MXU_DIM], gm64_ref[...]) * gqb[:, lo:lo + MXU_DIM],
            _group_rms(z[:, o3 + dw + lo:o3 + dw + lo + MXU_DIM], gm64_ref[...])
            * gkb[:, lo:lo + MXU_DIM],
            z[:, o3 + 2 * dw + lo:o3 + 2 * dw + lo + MXU_DIM]))
    tm = x_ref.shape[0]
    for a in range(3):
        for c in range(dw // LANES):
            half = (c % 2) * LANES
            stage_ref[a, c] = qkv[c // 2][a][:, half:half + LANES]
            for (_, dil), out_ref in zip(DIL_PATTERNS, dil_refs[a::3]):
                for r in range(dil):
                    out_ref[0, r, :, c * LANES:(c + 1) * LANES] = stage_ref[
                        a, c, pl.ds(r, tm // dil, stride=dil), :].astype(BF16)


def _even_proj(h2d, pos2d, gmix, win, gcq, gckv, wuq, wukv, gq, gkn, gkr, gqb, gkb,
               gmq, gm64, gm32, invf):
    t = h2d.shape[0]
    tm = ROW_TILE
    consts = (gmix, win, gcq, gckv, wuq, wukv, gq, gkn, gkr, gqb, gkb, gmq, gm64, gm32, invf)
    tpb = SEQ // tm
    dw = DIL_HEADS * HEAD_DIM
    out_specs = [pl.BlockSpec((tm, 1024), lambda i: (i, 0)),
                 pl.BlockSpec((tm, 1024), lambda i: (i, 0)),
                 pl.BlockSpec((1, MLA_HEADS * MLA_V, tm), lambda i: (i // tpb, 0, i % tpb))]
    out_shape = [jax.ShapeDtypeStruct((t, 1024), BF16), jax.ShapeDtypeStruct((t, 1024), BF16),
                 jax.ShapeDtypeStruct((BATCH, MLA_HEADS * MLA_V, SEQ), BF16)]
    for (_, dil) in DIL_PATTERNS:
        for _ in range(3):
            out_specs.append(pl.BlockSpec((1, dil, tm // dil, dw),
                                          lambda i: (i // tpb, 0, i % tpb, 0)))
            out_shape.append(jax.ShapeDtypeStruct((BATCH, dil, SEQ // dil, dw), BF16))
    return pl.pallas_call(
        _even_proj_kernel,
        grid=(t // tm,),
        in_specs=[pl.BlockSpec((tm, D_MODEL), lambda i: (i, 0)),
                  pl.BlockSpec((tm, 1), lambda i: (i, 0))] + [_const_spec(c.shape) for c in consts],
        out_specs=out_specs,
        out_shape=out_shape,
        scratch_shapes=[pltpu.VMEM((3, dw // LANES, tm, LANES), F32)],
        compiler_params=_params(("parallel",)),
        name="even_proj",
    )(h2d, pos2d, *consts)


def _mla_kernel(q_ref, k_ref, vt_ref, o_ref, s_a, s_b):
    i = pl.program_id(2)
    tq, tk = ATT_TILE_Q, ATT_TILE
    lane_q = lax.broadcasted_iota(jnp.int32, (1, MXU_DIM), 1)
    kpos = lax.broadcasted_iota(jnp.int32, (tk, tq), 0)
    qpos = lax.broadcasted_iota(jnp.int32, (tk, tq), 1)
    q = q_ref[...].astype(F32)
    qts = []
    for hh in range(2):
        nope = (lane_q >= MLA_NOPE * hh) & (lane_q < MLA_NOPE * (hh + 1))
        rope = (lane_q >= LANES + MLA_ROPE * hh) & (lane_q < LANES + MLA_ROPE * (hh + 1))
        qts.append(jnp.where(nope | rope, q, 0.0).T.astype(BF16))

    def scores(j, diag_shift, slot):
        kt = k_ref[0, pl.ds(pl.multiple_of(j * tk, tk), tk), :]
        cms = []
        for hh in range(2):
            s = jnp.dot(kt, qts[hh], preferred_element_type=F32)
            if diag_shift is not None:
                s = jnp.where(kpos + diag_shift <= qpos, s, -jnp.inf)
            slot[hh] = s
            cms.append(jnp.max(s, axis=0, keepdims=True))
        return tuple(cms)

    def consume(j, slot, cms, carry):
        vt = vt_ref[0, :, pl.ds(pl.multiple_of(j * tk, tk), tk)]
        ps, stats = [], []
        for hh in range(2):
            m, l, _ = carry[hh]
            m_new = jnp.maximum(m, cms[hh])
            alpha = jnp.exp2(m - m_new)
            p = jnp.exp2(slot[hh] - m_new)
            stats.append((m_new, alpha * l + jnp.sum(p, axis=0, keepdims=True), alpha))
            ps.append(p.astype(BF16))
        new = []
        for hh in range(2):
            m_new, l, alpha = stats[hh]
            acc = alpha * carry[hh][2] + jnp.dot(vt[hh * MLA_V:(hh + 1) * MLA_V, :], ps[hh],
                                                 preferred_element_type=F32)
            new.append((m_new, l, acc))
        return tuple(new)

    init = (jnp.full((1, tq), -jnp.inf, F32), jnp.zeros((1, tq), F32), jnp.zeros((MLA_V, tq), F32))
    cms_a = scores(2 * i, 0, s_a)
    cms_b = scores(2 * i + 1, tk, s_b)
    carry = consume(2 * i, s_a, cms_a, (init, init))

    def body(n, state):
        cms_b, carry = state
        cms_a = scores(2 * n, None, s_a)
        carry = consume(jnp.where(n == 0, 2 * i + 1, 2 * n - 1), s_b, cms_b, carry)
        cms_b = scores(2 * n + 1, None, s_b)
        return cms_b, consume(2 * n, s_a, cms_a, carry)

    cms_b, carry = lax.fori_loop(0, i, body, (cms_b, carry))
    carry = consume(jnp.where(i == 0, 1, 2 * i - 1), s_b, cms_b, carry)
    o_t = jnp.concatenate([acc / l for (_, l, acc) in carry], axis=0)
    o_ref[...] = o_t.T.astype(BF16)


def _mla_attention(qa, ka, vat):
    t = qa.shape[0]
    nq = SEQ // ATT_TILE_Q
    ka3 = ka.reshape(BATCH, SEQ, ka.shape[1])
    return pl.pallas_call(
        _mla_kernel,
        grid=(BATCH, MLA_HEADS // 2, nq),
        in_specs=[pl.BlockSpec((ATT_TILE_Q, MXU_DIM), lambda b, c, i: (b * nq + i, c)),
                  pl.BlockSpec((1, SEQ, MXU_DIM), lambda b, c, i: (b, 0, c)),
                  pl.BlockSpec((1, LANES, SEQ), lambda b, c, i: (b, c, 0))],
        out_specs=pl.BlockSpec((ATT_TILE_Q, LANES), lambda b, c, i: (b * nq + i, c)),
        out_shape=jax.ShapeDtypeStruct((t, MLA_HEADS * MLA_V), BF16),
        scratch_shapes=[pltpu.VMEM((2, ATT_TILE, ATT_TILE_Q), F32)] * 2,
        compiler_params=_params(("parallel", "parallel", "arbitrary")),
        name="mla_attention",
    )(qa, ka3, vat)


DIL_QBLK = 2 * DIL_BLOCK
DIL_HALO = DIL_BLOCK


def _dil_kernel(q_ref, k_ref, v_ref, mb_ref, o_ref, lse_ref, vt_scr, *, nblk):
    qb, halo = DIL_QBLK, DIL_HALO
    ncls = q_ref.shape[1]
    lane = lax.broadcasted_iota(jnp.int32, (1, LANES), 1)
    low = lane < HEAD_DIM
    for g in range(ncls):
        vt_scr[g] = v_ref[0, g].astype(F32).T.astype(BF16)

    def blocks(items):
        work = []
        for g, r0, first in items:
            q = q_ref[0, g, pl.ds(r0, qb), :].astype(F32)
            if first:
                k0, nk = 0, qb
            else:
                k0 = r0 - halo if isinstance(r0, int) else pl.multiple_of(r0 - halo, halo)
                nk = qb + halo
            k = k_ref[0, g, pl.ds(k0, nk), :]
            for hh in range(2):
                qt = jnp.where(low if hh == 0 else ~low, q, 0.0).T.astype(BF16)
                s = jnp.dot(k, qt, preferred_element_type=F32) - (
                    mb_ref[0, hh, halo:, :] if first else mb_ref[0, hh])
                work.append((g, r0, k0, nk, hh, s))
        outs = []
        for g, r0, k0, nk, hh, s in work:
            m = jnp.max(s, axis=0, keepdims=True)
            p = jnp.exp2(s - m)
            den = jnp.sum(p, axis=0, keepdims=True)
            vt = vt_scr[g, hh * HEAD_DIM:(hh + 1) * HEAD_DIM, pl.ds(k0, nk)]
            o = jnp.dot(vt, p.astype(BF16), preferred_element_type=F32) / den
            lse = (m + jnp.log2(den)) * LN2
            outs.append((o, jnp.broadcast_to(lse, (HEAD_DIM, qb))))
        for n, (g, r0, _) in enumerate(items):
            pair = outs[2 * n:2 * n + 2]
            o_ref[0, g, pl.ds(r0, qb), :] = jnp.concatenate([o for o, _ in pair], axis=0).T
            lse_ref[0, g, pl.ds(r0, qb), :] = jnp.concatenate([l for _, l in pair], axis=0).T

    if nblk == 1:
        blocks([(g, 0, True) for g in range(ncls)])
    else:
        for g in range(ncls):
            blocks([(g, 0, True), (g, qb, False)])

            def body(n, carry, g=g):
                r0 = pl.multiple_of(2 * n * qb, 2 * qb)
                blocks([(g, r0, False), (g, pl.multiple_of(r0 + qb, qb), False)])
                return carry

            lax.fori_loop(1, nblk // 2, body, 0)


def _dilated_branch(qc, kc, vc, mb, dil):
    l = SEQ // dil
    dw = DIL_HEADS * HEAD_DIM
    nblk = l // DIL_QBLK
    assert nblk == 1 or nblk % 2 == 0
    ncls = min(dil, 4) if nblk == 1 else 1
    blk = pl.BlockSpec((1, ncls, l, LANES), lambda c, b, r: (b, r, 0, c))
    return pl.pallas_call(
        functools.partial(_dil_kernel, nblk=nblk),
        grid=(DIL_HEADS // 2, BATCH, dil // ncls),
        in_specs=[blk, blk, blk,
                  pl.BlockSpec((1, 2, DIL_QBLK + DIL_HALO, DIL_QBLK), lambda c, b, r: (c, 0, 0, 0))],
        out_specs=[blk, blk],
        out_shape=[jax.ShapeDtypeStruct((BATCH, dil, l, dw), F32)] * 2,
        scratch_shapes=[pltpu.VMEM((ncls, LANES, l), BF16)],
        compiler_params=_params(("parallel", "parallel", "parallel")),
        name=f"dilated_d{dil}",
    )(qc, kc, vc, mb)


def _dil_bias_table(dil, window):
    reach = window // dil
    ki = np.arange(DIL_QBLK + DIL_HALO)[:, None]
    qi = np.arange(DIL_QBLK)[None, :]
    step = DIL_HALO + qi - ki
    valid = (step >= 0) & (step <= reach)
    slopes = 2.0 ** (-8.0 * np.arange(1, DIL_HEADS + 1, dtype=np.float32) / DIL_HEADS)
    bias = slopes.astype(np.float32)[:, None, None] * (step * dil).astype(np.float32)[None]
    tab = np.where(valid[None], bias * LOG2E, np.inf).astype(np.float32)
    return jnp.asarray(tab.reshape(DIL_HEADS // 2, 2, DIL_QBLK + DIL_HALO, DIL_QBLK))


def _even_out_kernel(h_ref, oa_ref, *rest):
    branch_refs, w_ref, out_ref, stage_ref = rest[:-3], rest[-3], rest[-2], rest[-1]
    tm = h_ref.shape[0]
    vals = []
    for n, ((_, dil), ref) in enumerate(zip([pt for pt in DIL_PATTERNS for _ in range(2)],
                                            branch_refs)):
        if dil == 1:
            vals.append(ref[0, 0])
        else:
            slot = n - 2
            for r in range(dil):
                for c in range(ref.shape[-1] // LANES):
                    stage_ref[slot, c, pl.ds(r, tm // dil, stride=dil), :] = ref[
                        0, r, :, c * LANES:(c + 1) * LANES]
            vals.append(jnp.concatenate(
                [stage_ref[slot, c] for c in range(ref.shape[-1] // LANES)], axis=1))
    o1, l1, o2, l2, o3, l3 = vals
    lm = jnp.maximum(jnp.maximum(l1, l2), l3)
    w1, w2, w3 = jnp.exp(l1 - lm), jnp.exp(l2 - lm), jnp.exp(l3 - lm)
    mixed_b = (w1 * o1 + w2 * o2 + w3 * o3) / (w1 + w2 + w3)
    half = MLA_HEADS * MLA_V
    acc = jnp.dot(oa_ref[...], w_ref[:half, :], preferred_element_type=F32)
    acc += jnp.dot(mixed_b.astype(BF16), w_ref[half:, :], preferred_element_type=F32)
    out_ref[...] = h_ref[...] + acc


def _even_out(h2d, oa, branches, w_out):
    t = h2d.shape[0]
    tm = ROW_TILE
    tpb = SEQ // tm
    dw = DIL_HEADS * HEAD_DIM
    full = pl.BlockSpec((tm, D_MODEL), lambda i: (i, 0))
    specs, args = [], []
    for (_, dil), pair in zip(DIL_PATTERNS, branches):
        for a in pair:
            specs.append(pl.BlockSpec((1, dil, tm // dil, dw), lambda i: (i // tpb, 0, i % tpb, 0)))
            args.append(a)
    return pl.pallas_call(
        _even_out_kernel,
        grid=(t // tm,),
        in_specs=[full, pl.BlockSpec((tm, dw), lambda i: (i, 0))] + specs
        + [_const_spec(w_out.shape)],
        out_specs=full,
        out_shape=jax.ShapeDtypeStruct((t, D_MODEL), F32),
        scratch_shapes=[pltpu.VMEM((4, dw // LANES, tm, LANES), F32)],
        compiler_params=_params(("parallel",)),
        name="even_out_proj",
    )(h2d, oa, *args, w_out)


def _odd_out_kernel(h_ref, o_ref, w_ref, out_ref):
    out_ref[...] = h_ref[...] + jnp.dot(o_ref[...], w_ref[...], preferred_element_type=F32)


def _odd_out(h2d, o, w_out):
    t = h2d.shape[0]
    tm = ROW_TILE
    full = pl.BlockSpec((tm, D_MODEL), lambda i: (i, 0))
    return pl.pallas_call(
        _odd_out_kernel,
        grid=(t // tm,),
        in_specs=[full, full, _const_spec(w_out.shape)],
        out_specs=full,
        out_shape=jax.ShapeDtypeStruct((t, D_MODEL), F32),
        compiler_params=_params(("parallel",)),
        name="odd_out_proj",
    )(h2d, o, w_out)


def _ffn_kernel(h_ref, g_ref, w1_ref, w2_ref, out_ref):
    h = h_ref[...]
    n = _rms(h, g_ref[...]).astype(BF16)
    acc = h
    chunk = D_MODEL
    for c in range(D_FF // chunk):
        u = jnp.maximum(jnp.dot(n, w1_ref[:, c * chunk:(c + 1) * chunk],
                                preferred_element_type=F32), 0.0)
        acc = acc + jnp.dot((u * u).astype(BF16), w2_ref[c * chunk:(c + 1) * chunk, :],
                            preferred_element_type=F32)
    out_ref[...] = acc


def _ffn(h2d, g, w1, w2):
    t = h2d.shape[0]
    tm = ROW_TILE
    full = pl.BlockSpec((tm, D_MODEL), lambda i: (i, 0))
    return pl.pallas_call(
        _ffn_kernel,
        grid=(t // tm,),
        in_specs=[full, _const_spec(g.shape), _const_spec(w1.shape), _const_spec(w2.shape)],
        out_specs=full,
        out_shape=jax.ShapeDtypeStruct((t, D_MODEL), F32),
        compiler_params=_params(("parallel",)),
        name="ffn",
    )(h2d, g, w1, w2)


def _ple_kernel(h_ref, p_ref, g_ref, wg_ref, wp_ref, out_ref):
    h = h_ref[...]
    n = _rms(h, g_ref[...]).astype(BF16)
    gate = jax.nn.sigmoid(jnp.dot(n, wg_ref[...], preferred_element_type=F32))
    proj = jnp.dot(p_ref[...].astype(BF16), wp_ref[...], preferred_element_type=F32)
    out_ref[...] = h + gate * proj


def _ple(h2d, p2d, g, wg, wp):
    t = h2d.shape[0]
    tm = ROW_TILE
    full = pl.BlockSpec((tm, D_MODEL), lambda i: (i, 0))
    return pl.pallas_call(
        _ple_kernel,
        grid=(t // tm,),
        in_specs=[full, pl.BlockSpec((tm, PLE_DIM), lambda i: (i, 0)), _const_spec(g.shape),
                  _const_spec(wg.shape), _const_spec(wp.shape)],
        out_specs=full,
        out_shape=jax.ShapeDtypeStruct((t, D_MODEL), F32),
        compiler_params=_params(("parallel",)),
        name="ple",
    )(h2d, p2d, g, wg, wp)


def _odd_proj_kernel(x_ref, gmix_ref, win_ref, gq_ref, gk_ref, gm64_ref,
                     q_ref, k_ref, vt_ref, km_ref):
    hn = _rms(x_ref[...], gmix_ref[...]).astype(BF16)
    z = jnp.dot(hn, win_ref[...], preferred_element_type=F32)
    dw = MOBA_HEADS * HEAD_DIM
    gq = gq_ref[...]
    gk = gk_ref[...]
    tm = x_ref.shape[0]
    for c in range(dw // MXU_DIM):
        lo = c * MXU_DIM
        q_ref[:, lo:lo + MXU_DIM] = (
            _group_rms(z[:, lo:lo + MXU_DIM], gm64_ref[...]) * gq[:, lo:lo + MXU_DIM]).astype(BF16)
        k = _group_rms(z[:, dw + lo:dw + lo + MXU_DIM], gm64_ref[...]) * gk[:, lo:lo + MXU_DIM]
        k_ref[:, lo:lo + MXU_DIM] = k.astype(BF16)
        for r in range(tm // MOBA_BLOCK):
            km_ref[r, :, lo:lo + MXU_DIM] = jnp.mean(
                k[r * MOBA_BLOCK:(r + 1) * MOBA_BLOCK], axis=0, keepdims=True)
    vt_ref[0] = z[:, 2 * dw:].T.astype(BF16)


def _odd_proj(h2d, gmix, win, gq, gk, gm64):
    t = h2d.shape[0]
    tm = ROW_TILE
    dw = MOBA_HEADS * HEAD_DIM
    consts = (gmix, win, gq, gk, gm64)
    full = pl.BlockSpec((tm, dw), lambda i: (i, 0))
    tpb = SEQ // tm
    return pl.pallas_call(
        _odd_proj_kernel,
        grid=(t // tm,),
        in_specs=[pl.BlockSpec((tm, D_MODEL), lambda i: (i, 0))] + [_const_spec(c.shape) for c in consts],
        out_specs=[full, full, pl.BlockSpec((1, dw, tm), lambda i: (i // tpb, 0, i % tpb)),
                   pl.BlockSpec((tm // MOBA_BLOCK, 1, dw), lambda i: (i, 0, 0))],
        out_shape=[jax.ShapeDtypeStruct((t, dw), BF16)] * 2
        + [jax.ShapeDtypeStruct((BATCH, dw, SEQ), BF16),
           jax.ShapeDtypeStruct((t // MOBA_BLOCK, 1, dw), F32)],
        compiler_params=_params(("parallel",)),
        name="odd_proj",
    )(h2d, *consts)


def _moba_kernel(q_ref, k_ref, vt_ref, km_ref, sl_ref, kb_ref, o_ref, s_a, s_b):
    i = pl.program_id(2)
    tq, tk = ATT_TILE_Q, MOBA_BLOCK
    nb = SEQ // MOBA_BLOCK
    lane = lax.broadcasted_iota(jnp.int32, (1, LANES), 1)
    low = lane < HEAD_DIM
    kpos = lax.broadcasted_iota(jnp.int32, (tk, tq), 0)
    qpos = lax.broadcasted_iota(jnp.int32, (tk, tq), 1)
    row = lax.broadcasted_iota(jnp.int32, (LANES, tq), 0)
    kbias = kb_ref[0]
    blk_id = lax.broadcasted_iota(jnp.int32, (nb, tq), 0)
    own = 2 * i + lax.broadcasted_iota(jnp.int32, (1, tq), 1) // tk
    q = q_ref[...].astype(F32)
    km = km_ref[0]
    heads = []
    for hh in range(2):
        qt = jnp.where(low if hh == 0 else ~low, q, 0.0).T
        slope = sl_ref[0, :, hh * HEAD_DIM:hh * HEAD_DIM + 1]
        gate = jnp.dot(km, qt, preferred_element_type=F32,
                       precision=lax.Precision.HIGHEST)
        g = jnp.where(blk_id < own, gate, -jnp.inf)
        sel = []
        for _ in range(MOBA_TOPK):
            mx = jnp.max(g, axis=0, keepdims=True)
            idx = jnp.min(jnp.where(g == mx, blk_id, nb), axis=0, keepdims=True)
            sel.append(jnp.where(mx > -jnp.inf, idx, -1))
            g = jnp.where(blk_id == idx, -jnp.inf, g)
        ones = jnp.where((row >= KB_PARTS * hh) & (row < KB_PARTS * (hh + 1)), 1.0, 0.0)
        heads.append((jnp.concatenate([qt, ones], axis=0).astype(BF16), slope, sel))

    def scores(j, diag_shift, slot):
        kt = jnp.concatenate([k_ref[0, pl.ds(pl.multiple_of(j * tk, tk), tk), :], kbias],
                             axis=1)
        cms = []
        for hh in range(2):
            s = jnp.dot(kt, heads[hh][0], preferred_element_type=F32)
            if diag_shift is not None:
                s = jnp.where(kpos + diag_shift <= qpos, s, -jnp.inf)
            slot[hh] = s
            cms.append(jnp.max(s, axis=0, keepdims=True))
        return tuple(cms)

    def consume(j, slot, cms, carry):
        vt = vt_ref[0, :, pl.ds(pl.multiple_of(j * tk, tk), tk)]
        ps, stats = [], []
        for hh in range(2):
            _, slope, sel = heads[hh]
            m, l, _ = carry[hh]
            seen = (sel[0] == j) | (sel[1] == j) | (sel[2] == j) | (own == j)
            off = slope * ((2 * i - j) * tk).astype(F32)
            m_new = jnp.where(seen, jnp.maximum(m, cms[hh] - off), m)
            alpha = jnp.exp2(m - m_new)
            p = jnp.exp2(slot[hh] - jnp.where(seen, m_new + off, jnp.inf))
            stats.append((m_new, alpha * l + jnp.sum(p, axis=0, keepdims=True), alpha))
            ps.append(p.astype(BF16))
        new = []
        for hh in range(2):
            m_new, l, alpha = stats[hh]
            acc = alpha * carry[hh][2] + jnp.dot(vt[hh * HEAD_DIM:(hh + 1) * HEAD_DIM, :], ps[hh],
                                                 preferred_element_type=F32)
            new.append((m_new, l, acc))
        return tuple(new)

    init = (jnp.full((1, tq), MAX_FLOOR, F32), jnp.zeros((1, tq), F32),
            jnp.zeros((HEAD_DIM, tq), F32))
    cms_a = scores(2 * i, 0, s_a)
    cms_b = scores(2 * i + 1, tk, s_b)
    carry = consume(2 * i, s_a, cms_a, (init, init))

    def body(n, state):
        cms_b, carry = state
        cms_a = scores(2 * n, None, s_a)
        carry = consume(jnp.where(n == 0, 2 * i + 1, 2 * n - 1), s_b, cms_b, carry)
        cms_b = scores(2 * n + 1, None, s_b)
        return cms_b, consume(2 * n, s_a, cms_a, carry)

    cms_b, carry = lax.fori_loop(0, i, body, (cms_b, carry))
    carry = consume(jnp.where(i == 0, 1, 2 * i - 1), s_b, cms_b, carry)
    o_t = jnp.concatenate([acc / l for (_, l, acc) in carry], axis=0)
    o_ref[...] = o_t.T.astype(BF16)


def _moba_attention(q, k, vt3, km, slopes, kbias):
    t = q.shape[0]
    nq = SEQ // ATT_TILE_Q
    nb = SEQ // MOBA_BLOCK
    dw = MOBA_HEADS * HEAD_DIM
    k3 = k.reshape(BATCH, SEQ, dw)
    km3 = km.reshape(BATCH, nb, dw)
    qspec = pl.BlockSpec((ATT_TILE_Q, LANES), lambda b, c, i: (b * nq + i, c))
    return pl.pallas_call(
        _moba_kernel,
        grid=(BATCH, MOBA_HEADS // 2, nq),
        in_specs=[qspec,
                  pl.BlockSpec((1, SEQ, LANES), lambda b, c, i: (b, 0, c)),
                  pl.BlockSpec((1, LANES, SEQ), lambda b, c, i: (b, c, 0)),
                  pl.BlockSpec((1, nb, LANES), lambda b, c, i: (b, 0, c)),
                  pl.BlockSpec((1, 1, LANES), lambda b, c, i: (c, 0, 0)),
                  pl.BlockSpec((1, MOBA_BLOCK, LANES), lambda b, c, i: (c, 0, 0))],
        out_specs=qspec,
        out_shape=jax.ShapeDtypeStruct((t, dw), BF16),
        scratch_shapes=[pltpu.VMEM((2, MOBA_BLOCK, ATT_TILE_Q), F32)] * 2,
        compiler_params=_params(("parallel", "parallel", "arbitrary")),
        name="moba_attention",
    )(q, k3, vt3, km3, slopes, kbias)


def _group_matrix(sizes, width):
    g = np.zeros((width, width), np.float32)
    lo = 0
    for n in sizes:
        g[lo:lo + n, lo:lo + n] = 1.0 / n
        lo += n
    return jnp.asarray(g, BF16)


def _alibi_slopes_log2(n_heads):
    slopes = 2.0 ** (-8.0 * np.arange(1, n_heads + 1, dtype=np.float32) / n_heads)
    return slopes.astype(np.float64) * LOG2E


def _pair_slopes(n_heads):
    tab = np.repeat(_alibi_slopes_log2(n_heads).astype(np.float32), HEAD_DIM)
    return jnp.asarray(tab.reshape(n_heads // 2, 1, LANES))


def _pair_key_bias(n_heads, block):
    bias = _alibi_slopes_log2(n_heads)[:, None] * np.arange(block, dtype=np.float64)[None, :]
    tab = np.zeros((n_heads // 2, block, LANES), np.float32)
    for part in range(KB_PARTS):
        term = bias.astype(jnp.bfloat16).astype(np.float64)
        bias = bias - term
        for h in range(n_heads):
            tab[h // 2, :, KB_PARTS * (h % 2) + part] = term[h]
    return jnp.asarray(tab, BF16)


def _even_weights(w_in, w_uq, w_ukv, qn_nope, qn_rope, kn_nope, kn_rope, dil_qn, dil_kn):
    o1 = Q_LORA
    o2 = o1 + KV_LORA
    o3 = o2 + MLA_ROPE
    kr = w_in[:, o2:o3]
    pad64 = jnp.zeros((D_MODEL, LANES - 2 * MLA_ROPE), F32)
    win = jnp.concatenate([w_in[:, :o2], kr, kr, pad64, w_in[:, o3:]], axis=1).astype(BF16)

    qd = MLA_NOPE + MLA_ROPE
    a_scale = qd ** -0.5
    zq = jnp.zeros((Q_LORA, LANES - 2 * MLA_ROPE), F32)
    zg = jnp.zeros((LANES - 2 * MLA_ROPE,), F32)
    cols, gains = [], []
    for c in range(MLA_HEADS // 2):
        h0, h1 = 2 * c, 2 * c + 1
        cols += [w_uq[:, h0 * qd:h0 * qd + MLA_NOPE], w_uq[:, h1 * qd:h1 * qd + MLA_NOPE],
                 w_uq[:, h0 * qd + MLA_NOPE:(h0 + 1) * qd], w_uq[:, h1 * qd + MLA_NOPE:(h1 + 1) * qd], zq]
        gains += [qn_nope, qn_nope, qn_rope, qn_rope, zg]
    wuq = jnp.concatenate(cols, axis=1).astype(BF16)
    gq = (jnp.concatenate(gains) * (a_scale * LOG2E))[None, :]

    kvd = MLA_NOPE + MLA_V
    kcols = [w_ukv[:, h * kvd:h * kvd + MLA_NOPE] for h in range(MLA_HEADS)]
    vcols = [w_ukv[:, h * kvd + MLA_NOPE:(h + 1) * kvd] for h in range(MLA_HEADS)]
    wukv = jnp.concatenate(kcols + vcols, axis=1).astype(BF16)
    gkn = jnp.tile(kn_nope, MLA_HEADS)[None, :]
    gkr = jnp.concatenate([kn_rope, kn_rope, zg])[None, :]
    gqb = (jnp.tile(dil_qn, DIL_HEADS) * (HEAD_DIM ** -0.5 * LOG2E))[None, :]
    gkb = jnp.tile(dil_kn, DIL_HEADS)[None, :]
    return win, wuq, wukv, gq, gkn, gkr, gqb, gkb


def kernel(x, p, positions, e_w_in, e_cq_norm, e_ckv_norm, e_w_uq, e_w_ukv, e_qn_nope, e_qn_rope, e_kn_nope, e_kn_rope, e_dil_qn, e_dil_kn, e_w_out, o_w_in, o_qn, o_kn, o_w_out, mix_norm, ff_norm, w_ff1, w_ff2, ple_norm, w_ple_gate, w_ple_proj):
    t = BATCH * SEQ
    h = x.reshape(t, D_MODEL)
    pos2d = positions.reshape(t, 1)
    p2d = p.reshape(DEPTH, t, PLE_DIM)

    gmq = _group_matrix((MLA_NOPE, MLA_NOPE, MLA_ROPE, MLA_ROPE, LANES - 2 * MLA_ROPE), MXU_DIM)
    gm64 = _group_matrix((HEAD_DIM,) * (MXU_DIM // HEAD_DIM), MXU_DIM)
    gm32 = _group_matrix((MLA_ROPE,) * (LANES // MLA_ROPE), LANES)
    half = MLA_ROPE // 2
    invf = ROPE_THETA ** (-(jnp.arange(LANES) % half).astype(F32) / half)
    invf = invf[None, :]

    for i in range(DEPTH):
        j = i // 2
        gmix = mix_norm[i][None, :]
        if i % 2 == 0:
            win, wuq, wukv, gq, gkn, gkr, gqb, gkb = _even_weights(
                e_w_in[j], e_w_uq[j], e_w_ukv[j], e_qn_nope[j], e_qn_rope[j], e_kn_nope[j],
                e_kn_rope[j], e_dil_qn[j], e_dil_kn[j])
            qa, ka, vat, *qkv_b = _even_proj(
                h, pos2d, gmix, win, e_cq_norm[j][None, :], e_ckv_norm[j][None, :], wuq, wukv,
                gq, gkn, gkr, gqb, gkb, gmq, gm64, gm32, invf)
            oa = _mla_attention(qa, ka, vat)
            branches = [_dilated_branch(*qkv_b[3 * n:3 * n + 3], _dil_bias_table(d, w), d)
                        for n, (w, d) in enumerate(DIL_PATTERNS)]
            h = _even_out(h, oa, branches, e_w_out[j].astype(BF16))
        else:
            gq = (jnp.tile(o_qn[j], MOBA_HEADS) * (HEAD_DIM ** -0.5 * LOG2E))[None, :]
            gk = jnp.tile(o_kn[j], MOBA_HEADS)[None, :]
            q, k, vt, km = _odd_proj(h, gmix, o_w_in[j].astype(BF16), gq, gk, gm64)
            o = _moba_attention(q, k, vt, km, _pair_slopes(MOBA_HEADS),
                                _pair_key_bias(MOBA_HEADS, MOBA_BLOCK))
            h = _odd_out(h, o, o_w_out[j].astype(BF16))
        h = _ffn(h, ff_norm[i][None, :], w_ff1[i].astype(BF16), w_ff2[i].astype(BF16))
        h = _ple(h, p2d[i], ple_norm[i][None, :], w_ple_gate[i].astype(BF16),
                 w_ple_proj[i].astype(BF16))
    return h.reshape(BATCH, SEQ, D_MODEL)
```

```python
import functools

import numpy as np
import jax
import jax.numpy as jnp
from jax import lax
from jax.experimental import pallas as pl
from jax.experimental.pallas import tpu as pltpu

D_MODEL = 1024
BATCH = 8
SEQ = 4096
DEPTH = 2
HEAD_DIM = 64
EPS = 1e-6
MLA_HEADS = 8
MLA_NOPE = 64
MLA_ROPE = 32
MLA_V = 64
Q_LORA = 384
KV_LORA = 256
ROPE_THETA = 10000.0
DIL_HEADS = 8
DIL_PATTERNS = ((128, 1), (512, 4), (2048, 16))
DIL_BLOCK = 128
MOBA_HEADS = 16
MOBA_BLOCK = 256
MOBA_TOPK = 3
D_FF = 4 * D_MODEL
PLE_DIM = 256

LANES = 128
MXU_DIM = 256
VMEM_LIMIT = 56 * 1024 * 1024
ROW_TILE = 512
ATT_TILE = 256
ATT_TILE_Q = 512
MAX_FLOOR = -1e30
KB_PARTS = 3
KM_PARTS = 3
LOG2E = float(np.log2(np.e))
LN2 = float(np.log(2.0))

F32 = jnp.float32
BF16 = jnp.bfloat16


def _const_spec(shape, single=False):
    nd = len(shape)
    mode = {"pipeline_mode": pl.Buffered(1)} if single else {}
    return pl.BlockSpec(shape, lambda *_: (0,) * nd, **mode)


def _params(sem):
    return pltpu.CompilerParams(dimension_semantics=sem, vmem_limit_bytes=VMEM_LIMIT)


def _aligned(x, m):
    return x if isinstance(x, int) else pl.multiple_of(x, m)


def _rms(x, g):
    return x * lax.rsqrt(jnp.mean(x * x, axis=-1, keepdims=True) + EPS) * g


def _group_rms(x, gmat):
    ms = jnp.dot((x * x).astype(BF16), gmat, preferred_element_type=F32)
    return x * lax.rsqrt(ms + EPS)


def _even_proj_kernel(x_ref, pos_ref, gmix_ref, win_ref, gcq_ref, gckv_ref, wuq_ref, wukv_ref,
                      gq_ref, gkn_ref, gkr_ref, gqb_ref, gkb_ref, gmq_ref, gm64_ref, gm32_ref,
                      invf_ref, qat_ref, ka_ref, vat_ref, *rest):
    dil_refs, stage_ref = rest[:-1], rest[-1]
    hn = _rms(x_ref[...], gmix_ref[...]).astype(BF16)
    z = jnp.dot(hn, win_ref[...], preferred_element_type=F32)
    o1 = Q_LORA
    o2 = o1 + KV_LORA
    o3 = o2 + LANES
    dw = DIL_HEADS * HEAD_DIM
    c_q = _rms(z[:, :o1], gcq_ref[...]).astype(BF16)
    c_kv = _rms(z[:, o1:o2], gckv_ref[...]).astype(BF16)

    ang = pos_ref[...].astype(F32) * invf_ref[...]
    cos = jnp.cos(ang)
    sin = jnp.sin(ang)
    lane = lax.broadcasted_iota(jnp.int32, (1, LANES), 1)
    first_half = (lane % MLA_ROPE) < (MLA_ROPE // 2)

    def rope(xr):
        rot = jnp.where(first_half, -pltpu.roll(xr, LANES - MLA_ROPE // 2, 1),
                        pltpu.roll(xr, MLA_ROPE // 2, 1))
        return xr * cos + rot * sin

    q = jnp.dot(c_q, wuq_ref[...], preferred_element_type=F32)
    gq = gq_ref[...]
    for c in range(MLA_HEADS // 2):
        lo = c * MXU_DIM
        qc = _group_rms(q[:, lo:lo + MXU_DIM], gmq_ref[...]) * gq[:, lo:lo + MXU_DIM]
        qat_ref[0, lo:lo + LANES, :] = qc[:, :LANES].T.astype(BF16)
        qat_ref[0, lo + LANES:lo + MXU_DIM, :] = rope(qc[:, LANES:]).T.astype(BF16)

    kv = jnp.dot(c_kv, wukv_ref[...], preferred_element_type=F32)
    k_rope = rope(_group_rms(z[:, o2:o3], gm32_ref[...]) * gkr_ref[...]).astype(BF16)
    gkn = gkn_ref[...]
    for c in range(MLA_HEADS // 2):
        lo = c * LANES
        if c % 2 == 0:
            kn2 = _group_rms(kv[:, lo:lo + MXU_DIM], gm64_ref[...]) * gkn[:, lo:lo + MXU_DIM]
        kn = kn2[:, (c % 2) * LANES:(c % 2 + 1) * LANES]
        ka_ref[:, c * MXU_DIM:c * MXU_DIM + LANES] = kn.astype(BF16)
        ka_ref[:, c * MXU_DIM + LANES:(c + 1) * MXU_DIM] = k_rope
    vat_ref[0] = kv[:, dw:].T.astype(BF16)

    gqb = gqb_ref[...]
    gkb = gkb_ref[...]
    qkv = []
    for c in range(dw // MXU_DIM):
        lo = c * MXU_DIM
        qkv.append((
            _group_rms(z[:, o3 + lo:o3 + lo + MXU_DIM], gm64_ref[...]) * gqb[:, lo:lo + MXU_DIM],
            _group_rms(z[:, o3 + dw + lo:o3 + dw + lo + MXU_DIM], gm64_ref[...])
            * gkb[:, lo:lo + MXU_DIM],
            z[:, o3 + 2 * dw + lo:o3 + 2 * dw + lo + MXU_DIM]))
    tm = x_ref.shape[0]
    for a in range(3):
        for c in range(dw // LANES):
            half = (c % 2) * LANES
            stage_ref[a, c] = qkv[c // 2][a][:, half:half + LANES]
            for (_, dil), out_ref in zip(DIL_PATTERNS, dil_refs[a::3]):
                for r in range(dil):
                    out_ref[0, r, :, c * LANES:(c + 1) * LANES] = stage_ref[
                        a, c, pl.ds(r, tm // dil, stride=dil), :].astype(BF16)


def _even_proj(h2d, pos2d, gmix, win, gcq, gckv, wuq, wukv, gq, gkn, gkr, gqb, gkb,
               gmq, gm64, gm32, invf):
    t = h2d.shape[0]
    tm = ROW_TILE
    consts = (gmix, win, gcq, gckv, wuq, wukv, gq, gkn, gkr, gqb, gkb, gmq, gm64, gm32, invf)
    tpb = SEQ // tm
    dw = DIL_HEADS * HEAD_DIM
    out_specs = [pl.BlockSpec((1, 1024, tm), lambda i: (i // tpb, 0, i % tpb)),
                 pl.BlockSpec((tm, 1024), lambda i: (i, 0)),
                 pl.BlockSpec((1, MLA_HEADS * MLA_V, tm), lambda i: (i // tpb, 0, i % tpb))]
    out_shape = [jax.ShapeDtypeStruct((BATCH, 1024, SEQ), BF16),
                 jax.ShapeDtypeStruct((t, 1024), BF16),
                 jax.ShapeDtypeStruct((BATCH, MLA_HEADS * MLA_V, SEQ), BF16)]
    for (_, dil) in DIL_PATTERNS:
        for _ in range(3):
            out_specs.append(pl.BlockSpec((1, dil, tm // dil, dw),
                                          lambda i: (i // tpb, 0, i % tpb, 0)))
            out_shape.append(jax.ShapeDtypeStruct((BATCH, dil, SEQ // dil, dw), BF16))
    return pl.pallas_call(
        _even_proj_kernel,
        grid=(t // tm,),
        in_specs=[pl.BlockSpec((tm, D_MODEL), lambda i: (i, 0)),
                  pl.BlockSpec((tm, 1), lambda i: (i, 0))] + [_const_spec(c.shape) for c in consts],
        out_specs=out_specs,
        out_shape=out_shape,
        scratch_shapes=[pltpu.VMEM((3, dw // LANES, tm, LANES), F32)],
        compiler_params=_params(("parallel",)),
        name="even_proj",
    )(h2d, pos2d, *consts)


def _mla_kernel(qt_ref, k_ref, vt_ref, o_ref, *bufs):
    tq, tk = ATT_TILE_Q, ATT_TILE
    feat = lax.broadcasted_iota(jnp.int32, (MXU_DIM, 1), 0)
    kpos = lax.broadcasted_iota(jnp.int32, (tk, tq), 0)
    qpos = lax.broadcasted_iota(jnp.int32, (tk, tq), 1)
    head_rows = []
    for hh in range(2):
        nope = (feat >= MLA_NOPE * hh) & (feat < MLA_NOPE * (hh + 1))
        rope = (feat >= LANES + MLA_ROPE * hh) & (feat < LANES + MLA_ROPE * (hh + 1))
        head_rows.append(nope | rope)
    init = (jnp.full((1, tq), -jnp.inf, F32), jnp.zeros((1, tq), F32), jnp.zeros((MLA_V, tq), F32))

    def consume(j, slot, cms, carry):
        vt = vt_ref[0, :, pl.ds(_aligned(j * tk, tk), tk)]
        ps, stats = [], []
        for hh in range(2):
            m, l, _ = carry[hh]
            m_new = jnp.maximum(m, cms[hh])
            alpha = jnp.exp2(m - m_new)
            p = jnp.exp2(slot[hh] - m_new)
            stats.append((m_new, alpha * l + jnp.sum(p, axis=0, keepdims=True), alpha))
            ps.append(p.astype(BF16))
        new = []
        for hh in range(2):
            m_new, l, alpha = stats[hh]
            acc = alpha * carry[hh][2] + jnp.dot(vt[hh * MLA_V:(hh + 1) * MLA_V, :], ps[hh],
                                                 preferred_element_type=F32)
            new.append((m_new, l, acc))
        return tuple(new)

    def finish(i, last, slot, cms, carry):
        carry = consume(last, slot, cms, carry)
        o_t = jnp.concatenate([acc / l for (_, l, acc) in carry], axis=0)
        o_ref[i * tq:(i + 1) * tq, :] = o_t.T.astype(BF16)

    pending = None
    for i in range(SEQ // tq):
        s_a, s_b = bufs[2 * (i % 2)], bufs[2 * (i % 2) + 1]
        qt = qt_ref[0, :, i * tq:(i + 1) * tq]
        qts = [jnp.where(rows, qt, jnp.zeros_like(qt)) for rows in head_rows]

        def scores(j, diag_shift, slot, qts=qts):
            kt = k_ref[0, pl.ds(_aligned(j * tk, tk), tk), :]
            cms = []
            for hh in range(2):
                s = jnp.dot(kt, qts[hh], preferred_element_type=F32)
                if diag_shift is not None:
                    s = jnp.where(kpos + diag_shift <= qpos, s, -jnp.inf)
                slot[hh] = s
                cms.append(jnp.max(s, axis=0, keepdims=True))
            return tuple(cms)

        cms_a = scores(2 * i, 0, s_a)
        cms_b = scores(2 * i + 1, tk, s_b)
        if pending is not None:
            finish(*pending)
        carry = consume(2 * i, s_a, cms_a, (init, init))

        def body(n, state, i=i, s_a=s_a, s_b=s_b, scores=scores):
            cms_b, carry = state
            cms_a = scores(2 * n, None, s_a)
            carry = consume(jnp.where(n == 0, 2 * i + 1, 2 * n - 1), s_b, cms_b, carry)
            cms_b = scores(2 * n + 1, None, s_b)
            return cms_b, consume(2 * n, s_a, cms_a, carry)

        if i > 0:
            cms_b, carry = lax.fori_loop(0, i, body, (cms_b, carry))
        pending = (i, 1 if i == 0 else 2 * i - 1, s_b, cms_b, carry)
    finish(*pending)


def _mla_attention(qat, ka, vat):
    t = ka.shape[0]
    ka3 = ka.reshape(BATCH, SEQ, ka.shape[1])
    return pl.pallas_call(
        _mla_kernel,
        grid=(BATCH, MLA_HEADS // 2),
        in_specs=[pl.BlockSpec((1, MXU_DIM, SEQ), lambda b, c: (b, c, 0)),
                  pl.BlockSpec((1, SEQ, MXU_DIM), lambda b, c: (b, 0, c)),
                  pl.BlockSpec((1, LANES, SEQ), lambda b, c: (b, c, 0))],
        out_specs=pl.BlockSpec((SEQ, LANES), lambda b, c: (b, c)),
        out_shape=jax.ShapeDtypeStruct((t, MLA_HEADS * MLA_V), BF16),
        scratch_shapes=[pltpu.VMEM((2, ATT_TILE, ATT_TILE_Q), F32)] * 4,
        compiler_params=_params(("parallel", "parallel")),
        name="mla_attention",
    )(qat, ka3, vat)


DIL_QBLK = 2 * DIL_BLOCK
DIL_HALO = DIL_BLOCK


def _dil_kernel(q_ref, k_ref, v_ref, mb_ref, o_ref, lse_ref, vt_scr, *, nblk):
    qb, halo = DIL_QBLK, DIL_HALO
    ncls = q_ref.shape[1]
    lane = lax.broadcasted_iota(jnp.int32, (1, LANES), 1)
    low = lane < HEAD_DIM
    for g in range(ncls):
        vt_scr[g] = v_ref[0, g].astype(F32).T.astype(BF16)

    def blocks(items):
        work = []
        for g, r0, first in items:
            q = q_ref[0, g, pl.ds(r0, qb), :].astype(F32)
            if first:
                k0, nk = 0, qb
            else:
                k0 = r0 - halo if isinstance(r0, int) else pl.multiple_of(r0 - halo, halo)
                nk = qb + halo
            k = k_ref[0, g, pl.ds(k0, nk), :]
            for hh in range(2):
                qt = jnp.where(low if hh == 0 else ~low, q, 0.0).T.astype(BF16)
                s = jnp.dot(k, qt, preferred_element_type=F32) - (
                    mb_ref[0, hh, halo:, :] if first else mb_ref[0, hh])
                work.append((g, r0, k0, nk, hh, s))
        outs = []
        for g, r0, k0, nk, hh, s in work:
            m = jnp.max(s, axis=0, keepdims=True)
            p = jnp.exp2(s - m)
            den = jnp.sum(p, axis=0, keepdims=True)
            vt = vt_scr[g, hh * HEAD_DIM:(hh + 1) * HEAD_DIM, pl.ds(k0, nk)]
            o = jnp.dot(vt, p.astype(BF16), preferred_element_type=F32) / den
            lse = (m + jnp.log2(den)) * LN2
            outs.append((o, jnp.broadcast_to(lse, (HEAD_DIM, qb))))
        for n, (g, r0, _) in enumerate(items):
            pair = outs[2 * n:2 * n + 2]
            o_ref[0, g, pl.ds(r0, qb), :] = jnp.concatenate([o for o, _ in pair], axis=0).T
            lse_ref[0, g, pl.ds(r0, qb), :] = jnp.concatenate([l for _, l in pair], axis=0).T

    if nblk == 1:
        blocks([(g, 0, True) for g in range(ncls)])
    else:
        for g in range(ncls):
            blocks([(g, 0, True), (g, qb, False)])

            def body(n, carry, g=g):
                r0 = pl.multiple_of(2 * n * qb, 2 * qb)
                blocks([(g, r0, False), (g, pl.multiple_of(r0 + qb, qb), False)])
                return carry

            lax.fori_loop(1, nblk // 2, body, 0)


def _dilated_branch(qc, kc, vc, mb, dil):
    l = SEQ // dil
    dw = DIL_HEADS * HEAD_DIM
    nblk = l // DIL_QBLK
    assert nblk == 1 or nblk % 2 == 0
    ncls = min(dil, 4) if nblk == 1 else 1
    blk = pl.BlockSpec((1, ncls, l, LANES), lambda c, b, r: (b, r, 0, c))
    return pl.pallas_call(
        functools.partial(_dil_kernel, nblk=nblk),
        grid=(DIL_HEADS // 2, BATCH, dil // ncls),
        in_specs=[blk, blk, blk,
                  pl.BlockSpec((1, 2, DIL_QBLK + DIL_HALO, DIL_QBLK), lambda c, b, r: (c, 0, 0, 0))],
        out_specs=[blk, blk],
        out_shape=[jax.ShapeDtypeStruct((BATCH, dil, l, dw), F32)] * 2,
        scratch_shapes=[pltpu.VMEM((ncls, LANES, l), BF16)],
        compiler_params=_params(("parallel", "parallel", "parallel")),
        name=f"dilated_d{dil}",
    )(qc, kc, vc, mb)


def _dil_bias_table(dil, window):
    reach = window // dil
    ki = np.arange(DIL_QBLK + DIL_HALO)[:, None]
    qi = np.arange(DIL_QBLK)[None, :]
    step = DIL_HALO + qi - ki
    valid = (step >= 0) & (step <= reach)
    slopes = 2.0 ** (-8.0 * np.arange(1, DIL_HEADS + 1, dtype=np.float32) / DIL_HEADS)
    bias = slopes.astype(np.float32)[:, None, None] * (step * dil).astype(np.float32)[None]
    tab = np.where(valid[None], bias * LOG2E, np.inf).astype(np.float32)
    return jnp.asarray(tab.reshape(DIL_HEADS // 2, 2, DIL_QBLK + DIL_HALO, DIL_QBLK))


def _even_out_kernel(h_ref, oa_ref, *rest):
    branch_refs, w_ref, out_ref, stage_ref = rest[:-3], rest[-3], rest[-2], rest[-1]
    tm = h_ref.shape[0]
    vals = []
    for n, ((_, dil), ref) in enumerate(zip([pt for pt in DIL_PATTERNS for _ in range(2)],
                                            branch_refs)):
        if dil == 1:
            vals.append(ref[0, 0])
        else:
            slot = n - 2
            for r in range(dil):
                for c in range(ref.shape[-1] // LANES):
                    stage_ref[slot, c, pl.ds(r, tm // dil, stride=dil), :] = ref[
                        0, r, :, c * LANES:(c + 1) * LANES]
            vals.append(jnp.concatenate(
                [stage_ref[slot, c] for c in range(ref.shape[-1] // LANES)], axis=1))
    o1, l1, o2, l2, o3, l3 = vals
    lm = jnp.maximum(jnp.maximum(l1, l2), l3)
    w1, w2, w3 = jnp.exp(l1 - lm), jnp.exp(l2 - lm), jnp.exp(l3 - lm)
    mixed_b = (w1 * o1 + w2 * o2 + w3 * o3) / (w1 + w2 + w3)
    half = MLA_HEADS * MLA_V
    acc = jnp.dot(oa_ref[...], w_ref[:half, :], preferred_element_type=F32)
    acc += jnp.dot(mixed_b.astype(BF16), w_ref[half:, :], preferred_element_type=F32)
    out_ref[...] = h_ref[...] + acc


def _even_out(h2d, oa, branches, w_out):
    t = h2d.shape[0]
    tm = ROW_TILE
    tpb = SEQ // tm
    dw = DIL_HEADS * HEAD_DIM
    full = pl.BlockSpec((tm, D_MODEL), lambda i: (i, 0))
    specs, args = [], []
    for (_, dil), pair in zip(DIL_PATTERNS, branches):
        for a in pair:
            specs.append(pl.BlockSpec((1, dil, tm // dil, dw), lambda i: (i // tpb, 0, i % tpb, 0)))
            args.append(a)
    return pl.pallas_call(
        _even_out_kernel,
        grid=(t // tm,),
        in_specs=[full, pl.BlockSpec((tm, dw), lambda i: (i, 0))] + specs
        + [_const_spec(w_out.shape)],
        out_specs=full,
        out_shape=jax.ShapeDtypeStruct((t, D_MODEL), F32),
        scratch_shapes=[pltpu.VMEM((4, dw // LANES, tm, LANES), F32)],
        compiler_params=_params(("parallel",)),
        name="even_out_proj",
    )(h2d, oa, *args, w_out)


def _tail_kernel(*refs, with_out_proj):
    if with_out_proj:
        h_ref, o_ref, wout_ref, *refs = refs
    else:
        h_ref, *refs = refs
    p_ref, gff_ref, w1_ref, w2_ref, gple_ref, wg_ref, wp_ref, out_ref = refs
    h = h_ref[...]
    if with_out_proj:
        h = h + jnp.dot(o_ref[...], wout_ref[...], preferred_element_type=F32)
    n = _rms(h, gff_ref[...]).astype(BF16)
    chunk = D_MODEL
    for c in range(D_FF // chunk):
        u = jnp.maximum(jnp.dot(n, w1_ref[:, c * chunk:(c + 1) * chunk],
                                preferred_element_type=F32), 0.0)
        h = h + jnp.dot((u * u).astype(BF16), w2_ref[c * chunk:(c + 1) * chunk, :],
                        preferred_element_type=F32)
    n = _rms(h, gple_ref[...]).astype(BF16)
    gate = jax.nn.sigmoid(jnp.dot(n, wg_ref[...], preferred_element_type=F32))
    proj = jnp.dot(p_ref[...].astype(BF16), wp_ref[...], preferred_element_type=F32)
    out_ref[...] = h + gate * proj


def _tail(h2d, p2d, gff, w1, w2, gple, wg, wp, mix=None, w_out=None):
    t = h2d.shape[0]
    tm = ROW_TILE
    full = pl.BlockSpec((tm, D_MODEL), lambda i: (i, 0))
    with_out = mix is not None
    head_specs = [full, full, _const_spec(w_out.shape, single=True)] if with_out else [full]
    head_args = (h2d, mix, w_out) if with_out else (h2d,)
    consts = (gff, w1, w2, gple, wg, wp)
    return pl.pallas_call(
        functools.partial(_tail_kernel, with_out_proj=with_out),
        grid=(t // tm,),
        in_specs=head_specs + [pl.BlockSpec((tm, PLE_DIM), lambda i: (i, 0))]
        + [_const_spec(c.shape, single=True) for c in consts],
        out_specs=full,
        out_shape=jax.ShapeDtypeStruct((t, D_MODEL), F32),
        compiler_params=_params(("parallel",)),
        name="tail_with_out_proj" if with_out else "tail",
    )(*head_args, p2d, *consts)


def _odd_proj_kernel(x_ref, gmix_ref, win_ref, gq_ref, gk_ref, gm64_ref,
                     qt_ref, k_ref, vt_ref, km_ref):
    hn = _rms(x_ref[...], gmix_ref[...]).astype(BF16)
    z = jnp.dot(hn, win_ref[...], preferred_element_type=F32)
    dw = MOBA_HEADS * HEAD_DIM
    gq = gq_ref[...]
    gk = gk_ref[...]
    tm = x_ref.shape[0]
    for c in range(dw // MXU_DIM):
        lo = c * MXU_DIM
        qt_ref[0, lo:lo + MXU_DIM, :] = (
            _group_rms(z[:, lo:lo + MXU_DIM], gm64_ref[...]) * gq[:, lo:lo + MXU_DIM]
        ).T.astype(BF16)
        k = _group_rms(z[:, dw + lo:dw + lo + MXU_DIM], gm64_ref[...]) * gk[:, lo:lo + MXU_DIM]
        k_ref[:, lo:lo + MXU_DIM] = k.astype(BF16)
        for r in range(tm // MOBA_BLOCK):
            km_ref[r, :, lo:lo + MXU_DIM] = jnp.mean(
                k[r * MOBA_BLOCK:(r + 1) * MOBA_BLOCK], axis=0, keepdims=True)
    vt_ref[0] = z[:, 2 * dw:].T.astype(BF16)


def _odd_proj(h2d, gmix, win, gq, gk, gm64):
    t = h2d.shape[0]
    tm = ROW_TILE
    dw = MOBA_HEADS * HEAD_DIM
    consts = (gmix, win, gq, gk, gm64)
    full = pl.BlockSpec((tm, dw), lambda i: (i, 0))
    tpb = SEQ // tm
    tspec = pl.BlockSpec((1, dw, tm), lambda i: (i // tpb, 0, i % tpb))
    return pl.pallas_call(
        _odd_proj_kernel,
        grid=(t // tm,),
        in_specs=[pl.BlockSpec((tm, D_MODEL), lambda i: (i, 0))] + [_const_spec(c.shape) for c in consts],
        out_specs=[tspec, full, tspec,
                   pl.BlockSpec((tm // MOBA_BLOCK, 1, dw), lambda i: (i, 0, 0))],
        out_shape=[jax.ShapeDtypeStruct((BATCH, dw, SEQ), BF16), jax.ShapeDtypeStruct((t, dw), BF16),
                   jax.ShapeDtypeStruct((BATCH, dw, SEQ), BF16),
                   jax.ShapeDtypeStruct((t // MOBA_BLOCK, 1, dw), F32)],
        compiler_params=_params(("parallel",)),
        name="odd_proj",
    )(h2d, *consts)


def _moba_kernel(qt_ref, k_ref, vt_ref, km_ref, sl_ref, kb_ref, o_ref, *bufs):
    tq, tk = ATT_TILE_Q, MOBA_BLOCK
    nb = SEQ // MOBA_BLOCK
    kpos = lax.broadcasted_iota(jnp.int32, (tk, tq), 0)
    qpos = lax.broadcasted_iota(jnp.int32, (tk, tq), 1)
    row = lax.broadcasted_iota(jnp.int32, (LANES, tq), 0)
    blk_id = lax.broadcasted_iota(jnp.int32, (nb, tq), 0)
    half = lax.broadcasted_iota(jnp.int32, (1, tq), 1) // tk
    kbias = kb_ref[0]
    slopes = [sl_ref[0, :, hh * HEAD_DIM:hh * HEAD_DIM + 1] for hh in range(2)]
    km_terms, resid = [], km_ref[0]
    for _ in range(KM_PARTS):
        km_terms.append(resid.astype(BF16))
        resid = resid - km_terms[-1].astype(F32)
    km_cat = jnp.concatenate(km_terms, axis=0)
    km_cat = jnp.concatenate([km_cat, jnp.zeros_like(km_cat)], axis=1)
    init = (jnp.full((1, tq), MAX_FLOOR, F32), jnp.zeros((1, tq), F32),
            jnp.zeros((HEAD_DIM, tq), F32))

    def prepare(i):
        own = 2 * i + half
        q_t = qt_ref[0, :, i * tq:(i + 1) * tq]
        heads = []
        for hh in range(2):
            qt = jnp.where((row >= HEAD_DIM * hh) & (row < HEAD_DIM * (hh + 1)), q_t,
                           jnp.zeros_like(q_t))
            ones = jnp.where((row >= KB_PARTS * hh) & (row < KB_PARTS * (hh + 1)), 1.0, 0.0)
            qt = jnp.concatenate([qt, ones.astype(BF16)], axis=0)
            gate_terms = jnp.dot(km_cat, qt, preferred_element_type=F32)
            gate = sum(gate_terms[n * nb:(n + 1) * nb] for n in range(KM_PARTS))
            g = jnp.where(blk_id < own, gate, -jnp.inf)
            sel = []
            for _ in range(MOBA_TOPK):
                mx = jnp.max(g, axis=0, keepdims=True)
                idx = jnp.min(jnp.where(g == mx, blk_id, nb), axis=0, keepdims=True)
                sel.append(jnp.where(mx > -jnp.inf, idx, -1))
                g = jnp.where(blk_id == idx, -jnp.inf, g)
            heads.append((qt, sel))
        return heads, own

    def scores(tile, j, diag_shift, slot):
        heads, _ = tile
        kt = jnp.concatenate([k_ref[0, pl.ds(_aligned(j * tk, tk), tk), :], kbias],
                             axis=1)
        cms = []
        for hh in range(2):
            s = jnp.dot(kt, heads[hh][0], preferred_element_type=F32)
            if diag_shift is not None:
                s = jnp.where(kpos + diag_shift <= qpos, s, -jnp.inf)
            slot[hh] = s
            cms.append(jnp.max(s, axis=0, keepdims=True))
        return tuple(cms)

    def consume(i, tile, j, slot, cms, carry):
        heads, own = tile
        vt = vt_ref[0, :, pl.ds(_aligned(j * tk, tk), tk)]
        ps, stats = [], []
        for hh in range(2):
            sel = heads[hh][1]
            m, l, _ = carry[hh]
            seen = (sel[0] == j) | (sel[1] == j) | (sel[2] == j) | (own == j)
            off = slopes[hh] * jnp.asarray((2 * i - j) * tk, F32)
            m_new = jnp.where(seen, jnp.maximum(m, cms[hh] - off), m)
            alpha = jnp.exp2(m - m_new)
            p = jnp.exp2(slot[hh] - jnp.where(seen, m_new + off, jnp.inf))
            stats.append((m_new, alpha * l + jnp.sum(p, axis=0, keepdims=True), alpha))
            ps.append(p.astype(BF16))
        new = []
        for hh in range(2):
            m_new, l, alpha = stats[hh]
            acc = alpha * carry[hh][2] + jnp.dot(vt[hh * HEAD_DIM:(hh + 1) * HEAD_DIM, :], ps[hh],
                                                 preferred_element_type=F32)
            new.append((m_new, l, acc))
        return tuple(new)

    def finish(i, tile, last, slot, cms, carry):
        carry = consume(i, tile, last, slot, cms, carry)
        o_t = jnp.concatenate([acc / l for (_, l, acc) in carry], axis=0)
        o_ref[i * tq:(i + 1) * tq, :] = o_t.T.astype(BF16)

    pending = None
    for i in range(SEQ // tq):
        s_a, s_b = bufs[2 * (i % 2)], bufs[2 * (i % 2) + 1]
        tile = prepare(i)
        cms_a = scores(tile, 2 * i, 0, s_a)
        cms_b = scores(tile, 2 * i + 1, tk, s_b)
        if pending is not None:
            finish(*pending)
        carry = consume(i, tile, 2 * i, s_a, cms_a, (init, init))

        def body(n, state, i=i, tile=tile, s_a=s_a, s_b=s_b):
            cms_b, carry = state
            cms_a = scores(tile, 2 * n, None, s_a)
            carry = consume(i, tile, jnp.where(n == 0, 2 * i + 1, 2 * n - 1), s_b, cms_b, carry)
            cms_b = scores(tile, 2 * n + 1, None, s_b)
            return cms_b, consume(i, tile, 2 * n, s_a, cms_a, carry)

        if i > 0:
            cms_b, carry = lax.fori_loop(0, i, body, (cms_b, carry))
        pending = (i, tile, 1 if i == 0 else 2 * i - 1, s_b, cms_b, carry)
    finish(*pending)


def _moba_attention(qt3, k, vt3, km, slopes, kbias):
    t = k.shape[0]
    nb = SEQ // MOBA_BLOCK
    dw = MOBA_HEADS * HEAD_DIM
    k3 = k.reshape(BATCH, SEQ, dw)
    km3 = km.reshape(BATCH, nb, dw)
    tspec = pl.BlockSpec((1, LANES, SEQ), lambda b, c: (b, c, 0))
    return pl.pallas_call(
        _moba_kernel,
        grid=(BATCH, MOBA_HEADS // 2),
        in_specs=[tspec,
                  pl.BlockSpec((1, SEQ, LANES), lambda b, c: (b, 0, c)),
                  tspec,
                  pl.BlockSpec((1, nb, LANES), lambda b, c: (b, 0, c)),
                  pl.BlockSpec((1, 1, LANES), lambda b, c: (c, 0, 0)),
                  pl.BlockSpec((1, MOBA_BLOCK, LANES), lambda b, c: (c, 0, 0))],
        out_specs=pl.BlockSpec((SEQ, LANES), lambda b, c: (b, c)),
        out_shape=jax.ShapeDtypeStruct((t, dw), BF16),
        scratch_shapes=[pltpu.VMEM((2, MOBA_BLOCK, ATT_TILE_Q), F32)] * 4,
        compiler_params=_params(("parallel", "parallel")),
        name="moba_attention",
    )(qt3, k3, vt3, km3, slopes, kbias)


def _group_matrix(sizes, width):
    g = np.zeros((width, width), np.float32)
    lo = 0
    for n in sizes:
        g[lo:lo + n, lo:lo + n] = 1.0 / n
        lo += n
    return jnp.asarray(g, BF16)


def _alibi_slopes_log2(n_heads):
    slopes = 2.0 ** (-8.0 * np.arange(1, n_heads + 1, dtype=np.float32) / n_heads)
    return slopes.astype(np.float64) * LOG2E


def _pair_slopes(n_heads):
    tab = np.repeat(_alibi_slopes_log2(n_heads).astype(np.float32), HEAD_DIM)
    return jnp.asarray(tab.reshape(n_heads // 2, 1, LANES))


def _pair_key_bias(n_heads, block):
    bias = _alibi_slopes_log2(n_heads)[:, None] * np.arange(block, dtype=np.float64)[None, :]
    tab = np.zeros((n_heads // 2, block, LANES), np.float32)
    for part in range(KB_PARTS):
        term = bias.astype(jnp.bfloat16).astype(np.float64)
        bias = bias - term
        for h in range(n_heads):
            tab[h // 2, :, KB_PARTS * (h % 2) + part] = term[h]
    return jnp.asarray(tab, BF16)


def _even_weights(w_in, w_uq, w_ukv, qn_nope, qn_rope, kn_nope, kn_rope, dil_qn, dil_kn):
    o1 = Q_LORA
    o2 = o1 + KV_LORA
    o3 = o2 + MLA_ROPE
    kr = w_in[:, o2:o3]
    pad64 = jnp.zeros((D_MODEL, LANES - 2 * MLA_ROPE), F32)
    win = jnp.concatenate([w_in[:, :o2], kr, kr, pad64, w_in[:, o3:]], axis=1).astype(BF16)

    qd = MLA_NOPE + MLA_ROPE
    a_scale = qd ** -0.5
    zq = jnp.zeros((Q_LORA, LANES - 2 * MLA_ROPE), F32)
    zg = jnp.zeros((LANES - 2 * MLA_ROPE,), F32)
    cols, gains = [], []
    for c in range(MLA_HEADS // 2):
        h0, h1 = 2 * c, 2 * c + 1
        cols += [w_uq[:, h0 * qd:h0 * qd + MLA_NOPE], w_uq[:, h1 * qd:h1 * qd + MLA_NOPE],
                 w_uq[:, h0 * qd + MLA_NOPE:(h0 + 1) * qd], w_uq[:, h1 * qd + MLA_NOPE:(h1 + 1) * qd], zq]
        gains += [qn_nope, qn_nope, qn_rope, qn_rope, zg]
    wuq = jnp.concatenate(cols, axis=1).astype(BF16)
    gq = (jnp.concatenate(gains) * (a_scale * LOG2E))[None, :]

    kvd = MLA_NOPE + MLA_V
    kcols = [w_ukv[:, h * kvd:h * kvd + MLA_NOPE] for h in range(MLA_HEADS)]
    vcols = [w_ukv[:, h * kvd + MLA_NOPE:(h + 1) * kvd] for h in range(MLA_HEADS)]
    wukv = jnp.concatenate(kcols + vcols, axis=1).astype(BF16)
    gkn = jnp.tile(kn_nope, MLA_HEADS)[None, :]
    gkr = jnp.concatenate([kn_rope, kn_rope, zg])[None, :]
    gqb = (jnp.tile(dil_qn, DIL_HEADS) * (HEAD_DIM ** -0.5 * LOG2E))[None, :]
    gkb = jnp.tile(dil_kn, DIL_HEADS)[None, :]
    return win, wuq, wukv, gq, gkn, gkr, gqb, gkb


def kernel(x, p, positions, e_w_in, e_cq_norm, e_ckv_norm, e_w_uq, e_w_ukv, e_qn_nope, e_qn_rope, e_kn_nope, e_kn_rope, e_dil_qn, e_dil_kn, e_w_out, o_w_in, o_qn, o_kn, o_w_out, mix_norm, ff_norm, w_ff1, w_ff2, ple_norm, w_ple_gate, w_ple_proj):
    t = BATCH * SEQ
    h = x.reshape(t, D_MODEL)
    pos2d = positions.reshape(t, 1)
    p2d = p.reshape(DEPTH, t, PLE_DIM)

    gmq = _group_matrix((MLA_NOPE, MLA_NOPE, MLA_ROPE, MLA_ROPE, LANES - 2 * MLA_ROPE), MXU_DIM)
    gm64 = _group_matrix((HEAD_DIM,) * (MXU_DIM // HEAD_DIM), MXU_DIM)
    gm32 = _group_matrix((MLA_ROPE,) * (LANES // MLA_ROPE), LANES)
    half = MLA_ROPE // 2
    invf = ROPE_THETA ** (-(jnp.arange(LANES) % half).astype(F32) / half)
    invf = invf[None, :]

    for i in range(DEPTH):
        j = i // 2
        gmix = mix_norm[i][None, :]
        if i % 2 == 0:
            win, wuq, wukv, gq, gkn, gkr, gqb, gkb = _even_weights(
                e_w_in[j], e_w_uq[j], e_w_ukv[j], e_qn_nope[j], e_qn_rope[j], e_kn_nope[j],
                e_kn_rope[j], e_dil_qn[j], e_dil_kn[j])
            qa, ka, vat, *qkv_b = _even_proj(
                h, pos2d, gmix, win, e_cq_norm[j][None, :], e_ckv_norm[j][None, :], wuq, wukv,
                gq, gkn, gkr, gqb, gkb, gmq, gm64, gm32, invf)
            oa = _mla_attention(qa, ka, vat)
            branches = [_dilated_branch(*qkv_b[3 * n:3 * n + 3], _dil_bias_table(d, w), d)
                        for n, (w, d) in enumerate(DIL_PATTERNS)]
            h = _even_out(h, oa, branches, e_w_out[j].astype(BF16))
            mix = {}
        else:
            gq = (jnp.tile(o_qn[j], MOBA_HEADS) * (HEAD_DIM ** -0.5 * LOG2E))[None, :]
            gk = jnp.tile(o_kn[j], MOBA_HEADS)[None, :]
            q, k, vt, km = _odd_proj(h, gmix, o_w_in[j].astype(BF16), gq, gk, gm64)
            o = _moba_attention(q, k, vt, km, _pair_slopes(MOBA_HEADS),
                                _pair_key_bias(MOBA_HEADS, MOBA_BLOCK))
            mix = {"mix": o, "w_out": o_w_out[j].astype(BF16)}
        h = _tail(h, p2d[i], ff_norm[i][None, :], w_ff1[i].astype(BF16), w_ff2[i].astype(BF16),
                  ple_norm[i][None, :], w_ple_gate[i].astype(BF16), w_ple_proj[i].astype(BF16),
                  **mix)
    return h.reshape(BATCH, SEQ, D_MODEL)
```

```python
import functools

import numpy as np
import jax
import jax.numpy as jnp
from jax import lax
from jax.experimental import pallas as pl
from jax.experimental.pallas import tpu as pltpu

D_MODEL = 1024
BATCH = 8
SEQ = 4096
DEPTH = 2
HEAD_DIM = 64
EPS = 1e-6
MLA_HEADS = 8
MLA_NOPE = 64
MLA_ROPE = 32
MLA_V = 64
Q_LORA = 384
KV_LORA = 256
ROPE_THETA = 10000.0
DIL_HEADS = 8
DIL_PATTERNS = ((128, 1), (512, 4), (2048, 16))
DIL_BLOCK = 128
MOBA_HEADS = 16
MOBA_BLOCK = 256
MOBA_TOPK = 3
D_FF = 4 * D_MODEL
PLE_DIM = 256

LANES = 128
MXU_DIM = 256
VMEM_LIMIT = 56 * 1024 * 1024
ROW_TILE = 512
ATT_TILE = 256
ATT_TILE_Q = 512
MAX_FLOOR = -1e30
KB_PARTS = 3
KM_PARTS = 3
LOG2E = float(np.log2(np.e))
LN2 = float(np.log(2.0))

F32 = jnp.float32
BF16 = jnp.bfloat16


def _const_spec(shape, single=False):
    nd = len(shape)
    mode = {"pipeline_mode": pl.Buffered(1)} if single else {}
    return pl.BlockSpec(shape, lambda *_: (0,) * nd, **mode)


def _params(sem):
    return pltpu.CompilerParams(dimension_semantics=sem, vmem_limit_bytes=VMEM_LIMIT)


def _aligned(x, m):
    return x if isinstance(x, int) else pl.multiple_of(x, m)


def _rms(x, g):
    return x * lax.rsqrt(jnp.mean(x * x, axis=-1, keepdims=True) + EPS) * g


def _group_rms(x, gmat):
    ms = jnp.dot((x * x).astype(BF16), gmat, preferred_element_type=F32)
    return x * lax.rsqrt(ms + EPS)


def _even_proj_kernel(x_ref, pos_ref, gmix_ref, win_ref, gcq_ref, gckv_ref, wuq_ref, wukv_ref,
                      gq_ref, gkn_ref, gkr_ref, gqb_ref, gkb_ref, gmq_ref, gm64_ref, gm32_ref,
                      invf_ref, qat_ref, ka_ref, vat_ref, *rest):
    dil_refs, stage_ref = rest[:-1], rest[-1]
    hn = _rms(x_ref[...], gmix_ref[...]).astype(BF16)
    z = jnp.dot(hn, win_ref[...], preferred_element_type=F32)
    o1 = Q_LORA
    o2 = o1 + KV_LORA
    o3 = o2 + LANES
    dw = DIL_HEADS * HEAD_DIM
    c_q = _rms(z[:, :o1], gcq_ref[...]).astype(BF16)
    c_kv = _rms(z[:, o1:o2], gckv_ref[...]).astype(BF16)

    ang = pos_ref[...].astype(F32) * invf_ref[...]
    cos = jnp.cos(ang)
    sin = jnp.sin(ang)
    lane = lax.broadcasted_iota(jnp.int32, (1, LANES), 1)
    first_half = (lane % MLA_ROPE) < (MLA_ROPE // 2)

    def rope(xr):
        rot = jnp.where(first_half, -pltpu.roll(xr, LANES - MLA_ROPE // 2, 1),
                        pltpu.roll(xr, MLA_ROPE // 2, 1))
        return xr * cos + rot * sin

    q = jnp.dot(c_q, wuq_ref[...], preferred_element_type=F32)
    gq = gq_ref[...]
    for c in range(MLA_HEADS // 2):
        lo = c * MXU_DIM
        qc = _group_rms(q[:, lo:lo + MXU_DIM], gmq_ref[...]) * gq[:, lo:lo + MXU_DIM]
        qat_ref[0, lo:lo + LANES, :] = qc[:, :LANES].T.astype(BF16)
        qat_ref[0, lo + LANES:lo + MXU_DIM, :] = rope(qc[:, LANES:]).T.astype(BF16)

    kv = jnp.dot(c_kv, wukv_ref[...], preferred_element_type=F32)
    k_rope = rope(_group_rms(z[:, o2:o3], gm32_ref[...]) * gkr_ref[...]).astype(BF16)
    gkn = gkn_ref[...]
    for c in range(MLA_HEADS // 2):
        lo = c * LANES
        if c % 2 == 0:
            kn2 = _group_rms(kv[:, lo:lo + MXU_DIM], gm64_ref[...]) * gkn[:, lo:lo + MXU_DIM]
        kn = kn2[:, (c % 2) * LANES:(c % 2 + 1) * LANES]
        ka_ref[:, c * MXU_DIM:c * MXU_DIM + LANES] = kn.astype(BF16)
        ka_ref[:, c * MXU_DIM + LANES:(c + 1) * MXU_DIM] = k_rope
    vat_ref[0] = kv[:, dw:].T.astype(BF16)

    gqb = gqb_ref[...]
    gkb = gkb_ref[...]
    qkv = []
    for c in range(dw // MXU_DIM):
        lo = c * MXU_DIM
        qkv.append((
            _group_rms(z[:, o3 + lo:o3 + lo + MXU_DIM], gm64_ref[...]) * gqb[:, lo:lo + MXU_DIM],
            _group_rms(z[:, o3 + dw + lo:o3 + dw + lo + MXU_DIM], gm64_ref[...])
            * gkb[:, lo:lo + MXU_DIM],
            z[:, o3 + 2 * dw + lo:o3 + 2 * dw + lo + MXU_DIM]))
    tm = x_ref.shape[0]
    for a in range(3):
        for c in range(dw // LANES):
            half = (c % 2) * LANES
            stage_ref[a, c] = qkv[c // 2][a][:, half:half + LANES]
            for (_, dil), out_ref in zip(DIL_PATTERNS, dil_refs[a::3]):
                for r in range(dil):
                    out_ref[0, r, :, c * LANES:(c + 1) * LANES] = stage_ref[
                        a, c, pl.ds(r, tm // dil, stride=dil), :].astype(BF16)


def _even_proj(h2d, pos2d, gmix, win, gcq, gckv, wuq, wukv, gq, gkn, gkr, gqb, gkb,
               gmq, gm64, gm32, invf):
    t = h2d.shape[0]
    tm = ROW_TILE
    consts = (gmix, win, gcq, gckv, wuq, wukv, gq, gkn, gkr, gqb, gkb, gmq, gm64, gm32, invf)
    tpb = SEQ // tm
    dw = DIL_HEADS * HEAD_DIM
    out_specs = [pl.BlockSpec((1, 1024, tm), lambda i: (i // tpb, 0, i % tpb)),
                 pl.BlockSpec((tm, 1024), lambda i: (i, 0)),
                 pl.BlockSpec((1, MLA_HEADS * MLA_V, tm), lambda i: (i // tpb, 0, i % tpb))]
    out_shape = [jax.ShapeDtypeStruct((BATCH, 1024, SEQ), BF16),
                 jax.ShapeDtypeStruct((t, 1024), BF16),
                 jax.ShapeDtypeStruct((BATCH, MLA_HEADS * MLA_V, SEQ), BF16)]
    for (_, dil) in DIL_PATTERNS:
        for _ in range(3):
            out_specs.append(pl.BlockSpec((1, dil, tm // dil, dw),
                                          lambda i: (i // tpb, 0, i % tpb, 0)))
            out_shape.append(jax.ShapeDtypeStruct((BATCH, dil, SEQ // dil, dw), BF16))
    return pl.pallas_call(
        _even_proj_kernel,
        grid=(t // tm,),
        in_specs=[pl.BlockSpec((tm, D_MODEL), lambda i: (i, 0)),
                  pl.BlockSpec((tm, 1), lambda i: (i, 0))] + [_const_spec(c.shape) for c in consts],
        out_specs=out_specs,
        out_shape=out_shape,
        scratch_shapes=[pltpu.VMEM((3, dw // LANES, tm, LANES), F32)],
        compiler_params=_params(("parallel",)),
        name="even_proj",
    )(h2d, pos2d, *consts)


def _mla_kernel(qt_ref, k_ref, vt_ref, o_ref, *bufs):
    tq, tk = ATT_TILE_Q, ATT_TILE
    feat = lax.broadcasted_iota(jnp.int32, (MXU_DIM, 1), 0)
    kpos = lax.broadcasted_iota(jnp.int32, (tk, tq), 0)
    qpos = lax.broadcasted_iota(jnp.int32, (tk, tq), 1)
    head_rows = []
    for hh in range(2):
        nope = (feat >= MLA_NOPE * hh) & (feat < MLA_NOPE * (hh + 1))
        rope = (feat >= LANES + MLA_ROPE * hh) & (feat < LANES + MLA_ROPE * (hh + 1))
        head_rows.append(nope | rope)
    init = (jnp.full((1, tq), -jnp.inf, F32), jnp.zeros((1, tq), F32), jnp.zeros((MLA_V, tq), F32))

    def consume(j, slot, cms, carry):
        vt = vt_ref[0, :, pl.ds(_aligned(j * tk, tk), tk)]
        ps, stats = [], []
        for hh in range(2):
            m, l, _ = carry[hh]
            m_new = jnp.maximum(m, cms[hh])
            alpha = jnp.exp2(m - m_new)
            p = jnp.exp2(slot[hh] - m_new)
            stats.append((m_new, alpha * l + jnp.sum(p, axis=0, keepdims=True), alpha))
            ps.append(p.astype(BF16))
        new = []
        for hh in range(2):
            m_new, l, alpha = stats[hh]
            acc = alpha * carry[hh][2] + jnp.dot(vt[hh * MLA_V:(hh + 1) * MLA_V, :], ps[hh],
                                                 preferred_element_type=F32)
            new.append((m_new, l, acc))
        return tuple(new)

    def finish(i, last, slot, cms, carry):
        carry = consume(last, slot, cms, carry)
        o_t = jnp.concatenate([acc / l for (_, l, acc) in carry], axis=0)
        o_ref[i * tq:(i + 1) * tq, :] = o_t.T.astype(BF16)

    pending = None
    for i in range(SEQ // tq):
        s_a, s_b = bufs[2 * (i % 2)], bufs[2 * (i % 2) + 1]
        qt = qt_ref[0, :, i * tq:(i + 1) * tq]
        qts = [jnp.where(rows, qt, jnp.zeros_like(qt)) for rows in head_rows]

        def scores(j, diag_shift, slot, qts=qts):
            kt = k_ref[0, pl.ds(_aligned(j * tk, tk), tk), :]
            cms = []
            for hh in range(2):
                s = jnp.dot(kt, qts[hh], preferred_element_type=F32)
                if diag_shift is not None:
                    s = jnp.where(kpos + diag_shift <= qpos, s, -jnp.inf)
                slot[hh] = s
                cms.append(jnp.max(s, axis=0, keepdims=True))
            return tuple(cms)

        cms_a = scores(2 * i, 0, s_a)
        cms_b = scores(2 * i + 1, tk, s_b)
        if pending is not None:
            finish(*pending)
        carry = consume(2 * i, s_a, cms_a, (init, init))

        def body(n, state, i=i, s_a=s_a, s_b=s_b, scores=scores):
            cms_b, carry = state
            cms_a = scores(2 * n, None, s_a)
            carry = consume(jnp.where(n == 0, 2 * i + 1, 2 * n - 1), s_b, cms_b, carry)
            cms_b = scores(2 * n + 1, None, s_b)
            return cms_b, consume(2 * n, s_a, cms_a, carry)

        if i > 0:
            cms_b, carry = lax.fori_loop(0, i, body, (cms_b, carry))
        pending = (i, 1 if i == 0 else 2 * i - 1, s_b, cms_b, carry)
    finish(*pending)


def _mla_attention(qat, ka, vat):
    t = ka.shape[0]
    ka3 = ka.reshape(BATCH, SEQ, ka.shape[1])
    return pl.pallas_call(
        _mla_kernel,
        grid=(BATCH, MLA_HEADS // 2),
        in_specs=[pl.BlockSpec((1, MXU_DIM, SEQ), lambda b, c: (b, c, 0)),
                  pl.BlockSpec((1, SEQ, MXU_DIM), lambda b, c: (b, 0, c)),
                  pl.BlockSpec((1, LANES, SEQ), lambda b, c: (b, c, 0))],
        out_specs=pl.BlockSpec((SEQ, LANES), lambda b, c: (b, c)),
        out_shape=jax.ShapeDtypeStruct((t, MLA_HEADS * MLA_V), BF16),
        scratch_shapes=[pltpu.VMEM((2, ATT_TILE, ATT_TILE_Q), F32)] * 4,
        compiler_params=_params(("parallel", "parallel")),
        name="mla_attention",
    )(qat, ka3, vat)


DIL_QBLK = 2 * DIL_BLOCK
DIL_HALO = DIL_BLOCK
DIL_GROUP = 8


def _dil_kernel(q_ref, k_ref, v_ref, mb_ref, o_ref, lse_ref, vt_scr, *, nblk):
    qb, halo = DIL_QBLK, DIL_HALO
    ncls = q_ref.shape[1]
    lane = lax.broadcasted_iota(jnp.int32, (1, LANES), 1)
    low = lane < HEAD_DIM
    for g in range(ncls):
        vt_scr[g] = v_ref[0, g].astype(F32).T.astype(BF16)

    def blocks(items):
        work = []
        for g, r0, first in items:
            q = q_ref[0, g, pl.ds(r0, qb), :].astype(F32)
            if first:
                k0, nk = 0, qb
            else:
                k0 = r0 - halo if isinstance(r0, int) else pl.multiple_of(r0 - halo, halo)
                nk = qb + halo
            k = k_ref[0, g, pl.ds(k0, nk), :]
            for hh in range(2):
                qt = jnp.where(low if hh == 0 else ~low, q, 0.0).T.astype(BF16)
                s = jnp.dot(k, qt, preferred_element_type=F32) - (
                    mb_ref[0, hh, halo:, :] if first else mb_ref[0, hh])
                work.append((g, r0, k0, nk, hh, s))
        outs = []
        for g, r0, k0, nk, hh, s in work:
            m = jnp.max(s, axis=0, keepdims=True)
            p = jnp.exp2(s - m)
            den = jnp.sum(p, axis=0, keepdims=True)
            vt = vt_scr[g, hh * HEAD_DIM:(hh + 1) * HEAD_DIM, pl.ds(k0, nk)]
            o = jnp.dot(vt, p.astype(BF16), preferred_element_type=F32) / den
            lse = (m + jnp.log2(den)) * LN2
            outs.append((o, jnp.broadcast_to(lse, (HEAD_DIM, qb))))
        for n, (g, r0, _) in enumerate(items):
            pair = outs[2 * n:2 * n + 2]
            o_ref[0, g, pl.ds(r0, qb), :] = jnp.concatenate([o for o, _ in pair], axis=0).T
            lse_ref[0, g, pl.ds(r0, qb), :] = jnp.concatenate([l for _, l in pair], axis=0).T

    per = DIL_GROUP // ncls
    blocks([(g, n * qb, n == 0) for g in range(ncls) for n in range(per)])

    def body(n, carry):
        r0 = pl.multiple_of(per * n * qb, per * qb)
        blocks([(g, r0 + c * qb if c == 0 else pl.multiple_of(r0 + c * qb, qb), False)
                for g in range(ncls) for c in range(per)])
        return carry

    lax.fori_loop(1, nblk // per, body, 0)


def _dilated_branch(qc, kc, vc, mb, dil):
    l = SEQ // dil
    dw = DIL_HEADS * HEAD_DIM
    nblk = l // DIL_QBLK
    ncls = max(1, DIL_GROUP // nblk)
    assert dil % ncls == 0 and nblk % (DIL_GROUP // ncls) == 0
    blk = pl.BlockSpec((1, ncls, l, LANES), lambda c, b, r: (b, r, 0, c))
    return pl.pallas_call(
        functools.partial(_dil_kernel, nblk=nblk),
        grid=(DIL_HEADS // 2, BATCH, dil // ncls),
        in_specs=[blk, blk, blk,
                  pl.BlockSpec((1, 2, DIL_QBLK + DIL_HALO, DIL_QBLK), lambda c, b, r: (c, 0, 0, 0))],
        out_specs=[blk, blk],
        out_shape=[jax.ShapeDtypeStruct((BATCH, dil, l, dw), F32)] * 2,
        scratch_shapes=[pltpu.VMEM((ncls, LANES, l), BF16)],
        compiler_params=_params(("parallel", "parallel", "parallel")),
        name=f"dilated_d{dil}",
    )(qc, kc, vc, mb)


def _dil_bias_table(dil, window):
    reach = window // dil
    ki = np.arange(DIL_QBLK + DIL_HALO)[:, None]
    qi = np.arange(DIL_QBLK)[None, :]
    step = DIL_HALO + qi - ki
    valid = (step >= 0) & (step <= reach)
    slopes = 2.0 ** (-8.0 * np.arange(1, DIL_HEADS + 1, dtype=np.float32) / DIL_HEADS)
    bias = slopes.astype(np.float32)[:, None, None] * (step * dil).astype(np.float32)[None]
    tab = np.where(valid[None], bias * LOG2E, np.inf).astype(np.float32)
    return jnp.asarray(tab.reshape(DIL_HEADS // 2, 2, DIL_QBLK + DIL_HALO, DIL_QBLK))


def _even_out_kernel(h_ref, oa_ref, *rest):
    branch_refs, w_ref, out_ref, stage_ref = rest[:-3], rest[-3], rest[-2], rest[-1]
    tm = h_ref.shape[0]
    vals = []
    for n, ((_, dil), ref) in enumerate(zip([pt for pt in DIL_PATTERNS for _ in range(2)],
                                            branch_refs)):
        if dil == 1:
            vals.append(ref[0, 0])
        else:
            slot = n - 2
            for r in range(dil):
                for c in range(ref.shape[-1] // LANES):
                    stage_ref[slot, c, pl.ds(r, tm // dil, stride=dil), :] = ref[
                        0, r, :, c * LANES:(c + 1) * LANES]
            vals.append(jnp.concatenate(
                [stage_ref[slot, c] for c in range(ref.shape[-1] // LANES)], axis=1))
    o1, l1, o2, l2, o3, l3 = vals
    lm = jnp.maximum(jnp.maximum(l1, l2), l3)
    w1, w2, w3 = jnp.exp(l1 - lm), jnp.exp(l2 - lm), jnp.exp(l3 - lm)
    mixed_b = (w1 * o1 + w2 * o2 + w3 * o3) / (w1 + w2 + w3)
    half = MLA_HEADS * MLA_V
    acc = jnp.dot(oa_ref[...], w_ref[:half, :], preferred_element_type=F32)
    acc += jnp.dot(mixed_b.astype(BF16), w_ref[half:, :], preferred_element_type=F32)
    out_ref[...] = h_ref[...] + acc


def _even_out(h2d, oa, branches, w_out):
    t = h2d.shape[0]
    tm = ROW_TILE
    tpb = SEQ // tm
    dw = DIL_HEADS * HEAD_DIM
    full = pl.BlockSpec((tm, D_MODEL), lambda i: (i, 0))
    specs, args = [], []
    for (_, dil), pair in zip(DIL_PATTERNS, branches):
        for a in pair:
            specs.append(pl.BlockSpec((1, dil, tm // dil, dw), lambda i: (i // tpb, 0, i % tpb, 0)))
            args.append(a)
    return pl.pallas_call(
        _even_out_kernel,
        grid=(t // tm,),
        in_specs=[full, pl.BlockSpec((tm, dw), lambda i: (i, 0))] + specs
        + [_const_spec(w_out.shape)],
        out_specs=full,
        out_shape=jax.ShapeDtypeStruct((t, D_MODEL), F32),
        scratch_shapes=[pltpu.VMEM((4, dw // LANES, tm, LANES), F32)],
        compiler_params=_params(("parallel",)),
        name="even_out_proj",
    )(h2d, oa, *args, w_out)


def _tail_kernel(*refs, with_out_proj):
    if with_out_proj:
        h_ref, o_ref, wout_ref, *refs = refs
    else:
        h_ref, *refs = refs
    p_ref, gff_ref, w1_ref, w2_ref, gple_ref, wg_ref, wp_ref, out_ref = refs
    h = h_ref[...]
    if with_out_proj:
        h = h + jnp.dot(o_ref[...], wout_ref[...], preferred_element_type=F32)
    n = _rms(h, gff_ref[...]).astype(BF16)
    chunk = D_MODEL
    for c in range(D_FF // chunk):
        u = jnp.maximum(jnp.dot(n, w1_ref[:, c * chunk:(c + 1) * chunk],
                                preferred_element_type=F32), 0.0)
        h = h + jnp.dot((u * u).astype(BF16), w2_ref[c * chunk:(c + 1) * chunk, :],
                        preferred_element_type=F32)
    n = _rms(h, gple_ref[...]).astype(BF16)
    gate = jax.nn.sigmoid(jnp.dot(n, wg_ref[...], preferred_element_type=F32))
    proj = jnp.dot(p_ref[...].astype(BF16), wp_ref[...], preferred_element_type=F32)
    out_ref[...] = h + gate * proj


def _tail(h2d, p2d, gff, w1, w2, gple, wg, wp, mix=None, w_out=None):
    t = h2d.shape[0]
    tm = ROW_TILE
    full = pl.BlockSpec((tm, D_MODEL), lambda i: (i, 0))
    with_out = mix is not None
    head_specs = [full, full, _const_spec(w_out.shape, single=True)] if with_out else [full]
    head_args = (h2d, mix, w_out) if with_out else (h2d,)
    consts = (gff, w1, w2, gple, wg, wp)
    return pl.pallas_call(
        functools.partial(_tail_kernel, with_out_proj=with_out),
        grid=(t // tm,),
        in_specs=head_specs + [pl.BlockSpec((tm, PLE_DIM), lambda i: (i, 0))]
        + [_const_spec(c.shape, single=True) for c in consts],
        out_specs=full,
        out_shape=jax.ShapeDtypeStruct((t, D_MODEL), F32),
        compiler_params=_params(("parallel",)),
        name="tail_with_out_proj" if with_out else "tail",
    )(*head_args, p2d, *consts)


def _odd_proj_kernel(x_ref, gmix_ref, win_ref, gq_ref, gk_ref, gm64_ref,
                     qt_ref, k_ref, vt_ref, km_ref):
    hn = _rms(x_ref[...], gmix_ref[...]).astype(BF16)
    z = jnp.dot(hn, win_ref[...], preferred_element_type=F32)
    dw = MOBA_HEADS * HEAD_DIM
    gq = gq_ref[...]
    gk = gk_ref[...]
    tm = x_ref.shape[0]
    for c in range(dw // MXU_DIM):
        lo = c * MXU_DIM
        qt_ref[0, lo:lo + MXU_DIM, :] = (
            _group_rms(z[:, lo:lo + MXU_DIM], gm64_ref[...]) * gq[:, lo:lo + MXU_DIM]
        ).T.astype(BF16)
        k = _group_rms(z[:, dw + lo:dw + lo + MXU_DIM], gm64_ref[...]) * gk[:, lo:lo + MXU_DIM]
        k_ref[:, lo:lo + MXU_DIM] = k.astype(BF16)
        for r in range(tm // MOBA_BLOCK):
            km_ref[r, :, lo:lo + MXU_DIM] = jnp.mean(
                k[r * MOBA_BLOCK:(r + 1) * MOBA_BLOCK], axis=0, keepdims=True)
    vt_ref[0] = z[:, 2 * dw:].T.astype(BF16)


def _odd_proj(h2d, gmix, win, gq, gk, gm64):
    t = h2d.shape[0]
    tm = ROW_TILE
    dw = MOBA_HEADS * HEAD_DIM
    consts = (gmix, win, gq, gk, gm64)
    full = pl.BlockSpec((tm, dw), lambda i: (i, 0))
    tpb = SEQ // tm
    tspec = pl.BlockSpec((1, dw, tm), lambda i: (i // tpb, 0, i % tpb))
    return pl.pallas_call(
        _odd_proj_kernel,
        grid=(t // tm,),
        in_specs=[pl.BlockSpec((tm, D_MODEL), lambda i: (i, 0))] + [_const_spec(c.shape) for c in consts],
        out_specs=[tspec, full, tspec,
                   pl.BlockSpec((tm // MOBA_BLOCK, 1, dw), lambda i: (i, 0, 0))],
        out_shape=[jax.ShapeDtypeStruct((BATCH, dw, SEQ), BF16), jax.ShapeDtypeStruct((t, dw), BF16),
                   jax.ShapeDtypeStruct((BATCH, dw, SEQ), BF16),
                   jax.ShapeDtypeStruct((t // MOBA_BLOCK, 1, dw), F32)],
        compiler_params=_params(("parallel",)),
        name="odd_proj",
    )(h2d, *consts)


def _moba_kernel(qt_ref, k_ref, vt_ref, km_ref, sl_ref, kb_ref, o_ref, *bufs):
    tq, tk = ATT_TILE_Q, MOBA_BLOCK
    nb = SEQ // MOBA_BLOCK
    kpos = lax.broadcasted_iota(jnp.int32, (tk, tq), 0)
    qpos = lax.broadcasted_iota(jnp.int32, (tk, tq), 1)
    row = lax.broadcasted_iota(jnp.int32, (LANES, tq), 0)
    blk_id = lax.broadcasted_iota(jnp.int32, (nb, tq), 0)
    half = lax.broadcasted_iota(jnp.int32, (1, tq), 1) // tk
    kbias = kb_ref[0]
    slopes = [sl_ref[0, :, hh * HEAD_DIM:hh * HEAD_DIM + 1] for hh in range(2)]
    km_terms, resid = [], km_ref[0]
    for _ in range(KM_PARTS):
        km_terms.append(resid.astype(BF16))
        resid = resid - km_terms[-1].astype(F32)
    km_cat = jnp.concatenate(km_terms, axis=0)
    km_cat = jnp.concatenate([km_cat, jnp.zeros_like(km_cat)], axis=1)
    init = (jnp.full((1, tq), MAX_FLOOR, F32), jnp.zeros((1, tq), F32),
            jnp.zeros((HEAD_DIM, tq), F32))

    def prepare(i):
        own = 2 * i + half
        q_t = qt_ref[0, :, i * tq:(i + 1) * tq]
        heads = []
        for hh in range(2):
            qt = jnp.where((row >= HEAD_DIM * hh) & (row < HEAD_DIM * (hh + 1)), q_t,
                           jnp.zeros_like(q_t))
            ones = jnp.where((row >= KB_PARTS * hh) & (row < KB_PARTS * (hh + 1)), 1.0, 0.0)
            qt = jnp.concatenate([qt, ones.astype(BF16)], axis=0)
            gate_terms = jnp.dot(km_cat, qt, preferred_element_type=F32)
            gate = sum(gate_terms[n * nb:(n + 1) * nb] for n in range(KM_PARTS))
            g = jnp.where(blk_id < own, gate, -jnp.inf)
            sel = []
            for _ in range(MOBA_TOPK):
                mx = jnp.max(g, axis=0, keepdims=True)
                idx = jnp.min(jnp.where(g == mx, blk_id, nb), axis=0, keepdims=True)
                sel.append(jnp.where(mx > -jnp.inf, idx, -1))
                g = jnp.where(blk_id == idx, -jnp.inf, g)
            heads.append((qt, sel))
        return heads, own

    def scores(tile, j, diag_shift, slot):
        heads, _ = tile
        kt = jnp.concatenate([k_ref[0, pl.ds(_aligned(j * tk, tk), tk), :], kbias],
                             axis=1)
        cms = []
        for hh in range(2):
            s = jnp.dot(kt, heads[hh][0], preferred_element_type=F32)
            if diag_shift is not None:
                s = jnp.where(kpos + diag_shift <= qpos, s, -jnp.inf)
            slot[hh] = s
            cms.append(jnp.max(s, axis=0, keepdims=True))
        return tuple(cms)

    def consume(i, tile, j, slot, cms, carry):
        heads, own = tile
        vt = vt_ref[0, :, pl.ds(_aligned(j * tk, tk), tk)]
        ps, stats = [], []
        for hh in range(2):
            sel = heads[hh][1]
            m, l, _ = carry[hh]
            seen = (sel[0] == j) | (sel[1] == j) | (sel[2] == j) | (own == j)
            off = slopes[hh] * jnp.asarray((2 * i - j) * tk, F32)
            m_new = jnp.where(seen, jnp.maximum(m, cms[hh] - off), m)
            alpha = jnp.exp2(m - m_new)
            p = jnp.exp2(slot[hh] - jnp.where(seen, m_new + off, jnp.inf))
            stats.append((m_new, alpha * l + jnp.sum(p, axis=0, keepdims=True), alpha))
            ps.append(p.astype(BF16))
        new = []
        for hh in range(2):
            m_new, l, alpha = stats[hh]
            acc = alpha * carry[hh][2] + jnp.dot(vt[hh * HEAD_DIM:(hh + 1) * HEAD_DIM, :], ps[hh],
                                                 preferred_element_type=F32)
            new.append((m_new, l, acc))
        return tuple(new)

    def finish(i, tile, last, slot, cms, carry):
        carry = consume(i, tile, last, slot, cms, carry)
        o_t = jnp.concatenate([acc / l for (_, l, acc) in carry], axis=0)
        o_ref[i * tq:(i + 1) * tq, :] = o_t.T.astype(BF16)

    pending = None
    for i in range(SEQ // tq):
        s_a, s_b = bufs[2 * (i % 2)], bufs[2 * (i % 2) + 1]
        tile = prepare(i)
        cms_a = scores(tile, 2 * i, 0, s_a)
        cms_b = scores(tile, 2 * i + 1, tk, s_b)
        if pending is not None:
            finish(*pending)
        carry = consume(i, tile, 2 * i, s_a, cms_a, (init, init))

        def body(n, state, i=i, tile=tile, s_a=s_a, s_b=s_b):
            cms_b, carry = state
            cms_a = scores(tile, 2 * n, None, s_a)
            carry = consume(i, tile, jnp.where(n == 0, 2 * i + 1, 2 * n - 1), s_b, cms_b, carry)
            cms_b = scores(tile, 2 * n + 1, None, s_b)
            return cms_b, consume(i, tile, 2 * n, s_a, cms_a, carry)

        if i > 0:
            cms_b, carry = lax.fori_loop(0, i, body, (cms_b, carry))
        pending = (i, tile, 1 if i == 0 else 2 * i - 1, s_b, cms_b, carry)
    finish(*pending)


def _moba_attention(qt3, k, vt3, km, slopes, kbias):
    t = k.shape[0]
    nb = SEQ // MOBA_BLOCK
    dw = MOBA_HEADS * HEAD_DIM
    k3 = k.reshape(BATCH, SEQ, dw)
    km3 = km.reshape(BATCH, nb, dw)
    tspec = pl.BlockSpec((1, LANES, SEQ), lambda b, c: (b, c, 0))
    return pl.pallas_call(
        _moba_kernel,
        grid=(BATCH, MOBA_HEADS // 2),
        in_specs=[tspec,
                  pl.BlockSpec((1, SEQ, LANES), lambda b, c: (b, 0, c)),
                  tspec,
                  pl.BlockSpec((1, nb, LANES), lambda b, c: (b, 0, c)),
                  pl.BlockSpec((1, 1, LANES), lambda b, c: (c, 0, 0)),
                  pl.BlockSpec((1, MOBA_BLOCK, LANES), lambda b, c: (c, 0, 0))],
        out_specs=pl.BlockSpec((SEQ, LANES), lambda b, c: (b, c)),
        out_shape=jax.ShapeDtypeStruct((t, dw), BF16),
        scratch_shapes=[pltpu.VMEM((2, MOBA_BLOCK, ATT_TILE_Q), F32)] * 4,
        compiler_params=_params(("parallel", "parallel")),
        name="moba_attention",
    )(qt3, k3, vt3, km3, slopes, kbias)


def _group_matrix(sizes, width):
    g = np.zeros((width, width), np.float32)
    lo = 0
    for n in sizes:
        g[lo:lo + n, lo:lo + n] = 1.0 / n
        lo += n
    return jnp.asarray(g, BF16)


def _alibi_slopes_log2(n_heads):
    slopes = 2.0 ** (-8.0 * np.arange(1, n_heads + 1, dtype=np.float32) / n_heads)
    return slopes.astype(np.float64) * LOG2E


def _pair_slopes(n_heads):
    tab = np.repeat(_alibi_slopes_log2(n_heads).astype(np.float32), HEAD_DIM)
    return jnp.asarray(tab.reshape(n_heads // 2, 1, LANES))


def _pair_key_bias(n_heads, block):
    bias = _alibi_slopes_log2(n_heads)[:, None] * np.arange(block, dtype=np.float64)[None, :]
    tab = np.zeros((n_heads // 2, block, LANES), np.float32)
    for part in range(KB_PARTS):
        term = bias.astype(jnp.bfloat16).astype(np.float64)
        bias = bias - term
        for h in range(n_heads):
            tab[h // 2, :, KB_PARTS * (h % 2) + part] = term[h]
    return jnp.asarray(tab, BF16)


def _even_weights(w_in, w_uq, w_ukv, qn_nope, qn_rope, kn_nope, kn_rope, dil_qn, dil_kn):
    o1 = Q_LORA
    o2 = o1 + KV_LORA
    o3 = o2 + MLA_ROPE
    kr = w_in[:, o2:o3]
    pad64 = jnp.zeros((D_MODEL, LANES - 2 * MLA_ROPE), F32)
    win = jnp.concatenate([w_in[:, :o2], kr, kr, pad64, w_in[:, o3:]], axis=1).astype(BF16)

    qd = MLA_NOPE + MLA_ROPE
    a_scale = qd ** -0.5
    zq = jnp.zeros((Q_LORA, LANES - 2 * MLA_ROPE), F32)
    zg = jnp.zeros((LANES - 2 * MLA_ROPE,), F32)
    cols, gains = [], []
    for c in range(MLA_HEADS // 2):
        h0, h1 = 2 * c, 2 * c + 1
        cols += [w_uq[:, h0 * qd:h0 * qd + MLA_NOPE], w_uq[:, h1 * qd:h1 * qd + MLA_NOPE],
                 w_uq[:, h0 * qd + MLA_NOPE:(h0 + 1) * qd], w_uq[:, h1 * qd + MLA_NOPE:(h1 + 1) * qd], zq]
        gains += [qn_nope, qn_nope, qn_rope, qn_rope, zg]
    wuq = jnp.concatenate(cols, axis=1).astype(BF16)
    gq = (jnp.concatenate(gains) * (a_scale * LOG2E))[None, :]

    kvd = MLA_NOPE + MLA_V
    kcols = [w_ukv[:, h * kvd:h * kvd + MLA_NOPE] for h in range(MLA_HEADS)]
    vcols = [w_ukv[:, h * kvd + MLA_NOPE:(h + 1) * kvd] for h in range(MLA_HEADS)]
    wukv = jnp.concatenate(kcols + vcols, axis=1).astype(BF16)
    gkn = jnp.tile(kn_nope, MLA_HEADS)[None, :]
    gkr = jnp.concatenate([kn_rope, kn_rope, zg])[None, :]
    gqb = (jnp.tile(dil_qn, DIL_HEADS) * (HEAD_DIM ** -0.5 * LOG2E))[None, :]
    gkb = jnp.tile(dil_kn, DIL_HEADS)[None, :]
    return win, wuq, wukv, gq, gkn, gkr, gqb, gkb


def kernel(x, p, positions, e_w_in, e_cq_norm, e_ckv_norm, e_w_uq, e_w_ukv, e_qn_nope, e_qn_rope, e_kn_nope, e_kn_rope, e_dil_qn, e_dil_kn, e_w_out, o_w_in, o_qn, o_kn, o_w_out, mix_norm, ff_norm, w_ff1, w_ff2, ple_norm, w_ple_gate, w_ple_proj):
    t = BATCH * SEQ
    h = x.reshape(t, D_MODEL)
    pos2d = positions.reshape(t, 1)
    p2d = p.reshape(DEPTH, t, PLE_DIM)

    gmq = _group_matrix((MLA_NOPE, MLA_NOPE, MLA_ROPE, MLA_ROPE, LANES - 2 * MLA_ROPE), MXU_DIM)
    gm64 = _group_matrix((HEAD_DIM,) * (MXU_DIM // HEAD_DIM), MXU_DIM)
    gm32 = _group_matrix((MLA_ROPE,) * (LANES // MLA_ROPE), LANES)
    half = MLA_ROPE // 2
    invf = ROPE_THETA ** (-(jnp.arange(LANES) % half).astype(F32) / half)
    invf = invf[None, :]

    for i in range(DEPTH):
        j = i // 2
        gmix = mix_norm[i][None, :]
        if i % 2 == 0:
            win, wuq, wukv, gq, gkn, gkr, gqb, gkb = _even_weights(
                e_w_in[j], e_w_uq[j], e_w_ukv[j], e_qn_nope[j], e_qn_rope[j], e_kn_nope[j],
                e_kn_rope[j], e_dil_qn[j], e_dil_kn[j])
            qa, ka, vat, *qkv_b = _even_proj(
                h, pos2d, gmix, win, e_cq_norm[j][None, :], e_ckv_norm[j][None, :], wuq, wukv,
                gq, gkn, gkr, gqb, gkb, gmq, gm64, gm32, invf)
            oa = _mla_attention(qa, ka, vat)
            branches = [_dilated_branch(*qkv_b[3 * n:3 * n + 3], _dil_bias_table(d, w), d)
                        for n, (w, d) in enumerate(DIL_PATTERNS)]
            h = _even_out(h, oa, branches, e_w_out[j].astype(BF16))
            mix = {}
        else:
            gq = (jnp.tile(o_qn[j], MOBA_HEADS) * (HEAD_DIM ** -0.5 * LOG2E))[None, :]
            gk = jnp.tile(o_kn[j], MOBA_HEADS)[None, :]
            q, k, vt, km = _odd_proj(h, gmix, o_w_in[j].astype(BF16), gq, gk, gm64)
            o = _moba_attention(q, k, vt, km, _pair_slopes(MOBA_HEADS),
                                _pair_key_bias(MOBA_HEADS, MOBA_BLOCK))
            mix = {"mix": o, "w_out": o_w_out[j].astype(BF16)}
        h = _tail(h, p2d[i], ff_norm[i][None, :], w_ff1[i].astype(BF16), w_ff2[i].astype(BF16),
                  ple_norm[i][None, :], w_ple_gate[i].astype(BF16), w_ple_proj[i].astype(BF16),
                  **mix)
    return h.reshape(BATCH, SEQ, D_MODEL)
```

```python
import functools

import numpy as np
import jax
import jax.numpy as jnp
from jax import lax
from jax.experimental import pallas as pl
from jax.experimental.pallas import tpu as pltpu

D_MODEL = 1024
BATCH = 8
SEQ = 4096
DEPTH = 2
HEAD_DIM = 64
EPS = 1e-6
MLA_HEADS = 8
MLA_NOPE = 64
MLA_ROPE = 32
MLA_V = 64
Q_LORA = 384
KV_LORA = 256
ROPE_THETA = 10000.0
DIL_HEADS = 8
DIL_PATTERNS = ((128, 1), (512, 4), (2048, 16))
DIL_BLOCK = 128
MOBA_HEADS = 16
MOBA_BLOCK = 256
MOBA_TOPK = 3
D_FF = 4 * D_MODEL
PLE_DIM = 256

LANES = 128
MXU_DIM = 256
VMEM_LIMIT = 56 * 1024 * 1024
ROW_TILE = 512
ATT_TILE = 256
ATT_TILE_Q = 512
MAX_FLOOR = -1e30
KB_PARTS = 3
KM_PARTS = 3
LOG2E = float(np.log2(np.e))
LN2 = float(np.log(2.0))

F32 = jnp.float32
BF16 = jnp.bfloat16


def _const_spec(shape, single=False):
    nd = len(shape)
    mode = {"pipeline_mode": pl.Buffered(1)} if single else {}
    return pl.BlockSpec(shape, lambda *_: (0,) * nd, **mode)


def _params(sem):
    return pltpu.CompilerParams(dimension_semantics=sem, vmem_limit_bytes=VMEM_LIMIT)


def _aligned(x, m):
    return x if isinstance(x, int) else pl.multiple_of(x, m)


def _rms(x, g):
    return x * lax.rsqrt(jnp.mean(x * x, axis=-1, keepdims=True) + EPS) * g


def _group_rms(x, gmat):
    ms = jnp.dot((x * x).astype(BF16), gmat, preferred_element_type=F32)
    return x * lax.rsqrt(ms + EPS)


def _even_proj_kernel(x_ref, pos_ref, gmix_ref, win_ref, gcq_ref, gckv_ref, wuq_ref, wukv_ref,
                      gq_ref, gkn_ref, gkr_ref, gqb_ref, gkb_ref, gmq_ref, gm64_ref, gm32_ref,
                      invf_ref, qat_ref, ka_ref, vat_ref, *rest):
    dil_refs, stage_ref = rest[:-1], rest[-1]
    hn = _rms(x_ref[...], gmix_ref[...]).astype(BF16)
    z = jnp.dot(hn, win_ref[...], preferred_element_type=F32)
    o1 = Q_LORA
    o2 = o1 + KV_LORA
    o3 = o2 + LANES
    dw = DIL_HEADS * HEAD_DIM
    c_q = _rms(z[:, :o1], gcq_ref[...]).astype(BF16)
    c_kv = _rms(z[:, o1:o2], gckv_ref[...]).astype(BF16)

    ang = pos_ref[...].astype(F32) * invf_ref[...]
    cos = jnp.cos(ang)
    sin = jnp.sin(ang)
    lane = lax.broadcasted_iota(jnp.int32, (1, LANES), 1)
    first_half = (lane % MLA_ROPE) < (MLA_ROPE // 2)

    def rope(xr):
        rot = jnp.where(first_half, -pltpu.roll(xr, LANES - MLA_ROPE // 2, 1),
                        pltpu.roll(xr, MLA_ROPE // 2, 1))
        return xr * cos + rot * sin

    q = jnp.dot(c_q, wuq_ref[...], preferred_element_type=F32)
    gq = gq_ref[...]
    for c in range(MLA_HEADS // 2):
        lo = c * MXU_DIM
        qc = _group_rms(q[:, lo:lo + MXU_DIM], gmq_ref[...]) * gq[:, lo:lo + MXU_DIM]
        qat_ref[0, lo:lo + LANES, :] = qc[:, :LANES].T.astype(BF16)
        qat_ref[0, lo + LANES:lo + MXU_DIM, :] = rope(qc[:, LANES:]).T.astype(BF16)

    kv = jnp.dot(c_kv, wukv_ref[...], preferred_element_type=F32)
    k_rope = rope(_group_rms(z[:, o2:o3], gm32_ref[...]) * gkr_ref[...]).astype(BF16)
    gkn = gkn_ref[...]
    for c in range(MLA_HEADS // 2):
        lo = c * LANES
        if c % 2 == 0:
            kn2 = _group_rms(kv[:, lo:lo + MXU_DIM], gm64_ref[...]) * gkn[:, lo:lo + MXU_DIM]
        kn = kn2[:, (c % 2) * LANES:(c % 2 + 1) * LANES]
        ka_ref[:, c * MXU_DIM:c * MXU_DIM + LANES] = kn.astype(BF16)
        ka_ref[:, c * MXU_DIM + LANES:(c + 1) * MXU_DIM] = k_rope
    vat_ref[0] = kv[:, dw:].T.astype(BF16)

    gqb = gqb_ref[...]
    gkb = gkb_ref[...]
    qkv = []
    for c in range(dw // MXU_DIM):
        lo = c * MXU_DIM
        qkv.append((
            _group_rms(z[:, o3 + lo:o3 + lo + MXU_DIM], gm64_ref[...]) * gqb[:, lo:lo + MXU_DIM],
            _group_rms(z[:, o3 + dw + lo:o3 + dw + lo + MXU_DIM], gm64_ref[...])
            * gkb[:, lo:lo + MXU_DIM],
            z[:, o3 + 2 * dw + lo:o3 + 2 * dw + lo + MXU_DIM]))
    tm = x_ref.shape[0]
    for a in range(3):
        for c in range(dw // LANES):
            half = (c % 2) * LANES
            stage_ref[a, c] = qkv[c // 2][a][:, half:half + LANES]
            for (_, dil), out_ref in zip(DIL_PATTERNS, dil_refs[a::3]):
                for r in range(dil):
                    out_ref[0, r, :, c * LANES:(c + 1) * LANES] = stage_ref[
                        a, c, pl.ds(r, tm // dil, stride=dil), :].astype(BF16)


def _even_proj(h2d, pos2d, gmix, win, gcq, gckv, wuq, wukv, gq, gkn, gkr, gqb, gkb,
               gmq, gm64, gm32, invf):
    t = h2d.shape[0]
    tm = ROW_TILE
    consts = (gmix, win, gcq, gckv, wuq, wukv, gq, gkn, gkr, gqb, gkb, gmq, gm64, gm32, invf)
    tpb = SEQ // tm
    dw = DIL_HEADS * HEAD_DIM
    out_specs = [pl.BlockSpec((1, 1024, tm), lambda i: (i // tpb, 0, i % tpb)),
                 pl.BlockSpec((tm, 1024), lambda i: (i, 0)),
                 pl.BlockSpec((1, MLA_HEADS * MLA_V, tm), lambda i: (i // tpb, 0, i % tpb))]
    out_shape = [jax.ShapeDtypeStruct((BATCH, 1024, SEQ), BF16),
                 jax.ShapeDtypeStruct((t, 1024), BF16),
                 jax.ShapeDtypeStruct((BATCH, MLA_HEADS * MLA_V, SEQ), BF16)]
    for (_, dil) in DIL_PATTERNS:
        for _ in range(3):
            out_specs.append(pl.BlockSpec((1, dil, tm // dil, dw),
                                          lambda i: (i // tpb, 0, i % tpb, 0)))
            out_shape.append(jax.ShapeDtypeStruct((BATCH, dil, SEQ // dil, dw), BF16))
    return pl.pallas_call(
        _even_proj_kernel,
        grid=(t // tm,),
        in_specs=[pl.BlockSpec((tm, D_MODEL), lambda i: (i, 0)),
                  pl.BlockSpec((tm, 1), lambda i: (i, 0))] + [_const_spec(c.shape) for c in consts],
        out_specs=out_specs,
        out_shape=out_shape,
        scratch_shapes=[pltpu.VMEM((3, dw // LANES, tm, LANES), F32)],
        compiler_params=_params(("parallel",)),
        name="even_proj",
    )(h2d, pos2d, *consts)


def _mla_kernel(qt_ref, k_ref, vt_ref, o_ref, *bufs):
    tq, tk = ATT_TILE_Q, ATT_TILE
    feat = lax.broadcasted_iota(jnp.int32, (MXU_DIM, 1), 0)
    visible = (lax.broadcasted_iota(jnp.int32, (tk, tq), 0)
               <= lax.broadcasted_iota(jnp.int32, (tk, tq), 1))
    visible_half = (lax.broadcasted_iota(jnp.int32, (tk, tk), 0)
                    <= lax.broadcasted_iota(jnp.int32, (tk, tk), 1))
    head_rows = []
    for hh in range(2):
        nope = (feat >= MLA_NOPE * hh) & (feat < MLA_NOPE * (hh + 1))
        rope = (feat >= LANES + MLA_ROPE * hh) & (feat < LANES + MLA_ROPE * (hh + 1))
        head_rows.append(nope | rope)
    empty_half = (jnp.full((1, tk), -jnp.inf, F32), jnp.zeros((1, tk), F32),
                  jnp.zeros((MLA_V, tk), F32))

    def consume(j, slot, cms, carry):
        vt = vt_ref[0, :, pl.ds(_aligned(j * tk, tk), tk)]
        ps, stats = [], []
        for hh in range(2):
            m, l, _ = carry[hh]
            m_new = jnp.maximum(m, cms[hh])
            alpha = jnp.exp2(m - m_new)
            p = jnp.exp2(slot[hh] - m_new)
            stats.append((m_new, alpha * l + jnp.sum(p, axis=0, keepdims=True), alpha))
            ps.append(p.astype(BF16))
        new = []
        for hh in range(2):
            m_new, l, alpha = stats[hh]
            acc = alpha * carry[hh][2] + jnp.dot(vt[hh * MLA_V:(hh + 1) * MLA_V, :], ps[hh],
                                                 preferred_element_type=F32)
            new.append((m_new, l, acc))
        return tuple(new)

    def start_right_half(j, slot, cms):
        vt = vt_ref[0, :, pl.ds(_aligned(j * tk, tk), tk)]
        ps = [jnp.exp2(slot[hh, :, :tk] - cms[hh]) for hh in range(2)]
        new = []
        for hh in range(2):
            acc = jnp.dot(vt[hh * MLA_V:(hh + 1) * MLA_V, :], ps[hh].astype(BF16),
                          preferred_element_type=F32)
            seen = (cms[hh], jnp.sum(ps[hh], axis=0, keepdims=True), acc)
            new.append(tuple(jnp.concatenate([e, s], axis=1) for e, s in zip(empty_half, seen)))
        return tuple(new)

    def finish(i, last, slot, cms, carry):
        carry = consume(last, slot, cms, carry)
        o_t = jnp.concatenate([acc / l for (_, l, acc) in carry], axis=0)
        o_ref[i * tq:(i + 1) * tq, :] = o_t.T.astype(BF16)

    pending = None
    for i in range(SEQ // tq):
        s_a, s_b = bufs[2 * (i % 2)], bufs[2 * (i % 2) + 1]
        qt = qt_ref[0, :, i * tq:(i + 1) * tq]
        qts = [jnp.where(rows, qt, jnp.zeros_like(qt)) for rows in head_rows]

        def scores(j, slot, qts=qts, causal=False, right_half=False):
            kt = k_ref[0, pl.ds(_aligned(j * tk, tk), tk), :]
            cms = []
            for hh in range(2):
                s = jnp.dot(kt, qts[hh][:, tk:] if right_half else qts[hh],
                            preferred_element_type=F32)
                if causal:
                    s = jnp.where(visible_half if right_half else visible, s, -jnp.inf)
                if right_half:
                    slot[hh, :, :tk] = s
                else:
                    slot[hh] = s
                cms.append(jnp.max(s, axis=0, keepdims=True))
            return tuple(cms)

        cms_a = scores(2 * i + 1, s_a, causal=True, right_half=True)
        cms_b = scores(2 * i, s_b, causal=True)
        if pending is not None:
            finish(*pending)
        carry = start_right_half(2 * i + 1, s_a, cms_a)

        def body(n, state, i=i, s_a=s_a, s_b=s_b, scores=scores):
            cms_b, carry = state
            cms_a = scores(2 * n, s_a)
            carry = consume(jnp.where(n == 0, 2 * i, 2 * n - 1), s_b, cms_b, carry)
            cms_b = scores(2 * n + 1, s_b)
            return cms_b, consume(2 * n, s_a, cms_a, carry)

        if i > 0:
            cms_b, carry = lax.fori_loop(0, i, body, (cms_b, carry))
        pending = (i, 2 * i if i == 0 else 2 * i - 1, s_b, cms_b, carry)
    finish(*pending)


def _mla_attention(qat, ka, vat):
    t = ka.shape[0]
    ka3 = ka.reshape(BATCH, SEQ, ka.shape[1])
    return pl.pallas_call(
        _mla_kernel,
        grid=(BATCH, MLA_HEADS // 2),
        in_specs=[pl.BlockSpec((1, MXU_DIM, SEQ), lambda b, c: (b, c, 0)),
                  pl.BlockSpec((1, SEQ, MXU_DIM), lambda b, c: (b, 0, c)),
                  pl.BlockSpec((1, LANES, SEQ), lambda b, c: (b, c, 0))],
        out_specs=pl.BlockSpec((SEQ, LANES), lambda b, c: (b, c)),
        out_shape=jax.ShapeDtypeStruct((t, MLA_HEADS * MLA_V), BF16),
        scratch_shapes=[pltpu.VMEM((2, ATT_TILE, ATT_TILE_Q), F32)] * 4,
        compiler_params=_params(("parallel", "parallel")),
        name="mla_attention",
    )(qat, ka3, vat)


DIL_QBLK = 2 * DIL_BLOCK
DIL_HALO = DIL_BLOCK
DIL_GROUP = 8


def _dil_kernel(q_ref, k_ref, v_ref, mb_ref, o_ref, lse_ref, vt_scr, *, nblk):
    qb, halo = DIL_QBLK, DIL_HALO
    ncls = q_ref.shape[1]
    lane = lax.broadcasted_iota(jnp.int32, (1, LANES), 1)
    low = lane < HEAD_DIM
    for g in range(ncls):
        vt_scr[g] = v_ref[0, g].astype(F32).T.astype(BF16)

    def blocks(items):
        work = []
        for g, r0, first in items:
            q = q_ref[0, g, pl.ds(r0, qb), :].astype(F32)
            if first:
                k0, nk = 0, qb
            else:
                k0 = r0 - halo if isinstance(r0, int) else pl.multiple_of(r0 - halo, halo)
                nk = qb + halo
            k = k_ref[0, g, pl.ds(k0, nk), :]
            for hh in range(2):
                qt = jnp.where(low if hh == 0 else ~low, q, 0.0).T.astype(BF16)
                s = jnp.dot(k, qt, preferred_element_type=F32) - (
                    mb_ref[0, hh, halo:, :] if first else mb_ref[0, hh])
                work.append((g, r0, k0, nk, hh, s))
        outs = []
        for g, r0, k0, nk, hh, s in work:
            m = jnp.max(s, axis=0, keepdims=True)
            p = jnp.exp2(s - m)
            den = jnp.sum(p, axis=0, keepdims=True)
            vt = vt_scr[g, hh * HEAD_DIM:(hh + 1) * HEAD_DIM, pl.ds(k0, nk)]
            o = jnp.dot(vt, p.astype(BF16), preferred_element_type=F32) / den
            lse = (m + jnp.log2(den)) * LN2
            outs.append((o, jnp.broadcast_to(lse, (HEAD_DIM, qb))))
        for n, (g, r0, _) in enumerate(items):
            pair = outs[2 * n:2 * n + 2]
            o_ref[0, g, pl.ds(r0, qb), :] = jnp.concatenate(
                [o for o, _ in pair], axis=0).T.astype(o_ref.dtype)
            lse_ref[0, g, pl.ds(r0, qb), :] = jnp.concatenate([l for _, l in pair], axis=0).T

    per = DIL_GROUP // ncls
    blocks([(g, n * qb, n == 0) for g in range(ncls) for n in range(per)])

    def body(n, carry):
        r0 = pl.multiple_of(per * n * qb, per * qb)
        blocks([(g, r0 + c * qb if c == 0 else pl.multiple_of(r0 + c * qb, qb), False)
                for g in range(ncls) for c in range(per)])
        return carry

    lax.fori_loop(1, nblk // per, body, 0)


def _dilated_branch(qc, kc, vc, mb, dil):
    l = SEQ // dil
    dw = DIL_HEADS * HEAD_DIM
    nblk = l // DIL_QBLK
    ncls = max(1, DIL_GROUP // nblk)
    assert dil % ncls == 0 and nblk % (DIL_GROUP // ncls) == 0
    blk = pl.BlockSpec((1, ncls, l, LANES), lambda c, b, r: (b, r, 0, c))
    return pl.pallas_call(
        functools.partial(_dil_kernel, nblk=nblk),
        grid=(DIL_HEADS // 2, BATCH, dil // ncls),
        in_specs=[blk, blk, blk,
                  pl.BlockSpec((1, 2, DIL_QBLK + DIL_HALO, DIL_QBLK), lambda c, b, r: (c, 0, 0, 0))],
        out_specs=[blk, blk],
        out_shape=[jax.ShapeDtypeStruct((BATCH, dil, l, dw), BF16),
                   jax.ShapeDtypeStruct((BATCH, dil, l, dw), F32)],
        scratch_shapes=[pltpu.VMEM((ncls, LANES, l), BF16)],
        compiler_params=_params(("parallel", "parallel", "parallel")),
        name=f"dilated_d{dil}",
    )(qc, kc, vc, mb)


def _dil_bias_table(dil, window):
    reach = window // dil
    ki = np.arange(DIL_QBLK + DIL_HALO)[:, None]
    qi = np.arange(DIL_QBLK)[None, :]
    step = DIL_HALO + qi - ki
    valid = (step >= 0) & (step <= reach)
    slopes = 2.0 ** (-8.0 * np.arange(1, DIL_HEADS + 1, dtype=np.float32) / DIL_HEADS)
    bias = slopes.astype(np.float32)[:, None, None] * (step * dil).astype(np.float32)[None]
    tab = np.where(valid[None], bias * LOG2E, np.inf).astype(np.float32)
    return jnp.asarray(tab.reshape(DIL_HEADS // 2, 2, DIL_QBLK + DIL_HALO, DIL_QBLK))


def _even_out_kernel(h_ref, oa_ref, *rest):
    branch_refs, w_ref, out_ref, stage_ref = rest[:-3], rest[-3], rest[-2], rest[-1]
    tm = h_ref.shape[0]
    vals = []
    for n, ((_, dil), ref) in enumerate(zip([pt for pt in DIL_PATTERNS for _ in range(2)],
                                            branch_refs)):
        if dil == 1:
            vals.append(ref[0, 0].astype(F32))
        else:
            slot = n - 2
            for r in range(dil):
                for c in range(ref.shape[-1] // LANES):
                    stage_ref[slot, c, pl.ds(r, tm // dil, stride=dil), :] = ref[
                        0, r, :, c * LANES:(c + 1) * LANES].astype(F32)
            vals.append(jnp.concatenate(
                [stage_ref[slot, c] for c in range(ref.shape[-1] // LANES)], axis=1))
    o1, l1, o2, l2, o3, l3 = vals
    lm = jnp.maximum(jnp.maximum(l1, l2), l3)
    w1, w2, w3 = jnp.exp(l1 - lm), jnp.exp(l2 - lm), jnp.exp(l3 - lm)
    mixed_b = (w1 * o1 + w2 * o2 + w3 * o3) / (w1 + w2 + w3)
    half = MLA_HEADS * MLA_V
    acc = jnp.dot(oa_ref[...], w_ref[:half, :], preferred_element_type=F32)
    acc += jnp.dot(mixed_b.astype(BF16), w_ref[half:, :], preferred_element_type=F32)
    out_ref[...] = h_ref[...] + acc


def _even_out(h2d, oa, branches, w_out):
    t = h2d.shape[0]
    tm = ROW_TILE
    tpb = SEQ // tm
    dw = DIL_HEADS * HEAD_DIM
    full = pl.BlockSpec((tm, D_MODEL), lambda i: (i, 0))
    specs, args = [], []
    for (_, dil), pair in zip(DIL_PATTERNS, branches):
        for a in pair:
            specs.append(pl.BlockSpec((1, dil, tm // dil, dw), lambda i: (i // tpb, 0, i % tpb, 0)))
            args.append(a)
    return pl.pallas_call(
        _even_out_kernel,
        grid=(t // tm,),
        in_specs=[full, pl.BlockSpec((tm, dw), lambda i: (i, 0))] + specs
        + [_const_spec(w_out.shape)],
        out_specs=full,
        out_shape=jax.ShapeDtypeStruct((t, D_MODEL), F32),
        scratch_shapes=[pltpu.VMEM((4, dw // LANES, tm, LANES), F32)],
        compiler_params=_params(("parallel",)),
        name="even_out_proj",
    )(h2d, oa, *args, w_out)


def _tail_kernel(*refs, with_out_proj):
    if with_out_proj:
        h_ref, o_ref, wout_ref, *refs = refs
    else:
        h_ref, *refs = refs
    p_ref, gff_ref, w1_ref, w2_ref, gple_ref, wg_ref, wp_ref, out_ref = refs
    h = h_ref[...]
    if with_out_proj:
        h = h + jnp.dot(o_ref[...], wout_ref[...], preferred_element_type=F32)
    n = _rms(h, gff_ref[...]).astype(BF16)
    chunk = D_MODEL
    for c in range(D_FF // chunk):
        u = jnp.maximum(jnp.dot(n, w1_ref[:, c * chunk:(c + 1) * chunk],
                                preferred_element_type=F32), 0.0)
        h = h + jnp.dot((u * u).astype(BF16), w2_ref[c * chunk:(c + 1) * chunk, :],
                        preferred_element_type=F32)
    n = _rms(h, gple_ref[...]).astype(BF16)
    gate = jax.nn.sigmoid(jnp.dot(n, wg_ref[...], preferred_element_type=F32))
    proj = jnp.dot(p_ref[...].astype(BF16), wp_ref[...], preferred_element_type=F32)
    out_ref[...] = h + gate * proj


def _tail(h2d, p2d, gff, w1, w2, gple, wg, wp, mix=None, w_out=None):
    t = h2d.shape[0]
    tm = ROW_TILE
    full = pl.BlockSpec((tm, D_MODEL), lambda i: (i, 0))
    with_out = mix is not None
    head_specs = [full, full, _const_spec(w_out.shape, single=True)] if with_out else [full]
    head_args = (h2d, mix, w_out) if with_out else (h2d,)
    consts = (gff, w1, w2, gple, wg, wp)
    return pl.pallas_call(
        functools.partial(_tail_kernel, with_out_proj=with_out),
        grid=(t // tm,),
        in_specs=head_specs + [pl.BlockSpec((tm, PLE_DIM), lambda i: (i, 0))]
        + [_const_spec(c.shape, single=True) for c in consts],
        out_specs=full,
        out_shape=jax.ShapeDtypeStruct((t, D_MODEL), F32),
        compiler_params=_params(("parallel",)),
        name="tail_with_out_proj" if with_out else "tail",
    )(*head_args, p2d, *consts)


def _odd_proj_kernel(x_ref, gmix_ref, win_ref, gq_ref, gk_ref, gm64_ref,
                     qt_ref, k_ref, vt_ref, km_ref):
    hn = _rms(x_ref[...], gmix_ref[...]).astype(BF16)
    z = jnp.dot(hn, win_ref[...], preferred_element_type=F32)
    dw = MOBA_HEADS * HEAD_DIM
    gq = gq_ref[...]
    gk = gk_ref[...]
    tm = x_ref.shape[0]
    for c in range(dw // MXU_DIM):
        lo = c * MXU_DIM
        qt_ref[0, lo:lo + MXU_DIM, :] = (
            _group_rms(z[:, lo:lo + MXU_DIM], gm64_ref[...]) * gq[:, lo:lo + MXU_DIM]
        ).T.astype(BF16)
        k = _group_rms(z[:, dw + lo:dw + lo + MXU_DIM], gm64_ref[...]) * gk[:, lo:lo + MXU_DIM]
        k_ref[:, lo:lo + MXU_DIM] = k.astype(BF16)
        for r in range(tm // MOBA_BLOCK):
            km_ref[r, :, lo:lo + MXU_DIM] = jnp.mean(
                k[r * MOBA_BLOCK:(r + 1) * MOBA_BLOCK], axis=0, keepdims=True)
    vt_ref[0] = z[:, 2 * dw:].T.astype(BF16)


def _odd_proj(h2d, gmix, win, gq, gk, gm64):
    t = h2d.shape[0]
    tm = ROW_TILE
    dw = MOBA_HEADS * HEAD_DIM
    consts = (gmix, win, gq, gk, gm64)
    full = pl.BlockSpec((tm, dw), lambda i: (i, 0))
    tpb = SEQ // tm
    tspec = pl.BlockSpec((1, dw, tm), lambda i: (i // tpb, 0, i % tpb))
    return pl.pallas_call(
        _odd_proj_kernel,
        grid=(t // tm,),
        in_specs=[pl.BlockSpec((tm, D_MODEL), lambda i: (i, 0))] + [_const_spec(c.shape) for c in consts],
        out_specs=[tspec, full, tspec,
                   pl.BlockSpec((tm // MOBA_BLOCK, 1, dw), lambda i: (i, 0, 0))],
        out_shape=[jax.ShapeDtypeStruct((BATCH, dw, SEQ), BF16), jax.ShapeDtypeStruct((t, dw), BF16),
                   jax.ShapeDtypeStruct((BATCH, dw, SEQ), BF16),
                   jax.ShapeDtypeStruct((t // MOBA_BLOCK, 1, dw), F32)],
        compiler_params=_params(("parallel",)),
        name="odd_proj",
    )(h2d, *consts)


def _moba_kernel(qt_ref, k_ref, vt_ref, km_ref, sl_ref, kb_ref, o_ref, *bufs):
    tq, tk = ATT_TILE_Q, MOBA_BLOCK
    nb = SEQ // MOBA_BLOCK
    visible = (lax.broadcasted_iota(jnp.int32, (tk, tq), 0)
               <= lax.broadcasted_iota(jnp.int32, (tk, tq), 1))
    visible_half = (lax.broadcasted_iota(jnp.int32, (tk, tk), 0)
                    <= lax.broadcasted_iota(jnp.int32, (tk, tk), 1))
    row = lax.broadcasted_iota(jnp.int32, (LANES, tq), 0)
    blk_id = lax.broadcasted_iota(jnp.int32, (nb, tq), 0)
    half = lax.broadcasted_iota(jnp.int32, (1, tq), 1) // tk
    kbias = kb_ref[0]
    slopes = [sl_ref[0, :, hh * HEAD_DIM:hh * HEAD_DIM + 1] for hh in range(2)]
    km_terms, resid = [], km_ref[0]
    for _ in range(KM_PARTS):
        km_terms.append(resid.astype(BF16))
        resid = resid - km_terms[-1].astype(F32)
    km_cat = jnp.concatenate(km_terms, axis=0)
    km_cat = jnp.concatenate([km_cat, jnp.zeros_like(km_cat)], axis=1)
    empty_half = (jnp.full((1, tk), MAX_FLOOR, F32), jnp.zeros((1, tk), F32),
                  jnp.zeros((HEAD_DIM, tk), F32))

    def prepare(i):
        own = 2 * i + half
        q_t = qt_ref[0, :, i * tq:(i + 1) * tq]
        heads = []
        for hh in range(2):
            qt = jnp.where((row >= HEAD_DIM * hh) & (row < HEAD_DIM * (hh + 1)), q_t,
                           jnp.zeros_like(q_t))
            ones = jnp.where((row >= KB_PARTS * hh) & (row < KB_PARTS * (hh + 1)), 1.0, 0.0)
            qt = jnp.concatenate([qt, ones.astype(BF16)], axis=0)
            gate_terms = jnp.dot(km_cat, qt, preferred_element_type=F32)
            gate = sum(gate_terms[n * nb:(n + 1) * nb] for n in range(KM_PARTS))
            g = jnp.where(blk_id < own, gate, -jnp.inf)
            sel = []
            for _ in range(MOBA_TOPK):
                mx = jnp.max(g, axis=0, keepdims=True)
                idx = jnp.min(jnp.where(g == mx, blk_id, nb), axis=0, keepdims=True)
                sel.append(jnp.where(mx > -jnp.inf, idx, -1))
                g = jnp.where(blk_id == idx, -jnp.inf, g)
            heads.append((qt, sel))
        return heads, own

    def scores(tile, j, slot, causal=False, right_half=False):
        heads, _ = tile
        kt = jnp.concatenate([k_ref[0, pl.ds(_aligned(j * tk, tk), tk), :], kbias],
                             axis=1)
        cms = []
        for hh in range(2):
            qt = heads[hh][0]
            s = jnp.dot(kt, qt[:, tk:] if right_half else qt,
                        preferred_element_type=F32)
            if causal:
                s = jnp.where(visible_half if right_half else visible, s, -jnp.inf)
            if right_half:
                slot[hh, :, :tk] = s
            else:
                slot[hh] = s
            cms.append(jnp.max(s, axis=0, keepdims=True))
        return tuple(cms)

    def start_right_half(i, j, slot, cms):
        vt = vt_ref[0, :, pl.ds(_aligned(j * tk, tk), tk)]
        ps = [jnp.exp2(slot[hh, :, :tk] - cms[hh]) for hh in range(2)]
        new = []
        for hh in range(2):
            acc = jnp.dot(vt[hh * HEAD_DIM:(hh + 1) * HEAD_DIM, :], ps[hh].astype(BF16),
                          preferred_element_type=F32)
            off = slopes[hh] * jnp.asarray((2 * i - j) * tk, F32)
            seen = (cms[hh] - off, jnp.sum(ps[hh], axis=0, keepdims=True), acc)
            new.append(tuple(jnp.concatenate([e, s], axis=1) for e, s in zip(empty_half, seen)))
        return tuple(new)

    def consume(i, tile, j, slot, cms, carry):
        heads, own = tile
        vt = vt_ref[0, :, pl.ds(_aligned(j * tk, tk), tk)]
        ps, stats = [], []
        for hh in range(2):
            sel = heads[hh][1]
            m, l, _ = carry[hh]
            seen = (sel[0] == j) | (sel[1] == j) | (sel[2] == j) | (own == j)
            off = slopes[hh] * jnp.asarray((2 * i - j) * tk, F32)
            m_new = jnp.where(seen, jnp.maximum(m, cms[hh] - off), m)
            alpha = jnp.exp2(m - m_new)
            p = jnp.exp2(slot[hh] - jnp.where(seen, m_new + off, jnp.inf))
            stats.append((m_new, alpha * l + jnp.sum(p, axis=0, keepdims=True), alpha))
            ps.append(p.astype(BF16))
        new = []
        for hh in range(2):
            m_new, l, alpha = stats[hh]
            acc = alpha * carry[hh][2] + jnp.dot(vt[hh * HEAD_DIM:(hh + 1) * HEAD_DIM, :], ps[hh],
                                                 preferred_element_type=F32)
            new.append((m_new, l, acc))
        return tuple(new)

    def finish(i, tile, last, slot, cms, carry):
        carry = consume(i, tile, last, slot, cms, carry)
        o_t = jnp.concatenate([acc / l for (_, l, acc) in carry], axis=0)
        o_ref[i * tq:(i + 1) * tq, :] = o_t.T.astype(BF16)

    pending = None
    for i in range(SEQ // tq):
        s_a, s_b = bufs[2 * (i % 2)], bufs[2 * (i % 2) + 1]
        tile = prepare(i)
        cms_a = scores(tile, 2 * i + 1, s_a, causal=True, right_half=True)
        cms_b = scores(tile, 2 * i, s_b, causal=True)
        if pending is not None:
            finish(*pending)
        carry = start_right_half(i, 2 * i + 1, s_a, cms_a)

        def body(n, state, i=i, tile=tile, s_a=s_a, s_b=s_b):
            cms_b, carry = state
            cms_a = scores(tile, 2 * n, s_a)
            carry = consume(i, tile, jnp.where(n == 0, 2 * i, 2 * n - 1), s_b, cms_b, carry)
            cms_b = scores(tile, 2 * n + 1, s_b)
            return cms_b, consume(i, tile, 2 * n, s_a, cms_a, carry)

        if i > 0:
            cms_b, carry = lax.fori_loop(0, i, body, (cms_b, carry))
        pending = (i, tile, 2 * i if i == 0 else 2 * i - 1, s_b, cms_b, carry)
    finish(*pending)


def _moba_attention(qt3, k, vt3, km, slopes, kbias):
    t = k.shape[0]
    nb = SEQ // MOBA_BLOCK
    dw = MOBA_HEADS * HEAD_DIM
    k3 = k.reshape(BATCH, SEQ, dw)
    km3 = km.reshape(BATCH, nb, dw)
    tspec = pl.BlockSpec((1, LANES, SEQ), lambda b, c: (b, c, 0))
    return pl.pallas_call(
        _moba_kernel,
        grid=(BATCH, MOBA_HEADS // 2),
        in_specs=[tspec,
                  pl.BlockSpec((1, SEQ, LANES), lambda b, c: (b, 0, c)),
                  tspec,
                  pl.BlockSpec((1, nb, LANES), lambda b, c: (b, 0, c)),
                  pl.BlockSpec((1, 1, LANES), lambda b, c: (c, 0, 0)),
                  pl.BlockSpec((1, MOBA_BLOCK, LANES), lambda b, c: (c, 0, 0))],
        out_specs=pl.BlockSpec((SEQ, LANES), lambda b, c: (b, c)),
        out_shape=jax.ShapeDtypeStruct((t, dw), BF16),
        scratch_shapes=[pltpu.VMEM((2, MOBA_BLOCK, ATT_TILE_Q), F32)] * 4,
        compiler_params=_params(("parallel", "parallel")),
        name="moba_attention",
    )(qt3, k3, vt3, km3, slopes, kbias)


def _group_matrix(sizes, width):
    g = np.zeros((width, width), np.float32)
    lo = 0
    for n in sizes:
        g[lo:lo + n, lo:lo + n] = 1.0 / n
        lo += n
    return jnp.asarray(g, BF16)


def _alibi_slopes_log2(n_heads):
    slopes = 2.0 ** (-8.0 * np.arange(1, n_heads + 1, dtype=np.float32) / n_heads)
    return slopes.astype(np.float64) * LOG2E


def _pair_slopes(n_heads):
    tab = np.repeat(_alibi_slopes_log2(n_heads).astype(np.float32), HEAD_DIM)
    return jnp.asarray(tab.reshape(n_heads // 2, 1, LANES))


def _pair_key_bias(n_heads, block):
    bias = _alibi_slopes_log2(n_heads)[:, None] * np.arange(block, dtype=np.float64)[None, :]
    tab = np.zeros((n_heads // 2, block, LANES), np.float32)
    for part in range(KB_PARTS):
        term = bias.astype(jnp.bfloat16).astype(np.float64)
        bias = bias - term
        for h in range(n_heads):
            tab[h // 2, :, KB_PARTS * (h % 2) + part] = term[h]
    return jnp.asarray(tab, BF16)


def _even_weights(w_in, w_uq, w_ukv, qn_nope, qn_rope, kn_nope, kn_rope, dil_qn, dil_kn):
    o1 = Q_LORA
    o2 = o1 + KV_LORA
    o3 = o2 + MLA_ROPE
    kr = w_in[:, o2:o3]
    pad64 = jnp.zeros((D_MODEL, LANES - 2 * MLA_ROPE), F32)
    win = jnp.concatenate([w_in[:, :o2], kr, kr, pad64, w_in[:, o3:]], axis=1).astype(BF16)

    qd = MLA_NOPE + MLA_ROPE
    a_scale = qd ** -0.5
    zq = jnp.zeros((Q_LORA, LANES - 2 * MLA_ROPE), F32)
    zg = jnp.zeros((LANES - 2 * MLA_ROPE,), F32)
    cols, gains = [], []
    for c in range(MLA_HEADS // 2):
        h0, h1 = 2 * c, 2 * c + 1
        cols += [w_uq[:, h0 * qd:h0 * qd + MLA_NOPE], w_uq[:, h1 * qd:h1 * qd + MLA_NOPE],
                 w_uq[:, h0 * qd + MLA_NOPE:(h0 + 1) * qd], w_uq[:, h1 * qd + MLA_NOPE:(h1 + 1) * qd], zq]
        gains += [qn_nope, qn_nope, qn_rope, qn_rope, zg]
    wuq = jnp.concatenate(cols, axis=1).astype(BF16)
    gq = (jnp.concatenate(gains) * (a_scale * LOG2E))[None, :]

    kvd = MLA_NOPE + MLA_V
    kcols = [w_ukv[:, h * kvd:h * kvd + MLA_NOPE] for h in range(MLA_HEADS)]
    vcols = [w_ukv[:, h * kvd + MLA_NOPE:(h + 1) * kvd] for h in range(MLA_HEADS)]
    wukv = jnp.concatenate(kcols + vcols, axis=1).astype(BF16)
    gkn = jnp.tile(kn_nope, MLA_HEADS)[None, :]
    gkr = jnp.concatenate([kn_rope, kn_rope, zg])[None, :]
    gqb = (jnp.tile(dil_qn, DIL_HEADS) * (HEAD_DIM ** -0.5 * LOG2E))[None, :]
    gkb = jnp.tile(dil_kn, DIL_HEADS)[None, :]
    return win, wuq, wukv, gq, gkn, gkr, gqb, gkb


def kernel(x, p, positions, e_w_in, e_cq_norm, e_ckv_norm, e_w_uq, e_w_ukv, e_qn_nope, e_qn_rope, e_kn_nope, e_kn_rope, e_dil_qn, e_dil_kn, e_w_out, o_w_in, o_qn, o_kn, o_w_out, mix_norm, ff_norm, w_ff1, w_ff2, ple_norm, w_ple_gate, w_ple_proj):
    t = BATCH * SEQ
    h = x.reshape(t, D_MODEL)
    pos2d = positions.reshape(t, 1)
    p2d = p.reshape(DEPTH, t, PLE_DIM)

    gmq = _group_matrix((MLA_NOPE, MLA_NOPE, MLA_ROPE, MLA_ROPE, LANES - 2 * MLA_ROPE), MXU_DIM)
    gm64 = _group_matrix((HEAD_DIM,) * (MXU_DIM // HEAD_DIM), MXU_DIM)
    gm32 = _group_matrix((MLA_ROPE,) * (LANES // MLA_ROPE), LANES)
    half = MLA_ROPE // 2
    invf = ROPE_THETA ** (-(jnp.arange(LANES) % half).astype(F32) / half)
    invf = invf[None, :]

    for i in range(DEPTH):
        j = i // 2
        gmix = mix_norm[i][None, :]
        if i % 2 == 0:
            win, wuq, wukv, gq, gkn, gkr, gqb, gkb = _even_weights(
                e_w_in[j], e_w_uq[j], e_w_ukv[j], e_qn_nope[j], e_qn_rope[j], e_kn_nope[j],
                e_kn_rope[j], e_dil_qn[j], e_dil_kn[j])
            qa, ka, vat, *qkv_b = _even_proj(
                h, pos2d, gmix, win, e_cq_norm[j][None, :], e_ckv_norm[j][None, :], wuq, wukv,
                gq, gkn, gkr, gqb, gkb, gmq, gm64, gm32, invf)
            oa = _mla_attention(qa, ka, vat)
            branches = [_dilated_branch(*qkv_b[3 * n:3 * n + 3], _dil_bias_table(d, w), d)
                        for n, (w, d) in enumerate(DIL_PATTERNS)]
            h = _even_out(h, oa, branches, e_w_out[j].astype(BF16))
            mix = {}
        else:
            gq = (jnp.tile(o_qn[j], MOBA_HEADS) * (HEAD_DIM ** -0.5 * LOG2E))[None, :]
            gk = jnp.tile(o_kn[j], MOBA_HEADS)[None, :]
            q, k, vt, km = _odd_proj(h, gmix, o_w_in[j].astype(BF16), gq, gk, gm64)
            o = _moba_attention(q, k, vt, km, _pair_slopes(MOBA_HEADS),
                                _pair_key_bias(MOBA_HEADS, MOBA_BLOCK))
            mix = {"mix": o, "w_out": o_w_out[j].astype(BF16)}
        h = _tail(h, p2d[i], ff_norm[i][None, :], w_ff1[i].astype(BF16), w_ff2[i].astype(BF16),
                  ple_norm[i][None, :], w_ple_gate[i].astype(BF16), w_ple_proj[i].astype(BF16),
                  **mix)
    return h.reshape(BATCH, SEQ, D_MODEL)
```

```python
import functools

import numpy as np
import jax
import jax.numpy as jnp
from jax import lax
from jax.experimental import pallas as pl
from jax.experimental.pallas import tpu as pltpu

D_MODEL = 1024
BATCH = 8
SEQ = 4096
DEPTH = 2
HEAD_DIM = 64
EPS = 1e-6
MLA_HEADS = 8
MLA_NOPE = 64
MLA_ROPE = 32
MLA_V = 64
Q_LORA = 384
KV_LORA = 256
ROPE_THETA = 10000.0
DIL_HEADS = 8
DIL_PATTERNS = ((128, 1), (512, 4), (2048, 16))
DIL_BLOCK = 128
MOBA_HEADS = 16
MOBA_BLOCK = 256
MOBA_TOPK = 3
D_FF = 4 * D_MODEL
PLE_DIM = 256

LANES = 128
MXU_DIM = 256
VMEM_LIMIT = 56 * 1024 * 1024
ROW_TILE = 512
ATT_TILE = 256
ATT_TILE_Q = 512
MAX_FLOOR = -1e30
KB_PARTS = 3
KM_PARTS = 3
LOG2E = float(np.log2(np.e))
LN2 = float(np.log(2.0))

F32 = jnp.float32
BF16 = jnp.bfloat16


def _const_spec(shape, single=False):
    nd = len(shape)
    mode = {"pipeline_mode": pl.Buffered(1)} if single else {}
    return pl.BlockSpec(shape, lambda *_: (0,) * nd, **mode)


def _params(sem):
    return pltpu.CompilerParams(dimension_semantics=sem, vmem_limit_bytes=VMEM_LIMIT)


def _rms(x, g):
    return x * lax.rsqrt(jnp.mean(x * x, axis=-1, keepdims=True) + EPS) * g


def _group_rms(x, gmat):
    ms = jnp.dot((x * x).astype(BF16), gmat, preferred_element_type=F32)
    return x * lax.rsqrt(ms + EPS)


def _even_proj_kernel(x_ref, pos_ref, gmix_ref, win_ref, gcq_ref, gckv_ref, wuq_ref, wukv_ref,
                      gq_ref, gkn_ref, gkr_ref, gqb_ref, gkb_ref, gmq_ref, gm64_ref, gm32_ref,
                      invf_ref, qat_ref, ka_ref, vat_ref, *rest):
    dil_refs, stage_ref = rest[:-1], rest[-1]
    hn = _rms(x_ref[...], gmix_ref[...]).astype(BF16)
    z = jnp.dot(hn, win_ref[...], preferred_element_type=F32)
    o1 = Q_LORA
    o2 = o1 + KV_LORA
    o3 = o2 + LANES
    dw = DIL_HEADS * HEAD_DIM
    c_q = _rms(z[:, :o1], gcq_ref[...]).astype(BF16)
    c_kv = _rms(z[:, o1:o2], gckv_ref[...]).astype(BF16)

    ang = pos_ref[...].astype(F32) * invf_ref[...]
    cos = jnp.cos(ang)
    sin = jnp.sin(ang)
    lane = lax.broadcasted_iota(jnp.int32, (1, LANES), 1)
    first_half = (lane % MLA_ROPE) < (MLA_ROPE // 2)

    def rope(xr):
        rot = jnp.where(first_half, -pltpu.roll(xr, LANES - MLA_ROPE // 2, 1),
                        pltpu.roll(xr, MLA_ROPE // 2, 1))
        return xr * cos + rot * sin

    q = jnp.dot(c_q, wuq_ref[...], preferred_element_type=F32)
    gq = gq_ref[...]
    for c in range(MLA_HEADS // 2):
        lo = c * MXU_DIM
        qc = _group_rms(q[:, lo:lo + MXU_DIM], gmq_ref[...]) * gq[:, lo:lo + MXU_DIM]
        qat_ref[0, lo:lo + LANES, :] = qc[:, :LANES].T.astype(BF16)
        qat_ref[0, lo + LANES:lo + MXU_DIM, :] = rope(qc[:, LANES:]).T.astype(BF16)

    kv = jnp.dot(c_kv, wukv_ref[...], preferred_element_type=F32)
    k_rope = rope(_group_rms(z[:, o2:o3], gm32_ref[...]) * gkr_ref[...]).astype(BF16)
    gkn = gkn_ref[...]
    for c in range(MLA_HEADS // 2):
        lo = c * LANES
        if c % 2 == 0:
            kn2 = _group_rms(kv[:, lo:lo + MXU_DIM], gm64_ref[...]) * gkn[:, lo:lo + MXU_DIM]
        kn = kn2[:, (c % 2) * LANES:(c % 2 + 1) * LANES]
        ka_ref[:, c * MXU_DIM:c * MXU_DIM + LANES] = kn.astype(BF16)
        ka_ref[:, c * MXU_DIM + LANES:(c + 1) * MXU_DIM] = k_rope
    vat_ref[0] = kv[:, dw:].T.astype(BF16)

    gqb = gqb_ref[...]
    gkb = gkb_ref[...]
    qkv = []
    for c in range(dw // MXU_DIM):
        lo = c * MXU_DIM
        qkv.append((
            _group_rms(z[:, o3 + lo:o3 + lo + MXU_DIM], gm64_ref[...]) * gqb[:, lo:lo + MXU_DIM],
            _group_rms(z[:, o3 + dw + lo:o3 + dw + lo + MXU_DIM], gm64_ref[...])
            * gkb[:, lo:lo + MXU_DIM],
            z[:, o3 + 2 * dw + lo:o3 + 2 * dw + lo + MXU_DIM]))
    tm = x_ref.shape[0]
    for a in range(3):
        for c in range(dw // LANES):
            half = (c % 2) * LANES
            stage_ref[a, c] = qkv[c // 2][a][:, half:half + LANES]
            for (_, dil), out_ref in zip(DIL_PATTERNS, dil_refs[a::3]):
                for r in range(dil):
                    out_ref[0, r, :, c * LANES:(c + 1) * LANES] = stage_ref[
                        a, c, pl.ds(r, tm // dil, stride=dil), :].astype(BF16)


def _even_proj(h2d, pos2d, gmix, win, gcq, gckv, wuq, wukv, gq, gkn, gkr, gqb, gkb,
               gmq, gm64, gm32, invf):
    t = h2d.shape[0]
    tm = ROW_TILE
    consts = (gmix, win, gcq, gckv, wuq, wukv, gq, gkn, gkr, gqb, gkb, gmq, gm64, gm32, invf)
    tpb = SEQ // tm
    dw = DIL_HEADS * HEAD_DIM
    out_specs = [pl.BlockSpec((1, 1024, tm), lambda i: (i // tpb, 0, i % tpb)),
                 pl.BlockSpec((tm, 1024), lambda i: (i, 0)),
                 pl.BlockSpec((1, MLA_HEADS * MLA_V, tm), lambda i: (i // tpb, 0, i % tpb))]
    out_shape = [jax.ShapeDtypeStruct((BATCH, 1024, SEQ), BF16),
                 jax.ShapeDtypeStruct((t, 1024), BF16),
                 jax.ShapeDtypeStruct((BATCH, MLA_HEADS * MLA_V, SEQ), BF16)]
    for (_, dil) in DIL_PATTERNS:
        for _ in range(3):
            out_specs.append(pl.BlockSpec((1, dil, tm // dil, dw),
                                          lambda i: (i // tpb, 0, i % tpb, 0)))
            out_shape.append(jax.ShapeDtypeStruct((BATCH, dil, SEQ // dil, dw), BF16))
    return pl.pallas_call(
        _even_proj_kernel,
        grid=(t // tm,),
        in_specs=[pl.BlockSpec((tm, D_MODEL), lambda i: (i, 0)),
                  pl.BlockSpec((tm, 1), lambda i: (i, 0))] + [_const_spec(c.shape) for c in consts],
        out_specs=out_specs,
        out_shape=out_shape,
        scratch_shapes=[pltpu.VMEM((3, dw // LANES, tm, LANES), F32)],
        compiler_params=_params(("parallel",)),
        name="even_proj",
    )(h2d, pos2d, *consts)


def _mla_kernel(qt_ref, k_ref, vt_ref, o_ref, *bufs):
    tq, tk = ATT_TILE_Q, ATT_TILE
    feat = lax.broadcasted_iota(jnp.int32, (MXU_DIM, 1), 0)
    visible = (lax.broadcasted_iota(jnp.int32, (tk, tq), 0)
               <= lax.broadcasted_iota(jnp.int32, (tk, tq), 1))
    visible_half = (lax.broadcasted_iota(jnp.int32, (tk, tk), 0)
                    <= lax.broadcasted_iota(jnp.int32, (tk, tk), 1))
    head_rows = []
    for hh in range(2):
        nope = (feat >= MLA_NOPE * hh) & (feat < MLA_NOPE * (hh + 1))
        rope = (feat >= LANES + MLA_ROPE * hh) & (feat < LANES + MLA_ROPE * (hh + 1))
        head_rows.append(nope | rope)
    empty_half = (jnp.full((1, tk), -jnp.inf, F32), jnp.zeros((1, tk), F32),
                  jnp.zeros((MLA_V, tk), F32))

    def consume(j, slot, cms, carry):
        vt = vt_ref[0, :, j * tk:(j + 1) * tk]
        ps, stats = [], []
        for hh in range(2):
            m, l, _ = carry[hh]
            m_new = jnp.maximum(m, cms[hh])
            alpha = jnp.exp2(m - m_new)
            p = jnp.exp2(slot[hh] - m_new)
            stats.append((m_new, alpha * l + jnp.sum(p, axis=0, keepdims=True), alpha))
            ps.append(p.astype(BF16))
        new = []
        for hh in range(2):
            m_new, l, alpha = stats[hh]
            acc = alpha * carry[hh][2] + jnp.dot(vt[hh * MLA_V:(hh + 1) * MLA_V, :], ps[hh],
                                                 preferred_element_type=F32)
            new.append((m_new, l, acc))
        return tuple(new)

    def start_right_half(j, slot, cms):
        vt = vt_ref[0, :, j * tk:(j + 1) * tk]
        ps = [jnp.exp2(slot[hh, :, :tk] - cms[hh]) for hh in range(2)]
        new = []
        for hh in range(2):
            acc = jnp.dot(vt[hh * MLA_V:(hh + 1) * MLA_V, :], ps[hh].astype(BF16),
                          preferred_element_type=F32)
            seen = (cms[hh], jnp.sum(ps[hh], axis=0, keepdims=True), acc)
            new.append(tuple(jnp.concatenate([e, s], axis=1) for e, s in zip(empty_half, seen)))
        return tuple(new)

    def finish(i, last, slot, cms, carry):
        carry = consume(last, slot, cms, carry)
        o_t = jnp.concatenate([acc / l for (_, l, acc) in carry], axis=0)
        o_ref[i * tq:(i + 1) * tq, :] = o_t.T.astype(BF16)

    pending = None
    for i in range(SEQ // tq):
        s_a, s_b = bufs[2 * (i % 2)], bufs[2 * (i % 2) + 1]
        qt = qt_ref[0, :, i * tq:(i + 1) * tq]
        qts = [jnp.where(rows, qt, jnp.zeros_like(qt)) for rows in head_rows]

        def scores(j, slot, qts=qts, causal=False, right_half=False):
            kt = k_ref[0, j * tk:(j + 1) * tk, :]
            cms = []
            for hh in range(2):
                s = jnp.dot(kt, qts[hh][:, tk:] if right_half else qts[hh],
                            preferred_element_type=F32)
                if causal:
                    s = jnp.where(visible_half if right_half else visible, s, -jnp.inf)
                if right_half:
                    slot[hh, :, :tk] = s
                else:
                    slot[hh] = s
                cms.append(jnp.max(s, axis=0, keepdims=True))
            return tuple(cms)

        cms_a = scores(2 * i + 1, s_a, causal=True, right_half=True)
        cms_b = scores(2 * i, s_b, causal=True)
        if pending is not None:
            finish(*pending)
        carry = start_right_half(2 * i + 1, s_a, cms_a)

        for n in range(i):
            cms_a = scores(2 * n, s_a)
            carry = consume(2 * i if n == 0 else 2 * n - 1, s_b, cms_b, carry)
            cms_b = scores(2 * n + 1, s_b)
            carry = consume(2 * n, s_a, cms_a, carry)
        pending = (i, 2 * i if i == 0 else 2 * i - 1, s_b, cms_b, carry)
    finish(*pending)


def _mla_attention(qat, ka, vat):
    t = ka.shape[0]
    ka3 = ka.reshape(BATCH, SEQ, ka.shape[1])
    return pl.pallas_call(
        _mla_kernel,
        grid=(BATCH, MLA_HEADS // 2),
        in_specs=[pl.BlockSpec((1, MXU_DIM, SEQ), lambda b, c: (b, c, 0)),
                  pl.BlockSpec((1, SEQ, MXU_DIM), lambda b, c: (b, 0, c)),
                  pl.BlockSpec((1, LANES, SEQ), lambda b, c: (b, c, 0))],
        out_specs=pl.BlockSpec((SEQ, LANES), lambda b, c: (b, c)),
        out_shape=jax.ShapeDtypeStruct((t, MLA_HEADS * MLA_V), BF16),
        scratch_shapes=[pltpu.VMEM((2, ATT_TILE, ATT_TILE_Q), F32)] * 4,
        compiler_params=_params(("parallel", "parallel")),
        name="mla_attention",
    )(qat, ka3, vat)


DIL_QBLK = 2 * DIL_BLOCK
DIL_HALO = DIL_BLOCK
DIL_GROUP = 8


def _dil_kernel(q_ref, k_ref, v_ref, mb_ref, o_ref, lse_ref, vt_scr, *, nblk):
    qb, halo = DIL_QBLK, DIL_HALO
    ncls = q_ref.shape[1]
    lane = lax.broadcasted_iota(jnp.int32, (1, LANES), 1)
    low = lane < HEAD_DIM
    for g in range(ncls):
        vt_scr[g] = v_ref[0, g].astype(F32).T.astype(BF16)

    def blocks(items):
        work = []
        for g, r0, first in items:
            q = q_ref[0, g, pl.ds(r0, qb), :].astype(F32)
            if first:
                k0, nk = 0, qb
            else:
                k0 = r0 - halo if isinstance(r0, int) else pl.multiple_of(r0 - halo, halo)
                nk = qb + halo
            k = k_ref[0, g, pl.ds(k0, nk), :]
            for hh in range(2):
                qt = jnp.where(low if hh == 0 else ~low, q, 0.0).T.astype(BF16)
                s = jnp.dot(k, qt, preferred_element_type=F32) - (
                    mb_ref[0, hh, halo:, :] if first else mb_ref[0, hh])
                work.append((g, r0, k0, nk, hh, s))
        outs = []
        for g, r0, k0, nk, hh, s in work:
            m = jnp.max(s, axis=0, keepdims=True)
            p = jnp.exp2(s - m)
            den = jnp.sum(p, axis=0, keepdims=True)
            vt = vt_scr[g, hh * HEAD_DIM:(hh + 1) * HEAD_DIM, pl.ds(k0, nk)]
            o = jnp.dot(vt, p.astype(BF16), preferred_element_type=F32) / den
            lse = (m + jnp.log2(den)) * LN2
            outs.append((o, jnp.broadcast_to(lse, (HEAD_DIM, qb))))
        for n, (g, r0, _) in enumerate(items):
            pair = outs[2 * n:2 * n + 2]
            o_ref[0, g, pl.ds(r0, qb), :] = jnp.concatenate(
                [o for o, _ in pair], axis=0).T.astype(o_ref.dtype)
            lse_ref[0, g, pl.ds(r0, qb), :] = jnp.concatenate([l for _, l in pair], axis=0).T

    per = DIL_GROUP // ncls
    blocks([(g, n * qb, n == 0) for g in range(ncls) for n in range(per)])

    def body(n, carry):
        r0 = pl.multiple_of(per * n * qb, per * qb)
        blocks([(g, r0 + c * qb if c == 0 else pl.multiple_of(r0 + c * qb, qb), False)
                for g in range(ncls) for c in range(per)])
        return carry

    lax.fori_loop(1, nblk // per, body, 0)


def _dilated_branch(qc, kc, vc, mb, dil):
    l = SEQ // dil
    dw = DIL_HEADS * HEAD_DIM
    nblk = l // DIL_QBLK
    ncls = max(1, DIL_GROUP // nblk)
    assert dil % ncls == 0 and nblk % (DIL_GROUP // ncls) == 0
    blk = pl.BlockSpec((1, ncls, l, LANES), lambda c, b, r: (b, r, 0, c))
    return pl.pallas_call(
        functools.partial(_dil_kernel, nblk=nblk),
        grid=(DIL_HEADS // 2, BATCH, dil // ncls),
        in_specs=[blk, blk, blk,
                  pl.BlockSpec((1, 2, DIL_QBLK + DIL_HALO, DIL_QBLK), lambda c, b, r: (c, 0, 0, 0))],
        out_specs=[blk, blk],
        out_shape=[jax.ShapeDtypeStruct((BATCH, dil, l, dw), BF16),
                   jax.ShapeDtypeStruct((BATCH, dil, l, dw), F32)],
        scratch_shapes=[pltpu.VMEM((ncls, LANES, l), BF16)],
        compiler_params=_params(("parallel", "parallel", "parallel")),
        name=f"dilated_d{dil}",
    )(qc, kc, vc, mb)


def _dil_bias_table(dil, window):
    reach = window // dil
    ki = np.arange(DIL_QBLK + DIL_HALO)[:, None]
    qi = np.arange(DIL_QBLK)[None, :]
    step = DIL_HALO + qi - ki
    valid = (step >= 0) & (step <= reach)
    slopes = 2.0 ** (-8.0 * np.arange(1, DIL_HEADS + 1, dtype=np.float32) / DIL_HEADS)
    bias = slopes.astype(np.float32)[:, None, None] * (step * dil).astype(np.float32)[None]
    tab = np.where(valid[None], bias * LOG2E, np.inf).astype(np.float32)
    return jnp.asarray(tab.reshape(DIL_HEADS // 2, 2, DIL_QBLK + DIL_HALO, DIL_QBLK))


def _even_out_kernel(h_ref, oa_ref, *rest):
    branch_refs, w_ref, out_ref, stage_ref = rest[:-3], rest[-3], rest[-2], rest[-1]
    tm = h_ref.shape[0]
    vals = []
    for n, ((_, dil), ref) in enumerate(zip([pt for pt in DIL_PATTERNS for _ in range(2)],
                                            branch_refs)):
        if dil == 1:
            vals.append(ref[0, 0].astype(F32))
        else:
            slot = n - 2
            for r in range(dil):
                for c in range(ref.shape[-1] // LANES):
                    stage_ref[slot, c, pl.ds(r, tm // dil, stride=dil), :] = ref[
                        0, r, :, c * LANES:(c + 1) * LANES].astype(F32)
            vals.append(jnp.concatenate(
                [stage_ref[slot, c] for c in range(ref.shape[-1] // LANES)], axis=1))
    o1, l1, o2, l2, o3, l3 = vals
    lm = jnp.maximum(jnp.maximum(l1, l2), l3)
    w1, w2, w3 = jnp.exp(l1 - lm), jnp.exp(l2 - lm), jnp.exp(l3 - lm)
    mixed_b = (w1 * o1 + w2 * o2 + w3 * o3) / (w1 + w2 + w3)
    half = MLA_HEADS * MLA_V
    acc = jnp.dot(oa_ref[...], w_ref[:half, :], preferred_element_type=F32)
    acc += jnp.dot(mixed_b.astype(BF16), w_ref[half:, :], preferred_element_type=F32)
    out_ref[...] = h_ref[...] + acc


def _even_out(h2d, oa, branches, w_out):
    t = h2d.shape[0]
    tm = ROW_TILE
    tpb = SEQ // tm
    dw = DIL_HEADS * HEAD_DIM
    full = pl.BlockSpec((tm, D_MODEL), lambda i: (i, 0))
    specs, args = [], []
    for (_, dil), pair in zip(DIL_PATTERNS, branches):
        for a in pair:
            specs.append(pl.BlockSpec((1, dil, tm // dil, dw), lambda i: (i // tpb, 0, i % tpb, 0)))
            args.append(a)
    return pl.pallas_call(
        _even_out_kernel,
        grid=(t // tm,),
        in_specs=[full, pl.BlockSpec((tm, dw), lambda i: (i, 0))] + specs
        + [_const_spec(w_out.shape)],
        out_specs=full,
        out_shape=jax.ShapeDtypeStruct((t, D_MODEL), F32),
        scratch_shapes=[pltpu.VMEM((4, dw // LANES, tm, LANES), F32)],
        compiler_params=_params(("parallel",)),
        name="even_out_proj",
    )(h2d, oa, *args, w_out)


def _tail_kernel(*refs, with_out_proj):
    if with_out_proj:
        h_ref, o_ref, wout_ref, *refs = refs
    else:
        h_ref, *refs = refs
    p_ref, gff_ref, w1_ref, w2_ref, gple_ref, wg_ref, wp_ref, out_ref = refs
    h = h_ref[...]
    if with_out_proj:
        h = h + jnp.dot(o_ref[...], wout_ref[...], preferred_element_type=F32)
    n = _rms(h, gff_ref[...]).astype(BF16)
    chunk = D_MODEL
    for c in range(D_FF // chunk):
        u = jnp.maximum(jnp.dot(n, w1_ref[:, c * chunk:(c + 1) * chunk],
                                preferred_element_type=F32), 0.0)
        h = h + jnp.dot((u * u).astype(BF16), w2_ref[c * chunk:(c + 1) * chunk, :],
                        preferred_element_type=F32)
    n = _rms(h, gple_ref[...]).astype(BF16)
    gate = jax.nn.sigmoid(jnp.dot(n, wg_ref[...], preferred_element_type=F32))
    proj = jnp.dot(p_ref[...].astype(BF16), wp_ref[...], preferred_element_type=F32)
    out_ref[...] = h + gate * proj


def _tail(h2d, p2d, gff, w1, w2, gple, wg, wp, mix=None, w_out=None):
    t = h2d.shape[0]
    tm = ROW_TILE
    full = pl.BlockSpec((tm, D_MODEL), lambda i: (i, 0))
    with_out = mix is not None
    head_specs = [full, full, _const_spec(w_out.shape, single=True)] if with_out else [full]
    head_args = (h2d, mix, w_out) if with_out else (h2d,)
    consts = (gff, w1, w2, gple, wg, wp)
    return pl.pallas_call(
        functools.partial(_tail_kernel, with_out_proj=with_out),
        grid=(t // tm,),
        in_specs=head_specs + [pl.BlockSpec((tm, PLE_DIM), lambda i: (i, 0))]
        + [_const_spec(c.shape, single=True) for c in consts],
        out_specs=full,
        out_shape=jax.ShapeDtypeStruct((t, D_MODEL), F32),
        compiler_params=_params(("parallel",)),
        name="tail_with_out_proj" if with_out else "tail",
    )(*head_args, p2d, *consts)


def _odd_proj_kernel(x_ref, gmix_ref, win_ref, gq_ref, gk_ref, gm64_ref,
                     qt_ref, k_ref, vt_ref, km_ref):
    hn = _rms(x_ref[...], gmix_ref[...]).astype(BF16)
    z = jnp.dot(hn, win_ref[...], preferred_element_type=F32)
    dw = MOBA_HEADS * HEAD_DIM
    gq = gq_ref[...]
    gk = gk_ref[...]
    tm = x_ref.shape[0]
    for c in range(dw // MXU_DIM):
        lo = c * MXU_DIM
        qt_ref[0, lo:lo + MXU_DIM, :] = (
            _group_rms(z[:, lo:lo + MXU_DIM], gm64_ref[...]) * gq[:, lo:lo + MXU_DIM]
        ).T.astype(BF16)
        k = _group_rms(z[:, dw + lo:dw + lo + MXU_DIM], gm64_ref[...]) * gk[:, lo:lo + MXU_DIM]
        k_ref[:, lo:lo + MXU_DIM] = k.astype(BF16)
        for r in range(tm // MOBA_BLOCK):
            km_ref[r, :, lo:lo + MXU_DIM] = jnp.mean(
                k[r * MOBA_BLOCK:(r + 1) * MOBA_BLOCK], axis=0, keepdims=True)
    vt_ref[0] = z[:, 2 * dw:].T.astype(BF16)


def _odd_proj(h2d, gmix, win, gq, gk, gm64):
    t = h2d.shape[0]
    tm = ROW_TILE
    dw = MOBA_HEADS * HEAD_DIM
    consts = (gmix, win, gq, gk, gm64)
    full = pl.BlockSpec((tm, dw), lambda i: (i, 0))
    tpb = SEQ // tm
    tspec = pl.BlockSpec((1, dw, tm), lambda i: (i // tpb, 0, i % tpb))
    return pl.pallas_call(
        _odd_proj_kernel,
        grid=(t // tm,),
        in_specs=[pl.BlockSpec((tm, D_MODEL), lambda i: (i, 0))] + [_const_spec(c.shape) for c in consts],
        out_specs=[tspec, full, tspec,
                   pl.BlockSpec((tm // MOBA_BLOCK, 1, dw), lambda i: (i, 0, 0))],
        out_shape=[jax.ShapeDtypeStruct((BATCH, dw, SEQ), BF16), jax.ShapeDtypeStruct((t, dw), BF16),
                   jax.ShapeDtypeStruct((BATCH, dw, SEQ), BF16),
                   jax.ShapeDtypeStruct((t // MOBA_BLOCK, 1, dw), F32)],
        compiler_params=_params(("parallel",)),
        name="odd_proj",
    )(h2d, *consts)


def _moba_kernel(qt_ref, k_ref, vt_ref, km_ref, sl_ref, kb_ref, o_ref, *bufs):
    tq, tk = ATT_TILE_Q, MOBA_BLOCK
    nb = SEQ // MOBA_BLOCK
    visible = (lax.broadcasted_iota(jnp.int32, (tk, tq), 0)
               <= lax.broadcasted_iota(jnp.int32, (tk, tq), 1))
    visible_half = (lax.broadcasted_iota(jnp.int32, (tk, tk), 0)
                    <= lax.broadcasted_iota(jnp.int32, (tk, tk), 1))
    row = lax.broadcasted_iota(jnp.int32, (LANES, tq), 0)
    blk_id = lax.broadcasted_iota(jnp.int32, (nb, tq), 0)
    half = lax.broadcasted_iota(jnp.int32, (1, tq), 1) // tk
    kbias = kb_ref[0]
    slopes = [sl_ref[0, :, hh * HEAD_DIM:hh * HEAD_DIM + 1] for hh in range(2)]
    km_terms, resid = [], km_ref[0]
    for _ in range(KM_PARTS):
        km_terms.append(resid.astype(BF16))
        resid = resid - km_terms[-1].astype(F32)
    km_cat = jnp.concatenate(km_terms, axis=0)
    km_cat = jnp.concatenate([km_cat, jnp.zeros_like(km_cat)], axis=1)
    empty_half = (jnp.full((1, tk), MAX_FLOOR, F32), jnp.zeros((1, tk), F32),
                  jnp.zeros((HEAD_DIM, tk), F32))

    def prepare(i):
        own = 2 * i + half
        q_t = qt_ref[0, :, i * tq:(i + 1) * tq]
        heads = []
        for hh in range(2):
            qt = jnp.where((row >= HEAD_DIM * hh) & (row < HEAD_DIM * (hh + 1)), q_t,
                           jnp.zeros_like(q_t))
            ones = jnp.where((row >= KB_PARTS * hh) & (row < KB_PARTS * (hh + 1)), 1.0, 0.0)
            qt = jnp.concatenate([qt, ones.astype(BF16)], axis=0)
            gate_terms = jnp.dot(km_cat, qt, preferred_element_type=F32)
            gate = sum(gate_terms[n * nb:(n + 1) * nb] for n in range(KM_PARTS))
            g = jnp.where(blk_id < own, gate, -jnp.inf)
            sel = []
            for _ in range(MOBA_TOPK):
                mx = jnp.max(g, axis=0, keepdims=True)
                idx = jnp.min(jnp.where(g == mx, blk_id, nb), axis=0, keepdims=True)
                sel.append(jnp.where(mx > -jnp.inf, idx, -1))
                g = jnp.where(blk_id == idx, -jnp.inf, g)
            heads.append((qt, sel))
        return heads, own

    def scores(tile, j, slot, causal=False, right_half=False):
        heads, _ = tile
        kt = jnp.concatenate([k_ref[0, j * tk:(j + 1) * tk, :], kbias],
                             axis=1)
        cms = []
        for hh in range(2):
            qt = heads[hh][0]
            s = jnp.dot(kt, qt[:, tk:] if right_half else qt,
                        preferred_element_type=F32)
            if causal:
                s = jnp.where(visible_half if right_half else visible, s, -jnp.inf)
            if right_half:
                slot[hh, :, :tk] = s
            else:
                slot[hh] = s
            cms.append(jnp.max(s, axis=0, keepdims=True))
        return tuple(cms)

    def start_right_half(i, j, slot, cms):
        vt = vt_ref[0, :, j * tk:(j + 1) * tk]
        ps = [jnp.exp2(slot[hh, :, :tk] - cms[hh]) for hh in range(2)]
        new = []
        for hh in range(2):
            acc = jnp.dot(vt[hh * HEAD_DIM:(hh + 1) * HEAD_DIM, :], ps[hh].astype(BF16),
                          preferred_element_type=F32)
            off = slopes[hh] * float((2 * i - j) * tk)
            seen = (cms[hh] - off, jnp.sum(ps[hh], axis=0, keepdims=True), acc)
            new.append(tuple(jnp.concatenate([e, s], axis=1) for e, s in zip(empty_half, seen)))
        return tuple(new)

    def consume(i, tile, j, slot, cms, carry):
        heads, own = tile
        vt = vt_ref[0, :, j * tk:(j + 1) * tk]
        ps, stats = [], []
        for hh in range(2):
            sel = heads[hh][1]
            m, l, _ = carry[hh]
            seen = (sel[0] == j) | (sel[1] == j) | (sel[2] == j) | (own == j)
            off = slopes[hh] * float((2 * i - j) * tk)
            m_new = jnp.where(seen, jnp.maximum(m, cms[hh] - off), m)
            alpha = jnp.exp2(m - m_new)
            p = jnp.exp2(slot[hh] - jnp.where(seen, m_new + off, jnp.inf))
            stats.append((m_new, alpha * l + jnp.sum(p, axis=0, keepdims=True), alpha))
            ps.append(p.astype(BF16))
        new = []
        for hh in range(2):
            m_new, l, alpha = stats[hh]
            acc = alpha * carry[hh][2] + jnp.dot(vt[hh * HEAD_DIM:(hh + 1) * HEAD_DIM, :], ps[hh],
                                                 preferred_element_type=F32)
            new.append((m_new, l, acc))
        return tuple(new)

    def finish(i, tile, last, slot, cms, carry):
        carry = consume(i, tile, last, slot, cms, carry)
        o_t = jnp.concatenate([acc / l for (_, l, acc) in carry], axis=0)
        o_ref[i * tq:(i + 1) * tq, :] = o_t.T.astype(BF16)

    pending = None
    for i in range(SEQ // tq):
        s_a, s_b = bufs[2 * (i % 2)], bufs[2 * (i % 2) + 1]
        tile = prepare(i)
        cms_a = scores(tile, 2 * i + 1, s_a, causal=True, right_half=True)
        cms_b = scores(tile, 2 * i, s_b, causal=True)
        if pending is not None:
            finish(*pending)
        carry = start_right_half(i, 2 * i + 1, s_a, cms_a)

        for n in range(i):
            cms_a = scores(tile, 2 * n, s_a)
            carry = consume(i, tile, 2 * i if n == 0 else 2 * n - 1, s_b, cms_b, carry)
            cms_b = scores(tile, 2 * n + 1, s_b)
            carry = consume(i, tile, 2 * n, s_a, cms_a, carry)
        pending = (i, tile, 2 * i if i == 0 else 2 * i - 1, s_b, cms_b, carry)
    finish(*pending)


def _moba_attention(qt3, k, vt3, km, slopes, kbias):
    t = k.shape[0]
    nb = SEQ // MOBA_BLOCK
    dw = MOBA_HEADS * HEAD_DIM
    k3 = k.reshape(BATCH, SEQ, dw)
    km3 = km.reshape(BATCH, nb, dw)
    tspec = pl.BlockSpec((1, LANES, SEQ), lambda b, c: (b, c, 0))
    return pl.pallas_call(
        _moba_kernel,
        grid=(BATCH, MOBA_HEADS // 2),
        in_specs=[tspec,
                  pl.BlockSpec((1, SEQ, LANES), lambda b, c: (b, 0, c)),
                  tspec,
                  pl.BlockSpec((1, nb, LANES), lambda b, c: (b, 0, c)),
                  pl.BlockSpec((1, 1, LANES), lambda b, c: (c, 0, 0)),
                  pl.BlockSpec((1, MOBA_BLOCK, LANES), lambda b, c: (c, 0, 0))],
        out_specs=pl.BlockSpec((SEQ, LANES), lambda b, c: (b, c)),
        out_shape=jax.ShapeDtypeStruct((t, dw), BF16),
        scratch_shapes=[pltpu.VMEM((2, MOBA_BLOCK, ATT_TILE_Q), F32)] * 4,
        compiler_params=_params(("parallel", "parallel")),
        name="moba_attention",
    )(qt3, k3, vt3, km3, slopes, kbias)


def _group_matrix(sizes, width):
    g = np.zeros((width, width), np.float32)
    lo = 0
    for n in sizes:
        g[lo:lo + n, lo:lo + n] = 1.0 / n
        lo += n
    return jnp.asarray(g, BF16)


def _alibi_slopes_log2(n_heads):
    slopes = 2.0 ** (-8.0 * np.arange(1, n_heads + 1, dtype=np.float32) / n_heads)
    return slopes.astype(np.float64) * LOG2E


def _pair_slopes(n_heads):
    tab = np.repeat(_alibi_slopes_log2(n_heads).astype(np.float32), HEAD_DIM)
    return jnp.asarray(tab.reshape(n_heads // 2, 1, LANES))


def _pair_key_bias(n_heads, block):
    bias = _alibi_slopes_log2(n_heads)[:, None] * np.arange(block, dtype=np.float64)[None, :]
    tab = np.zeros((n_heads // 2, block, LANES), np.float32)
    for part in range(KB_PARTS):
        term = bias.astype(jnp.bfloat16).astype(np.float64)
        bias = bias - term
        for h in range(n_heads):
            tab[h // 2, :, KB_PARTS * (h % 2) + part] = term[h]
    return jnp.asarray(tab, BF16)


def _even_weights(w_in, w_uq, w_ukv, qn_nope, qn_rope, kn_nope, kn_rope, dil_qn, dil_kn):
    o1 = Q_LORA
    o2 = o1 + KV_LORA
    o3 = o2 + MLA_ROPE
    kr = w_in[:, o2:o3]
    pad64 = jnp.zeros((D_MODEL, LANES - 2 * MLA_ROPE), F32)
    win = jnp.concatenate([w_in[:, :o2], kr, kr, pad64, w_in[:, o3:]], axis=1).astype(BF16)

    qd = MLA_NOPE + MLA_ROPE
    a_scale = qd ** -0.5
    zq = jnp.zeros((Q_LORA, LANES - 2 * MLA_ROPE), F32)
    zg = jnp.zeros((LANES - 2 * MLA_ROPE,), F32)
    cols, gains = [], []
    for c in range(MLA_HEADS // 2):
        h0, h1 = 2 * c, 2 * c + 1
        cols += [w_uq[:, h0 * qd:h0 * qd + MLA_NOPE], w_uq[:, h1 * qd:h1 * qd + MLA_NOPE],
                 w_uq[:, h0 * qd + MLA_NOPE:(h0 + 1) * qd], w_uq[:, h1 * qd + MLA_NOPE:(h1 + 1) * qd], zq]
        gains += [qn_nope, qn_nope, qn_rope, qn_rope, zg]
    wuq = jnp.concatenate(cols, axis=1).astype(BF16)
    gq = (jnp.concatenate(gains) * (a_scale * LOG2E))[None, :]

    kvd = MLA_NOPE + MLA_V
    kcols = [w_ukv[:, h * kvd:h * kvd + MLA_NOPE] for h in range(MLA_HEADS)]
    vcols = [w_ukv[:, h * kvd + MLA_NOPE:(h + 1) * kvd] for h in range(MLA_HEADS)]
    wukv = jnp.concatenate(kcols + vcols, axis=1).astype(BF16)
    gkn = jnp.tile(kn_nope, MLA_HEADS)[None, :]
    gkr = jnp.concatenate([kn_rope, kn_rope, zg])[None, :]
    gqb = (jnp.tile(dil_qn, DIL_HEADS) * (HEAD_DIM ** -0.5 * LOG2E))[None, :]
    gkb = jnp.tile(dil_kn, DIL_HEADS)[None, :]
    return win, wuq, wukv, gq, gkn, gkr, gqb, gkb


def kernel(x, p, positions, e_w_in, e_cq_norm, e_ckv_norm, e_w_uq, e_w_ukv, e_qn_nope, e_qn_rope, e_kn_nope, e_kn_rope, e_dil_qn, e_dil_kn, e_w_out, o_w_in, o_qn, o_kn, o_w_out, mix_norm, ff_norm, w_ff1, w_ff2, ple_norm, w_ple_gate, w_ple_proj):
    t = BATCH * SEQ
    h = x.reshape(t, D_MODEL)
    pos2d = positions.reshape(t, 1)
    p2d = p.reshape(DEPTH, t, PLE_DIM)

    gmq = _group_matrix((MLA_NOPE, MLA_NOPE, MLA_ROPE, MLA_ROPE, LANES - 2 * MLA_ROPE), MXU_DIM)
    gm64 = _group_matrix((HEAD_DIM,) * (MXU_DIM // HEAD_DIM), MXU_DIM)
    gm32 = _group_matrix((MLA_ROPE,) * (LANES // MLA_ROPE), LANES)
    half = MLA_ROPE // 2
    invf = ROPE_THETA ** (-(jnp.arange(LANES) % half).astype(F32) / half)
    invf = invf[None, :]

    for i in range(DEPTH):
        j = i // 2
        gmix = mix_norm[i][None, :]
        if i % 2 == 0:
            win, wuq, wukv, gq, gkn, gkr, gqb, gkb = _even_weights(
                e_w_in[j], e_w_uq[j], e_w_ukv[j], e_qn_nope[j], e_qn_rope[j], e_kn_nope[j],
                e_kn_rope[j], e_dil_qn[j], e_dil_kn[j])
            qa, ka, vat, *qkv_b = _even_proj(
                h, pos2d, gmix, win, e_cq_norm[j][None, :], e_ckv_norm[j][None, :], wuq, wukv,
                gq, gkn, gkr, gqb, gkb, gmq, gm64, gm32, invf)
            oa = _mla_attention(qa, ka, vat)
            branches = [_dilated_branch(*qkv_b[3 * n:3 * n + 3], _dil_bias_table(d, w), d)
                        for n, (w, d) in enumerate(DIL_PATTERNS)]
            h = _even_out(h, oa, branches, e_w_out[j].astype(BF16))
            mix = {}
        else:
            gq = (jnp.tile(o_qn[j], MOBA_HEADS) * (HEAD_DIM ** -0.5 * LOG2E))[None, :]
            gk = jnp.tile(o_kn[j], MOBA_HEADS)[None, :]
            q, k, vt, km = _odd_proj(h, gmix, o_w_in[j].astype(BF16), gq, gk, gm64)
            o = _moba_attention(q, k, vt, km, _pair_slopes(MOBA_HEADS),
                                _pair_key_bias(MOBA_HEADS, MOBA_BLOCK))
            mix = {"mix": o, "w_out": o_w_out[j].astype(BF16)}
        h = _tail(h, p2d[i], ff_norm[i][None, :], w_ff1[i].astype(BF16), w_ff2[i].astype(BF16),
                  ple_norm[i][None, :], w_ple_gate[i].astype(BF16), w_ple_proj[i].astype(BF16),
                  **mix)
    return h.reshape(BATCH, SEQ, D_MODEL)
```

```python
import functools

import numpy as np
import jax
import jax.numpy as jnp
from jax import lax
from jax.experimental import pallas as pl
from jax.experimental.pallas import tpu as pltpu

D_MODEL = 1024
BATCH = 8
SEQ = 4096
DEPTH = 2
HEAD_DIM = 64
EPS = 1e-6
MLA_HEADS = 8
MLA_NOPE = 64
MLA_ROPE = 32
MLA_V = 64
Q_LORA = 384
KV_LORA = 256
ROPE_THETA = 10000.0
DIL_HEADS = 8
DIL_PATTERNS = ((128, 1), (512, 4), (2048, 16))
DIL_BLOCK = 128
MOBA_HEADS = 16
MOBA_BLOCK = 256
MOBA_TOPK = 3
D_FF = 4 * D_MODEL
PLE_DIM = 256

LANES = 128
MXU_DIM = 256
VMEM_LIMIT = 56 * 1024 * 1024
ROW_TILE = 512
ATT_TILE = 256
ATT_TILE_Q = 512
MAX_FLOOR = -1e30
KB_PARTS = 3
KM_PARTS = 3
LOG2E = float(np.log2(np.e))
LN2 = float(np.log(2.0))

F32 = jnp.float32
BF16 = jnp.bfloat16


def _const_spec(shape, single=False):
    nd = len(shape)
    mode = {"pipeline_mode": pl.Buffered(1)} if single else {}
    return pl.BlockSpec(shape, lambda *_: (0,) * nd, **mode)


def _params(sem):
    return pltpu.CompilerParams(dimension_semantics=sem, vmem_limit_bytes=VMEM_LIMIT)


def _rms(x, g):
    return x * lax.rsqrt(jnp.mean(x * x, axis=-1, keepdims=True) + EPS) * g


def _group_rms(x, gmat):
    ms = jnp.dot((x * x).astype(BF16), gmat, preferred_element_type=F32)
    return x * lax.rsqrt(ms + EPS)


def _even_proj_kernel(x_ref, pos_ref, gmix_ref, win_ref, gcq_ref, gckv_ref, wuq_ref, wukv_ref,
                      gq_ref, gkn_ref, gkr_ref, gqb_ref, gkb_ref, gmq_ref, gm64_ref, gm32_ref,
                      invf_ref, qat_ref, ka_ref, vat_ref, *rest):
    dil_refs, stage_ref = rest[:-1], rest[-1]
    hn = _rms(x_ref[...], gmix_ref[...]).astype(BF16)
    z = jnp.dot(hn, win_ref[...], preferred_element_type=F32)
    o1 = Q_LORA
    o2 = o1 + KV_LORA
    o3 = o2 + LANES
    dw = DIL_HEADS * HEAD_DIM
    c_q = _rms(z[:, :o1], gcq_ref[...]).astype(BF16)
    c_kv = _rms(z[:, o1:o2], gckv_ref[...]).astype(BF16)

    ang = pos_ref[...].astype(F32) * invf_ref[...]
    cos = jnp.cos(ang)
    sin = jnp.sin(ang)
    lane = lax.broadcasted_iota(jnp.int32, (1, LANES), 1)
    first_half = (lane % MLA_ROPE) < (MLA_ROPE // 2)

    def rope(xr):
        rot = jnp.where(first_half, -pltpu.roll(xr, LANES - MLA_ROPE // 2, 1),
                        pltpu.roll(xr, MLA_ROPE // 2, 1))
        return xr * cos + rot * sin

    q = jnp.dot(c_q, wuq_ref[...], preferred_element_type=F32)
    gq = gq_ref[...]
    for c in range(MLA_HEADS // 2):
        lo = c * MXU_DIM
        qc = _group_rms(q[:, lo:lo + MXU_DIM], gmq_ref[...]) * gq[:, lo:lo + MXU_DIM]
        qat_ref[0, lo:lo + LANES, :] = qc[:, :LANES].T.astype(BF16)
        qat_ref[0, lo + LANES:lo + MXU_DIM, :] = rope(qc[:, LANES:]).T.astype(BF16)

    kv = jnp.dot(c_kv, wukv_ref[...], preferred_element_type=F32)
    k_rope = rope(_group_rms(z[:, o2:o3], gm32_ref[...]) * gkr_ref[...]).astype(BF16)
    gkn = gkn_ref[...]
    for c in range(MLA_HEADS // 2):
        lo = c * LANES
        if c % 2 == 0:
            kn2 = _group_rms(kv[:, lo:lo + MXU_DIM], gm64_ref[...]) * gkn[:, lo:lo + MXU_DIM]
        kn = kn2[:, (c % 2) * LANES:(c % 2 + 1) * LANES]
        ka_ref[:, c * MXU_DIM:c * MXU_DIM + LANES] = kn.astype(BF16)
        ka_ref[:, c * MXU_DIM + LANES:(c + 1) * MXU_DIM] = k_rope
    vat_ref[0] = kv[:, dw:].T.astype(BF16)

    gqb = gqb_ref[...]
    gkb = gkb_ref[...]
    qkv = []
    for c in range(dw // MXU_DIM):
        lo = c * MXU_DIM
        qkv.append((
            _group_rms(z[:, o3 + lo:o3 + lo + MXU_DIM], gm64_ref[...]) * gqb[:, lo:lo + MXU_DIM],
            _group_rms(z[:, o3 + dw + lo:o3 + dw + lo + MXU_DIM], gm64_ref[...])
            * gkb[:, lo:lo + MXU_DIM],
            z[:, o3 + 2 * dw + lo:o3 + 2 * dw + lo + MXU_DIM]))
    tm = x_ref.shape[0]
    for a in range(3):
        for c in range(dw // LANES):
            half = (c % 2) * LANES
            stage_ref[a, c] = qkv[c // 2][a][:, half:half + LANES]
            for (_, dil), out_ref in zip(DIL_PATTERNS, dil_refs[a::3]):
                for r in range(dil):
                    out_ref[0, r, :, c * LANES:(c + 1) * LANES] = stage_ref[
                        a, c, pl.ds(r, tm // dil, stride=dil), :].astype(BF16)


def _even_proj(h2d, pos2d, gmix, win, gcq, gckv, wuq, wukv, gq, gkn, gkr, gqb, gkb,
               gmq, gm64, gm32, invf):
    t = h2d.shape[0]
    tm = ROW_TILE
    consts = (gmix, win, gcq, gckv, wuq, wukv, gq, gkn, gkr, gqb, gkb, gmq, gm64, gm32, invf)
    tpb = SEQ // tm
    dw = DIL_HEADS * HEAD_DIM
    out_specs = [pl.BlockSpec((1, 1024, tm), lambda i: (i // tpb, 0, i % tpb)),
                 pl.BlockSpec((tm, 1024), lambda i: (i, 0)),
                 pl.BlockSpec((1, MLA_HEADS * MLA_V, tm), lambda i: (i // tpb, 0, i % tpb))]
    out_shape = [jax.ShapeDtypeStruct((BATCH, 1024, SEQ), BF16),
                 jax.ShapeDtypeStruct((t, 1024), BF16),
                 jax.ShapeDtypeStruct((BATCH, MLA_HEADS * MLA_V, SEQ), BF16)]
    for (_, dil) in DIL_PATTERNS:
        for _ in range(3):
            out_specs.append(pl.BlockSpec((1, dil, tm // dil, dw),
                                          lambda i: (i // tpb, 0, i % tpb, 0)))
            out_shape.append(jax.ShapeDtypeStruct((BATCH, dil, SEQ // dil, dw), BF16))
    return pl.pallas_call(
        _even_proj_kernel,
        grid=(t // tm,),
        in_specs=[pl.BlockSpec((tm, D_MODEL), lambda i: (i, 0)),
                  pl.BlockSpec((tm, 1), lambda i: (i, 0))] + [_const_spec(c.shape) for c in consts],
        out_specs=out_specs,
        out_shape=out_shape,
        scratch_shapes=[pltpu.VMEM((3, dw // LANES, tm, LANES), F32)],
        compiler_params=_params(("parallel",)),
        name="even_proj",
    )(h2d, pos2d, *consts)


def _mla_kernel(qt_ref, k_ref, vt_ref, o_ref, *bufs):
    tq, tk = ATT_TILE_Q, ATT_TILE
    feat = lax.broadcasted_iota(jnp.int32, (MXU_DIM, 1), 0)
    visible = (lax.broadcasted_iota(jnp.int32, (tk, tq), 0)
               <= lax.broadcasted_iota(jnp.int32, (tk, tq), 1))
    visible_half = (lax.broadcasted_iota(jnp.int32, (tk, tk), 0)
                    <= lax.broadcasted_iota(jnp.int32, (tk, tk), 1))
    head_rows = []
    for hh in range(2):
        nope = (feat >= MLA_NOPE * hh) & (feat < MLA_NOPE * (hh + 1))
        rope = (feat >= LANES + MLA_ROPE * hh) & (feat < LANES + MLA_ROPE * (hh + 1))
        head_rows.append(nope | rope)
    empty_half = (jnp.full((1, tk), -jnp.inf, F32), jnp.zeros((1, tk), F32),
                  jnp.zeros((MLA_V, tk), F32))

    def consume(j, nk, slot, cms, carry):
        vt = vt_ref[0, :, j * tk:(j + nk) * tk]
        ps, stats = [], []
        for hh in range(2):
            m, l, _ = carry[hh]
            m_new = jnp.maximum(m, cms[hh])
            alpha = jnp.exp2(m - m_new)
            p = jnp.exp2(slot[hh, :nk * tk] - m_new)
            stats.append((m_new, alpha * l + jnp.sum(p, axis=0, keepdims=True), alpha))
            ps.append(p.astype(BF16))
        new = []
        for hh in range(2):
            m_new, l, alpha = stats[hh]
            acc = alpha * carry[hh][2] + jnp.dot(vt[hh * MLA_V:(hh + 1) * MLA_V, :], ps[hh],
                                                 preferred_element_type=F32)
            new.append((m_new, l, acc))
        return tuple(new)

    def start_right_half(j, slot, cms):
        vt = vt_ref[0, :, j * tk:(j + 1) * tk]
        ps = [jnp.exp2(slot[hh, :tk, :tk] - cms[hh]) for hh in range(2)]
        new = []
        for hh in range(2):
            acc = jnp.dot(vt[hh * MLA_V:(hh + 1) * MLA_V, :], ps[hh].astype(BF16),
                          preferred_element_type=F32)
            seen = (cms[hh], jnp.sum(ps[hh], axis=0, keepdims=True), acc)
            new.append(tuple(jnp.concatenate([e, s], axis=1) for e, s in zip(empty_half, seen)))
        return tuple(new)

    def finish(i, last, carry):
        carry = consume(*last, carry)
        o_t = jnp.concatenate([acc / l for (_, l, acc) in carry], axis=0)
        o_ref[i * tq:(i + 1) * tq, :] = o_t.T.astype(BF16)

    pending = None
    for i in range(SEQ // tq):
        s_a, s_b = bufs[2 * (i % 2)], bufs[2 * (i % 2) + 1]
        qt = qt_ref[0, :, i * tq:(i + 1) * tq]
        qts = [jnp.where(rows, qt, jnp.zeros_like(qt)) for rows in head_rows]

        def scores(j, nk, slot, qts=qts, causal=False, right_half=False):
            kt = k_ref[0, j * tk:(j + nk) * tk, :]
            cms = []
            for hh in range(2):
                s = jnp.dot(kt, qts[hh][:, tk:] if right_half else qts[hh],
                            preferred_element_type=F32)
                if causal:
                    s = jnp.where(visible_half if right_half else visible, s, -jnp.inf)
                if right_half:
                    slot[hh, :tk, :tk] = s
                else:
                    slot[hh, :nk * tk] = s
                cms.append(jnp.max(s, axis=0, keepdims=True))
            return tuple(cms)

        cms_a = scores(2 * i + 1, 1, s_a, causal=True, right_half=True)
        last = (2 * i, 1, s_b, scores(2 * i, 1, s_b, causal=True))
        if pending is not None:
            finish(*pending)
        carry = start_right_half(2 * i + 1, s_a, cms_a)

        for n in range(i):
            slot = (s_a, s_b)[n % 2]
            cms = scores(2 * n, 2, slot)
            carry = consume(*last, carry)
            last = (2 * n, 2, slot, cms)
        pending = (i, last, carry)
    finish(*pending)


def _mla_attention(qat, ka, vat):
    t = ka.shape[0]
    ka3 = ka.reshape(BATCH, SEQ, ka.shape[1])
    return pl.pallas_call(
        _mla_kernel,
        grid=(BATCH, MLA_HEADS // 2),
        in_specs=[pl.BlockSpec((1, MXU_DIM, SEQ), lambda b, c: (b, c, 0)),
                  pl.BlockSpec((1, SEQ, MXU_DIM), lambda b, c: (b, 0, c)),
                  pl.BlockSpec((1, LANES, SEQ), lambda b, c: (b, c, 0))],
        out_specs=pl.BlockSpec((SEQ, LANES), lambda b, c: (b, c)),
        out_shape=jax.ShapeDtypeStruct((t, MLA_HEADS * MLA_V), BF16),
        scratch_shapes=[pltpu.VMEM((2, 2 * ATT_TILE, ATT_TILE_Q), F32)] * 4,
        compiler_params=_params(("parallel", "parallel")),
        name="mla_attention",
    )(qat, ka3, vat)


DIL_QBLK = 2 * DIL_BLOCK
DIL_HALO = DIL_BLOCK
DIL_GROUP = 8


def _dil_kernel(q_ref, k_ref, v_ref, mb_ref, o_ref, lse_ref, vt_scr, *, nblk):
    qb, halo = DIL_QBLK, DIL_HALO
    ncls = q_ref.shape[1]
    lane = lax.broadcasted_iota(jnp.int32, (1, LANES), 1)
    low = lane < HEAD_DIM
    for g in range(ncls):
        vt_scr[g] = v_ref[0, g].astype(F32).T.astype(BF16)

    def blocks(items):
        work = []
        for g, r0, first in items:
            q = q_ref[0, g, pl.ds(r0, qb), :].astype(F32)
            if first:
                k0, nk = 0, qb
            else:
                k0 = r0 - halo if isinstance(r0, int) else pl.multiple_of(r0 - halo, halo)
                nk = qb + halo
            k = k_ref[0, g, pl.ds(k0, nk), :]
            for hh in range(2):
                qt = jnp.where(low if hh == 0 else ~low, q, 0.0).T.astype(BF16)
                s = jnp.dot(k, qt, preferred_element_type=F32) - (
                    mb_ref[0, hh, halo:, :] if first else mb_ref[0, hh])
                work.append((g, r0, k0, nk, hh, s))
        outs = []
        for g, r0, k0, nk, hh, s in work:
            m = jnp.max(s, axis=0, keepdims=True)
            p = jnp.exp2(s - m)
            den = jnp.sum(p, axis=0, keepdims=True)
            vt = vt_scr[g, hh * HEAD_DIM:(hh + 1) * HEAD_DIM, pl.ds(k0, nk)]
            o = jnp.dot(vt, p.astype(BF16), preferred_element_type=F32) / den
            lse = (m + jnp.log2(den)) * LN2
            outs.append((o, jnp.broadcast_to(lse, (HEAD_DIM, qb))))
        for n, (g, r0, _) in enumerate(items):
            pair = outs[2 * n:2 * n + 2]
            o_ref[0, g, pl.ds(r0, qb), :] = jnp.concatenate(
                [o for o, _ in pair], axis=0).T.astype(o_ref.dtype)
            lse_ref[0, g, pl.ds(r0, qb), :] = jnp.concatenate([l for _, l in pair], axis=0).T

    per = DIL_GROUP // ncls
    blocks([(g, n * qb, n == 0) for g in range(ncls) for n in range(per)])

    def body(n, carry):
        r0 = pl.multiple_of(per * n * qb, per * qb)
        blocks([(g, r0 + c * qb if c == 0 else pl.multiple_of(r0 + c * qb, qb), False)
                for g in range(ncls) for c in range(per)])
        return carry

    lax.fori_loop(1, nblk // per, body, 0)


def _dilated_branch(qc, kc, vc, mb, dil):
    l = SEQ // dil
    dw = DIL_HEADS * HEAD_DIM
    nblk = l // DIL_QBLK
    ncls = max(1, DIL_GROUP // nblk)
    assert dil % ncls == 0 and nblk % (DIL_GROUP // ncls) == 0
    blk = pl.BlockSpec((1, ncls, l, LANES), lambda c, b, r: (b, r, 0, c))
    return pl.pallas_call(
        functools.partial(_dil_kernel, nblk=nblk),
        grid=(DIL_HEADS // 2, BATCH, dil // ncls),
        in_specs=[blk, blk, blk,
                  pl.BlockSpec((1, 2, DIL_QBLK + DIL_HALO, DIL_QBLK), lambda c, b, r: (c, 0, 0, 0))],
        out_specs=[blk, blk],
        out_shape=[jax.ShapeDtypeStruct((BATCH, dil, l, dw), BF16),
                   jax.ShapeDtypeStruct((BATCH, dil, l, dw), F32)],
        scratch_shapes=[pltpu.VMEM((ncls, LANES, l), BF16)],
        compiler_params=_params(("parallel", "parallel", "parallel")),
        name=f"dilated_d{dil}",
    )(qc, kc, vc, mb)


def _dil_bias_table(dil, window):
    reach = window // dil
    ki = np.arange(DIL_QBLK + DIL_HALO)[:, None]
    qi = np.arange(DIL_QBLK)[None, :]
    step = DIL_HALO + qi - ki
    valid = (step >= 0) & (step <= reach)
    slopes = 2.0 ** (-8.0 * np.arange(1, DIL_HEADS + 1, dtype=np.float32) / DIL_HEADS)
    bias = slopes.astype(np.float32)[:, None, None] * (step * dil).astype(np.float32)[None]
    tab = np.where(valid[None], bias * LOG2E, np.inf).astype(np.float32)
    return jnp.asarray(tab.reshape(DIL_HEADS // 2, 2, DIL_QBLK + DIL_HALO, DIL_QBLK))


def _even_out_kernel(h_ref, oa_ref, *rest):
    branch_refs, w_ref, out_ref, stage_ref = rest[:-3], rest[-3], rest[-2], rest[-1]
    tm = h_ref.shape[0]
    vals = []
    for n, ((_, dil), ref) in enumerate(zip([pt for pt in DIL_PATTERNS for _ in range(2)],
                                            branch_refs)):
        if dil == 1:
            vals.append(ref[0, 0].astype(F32))
        else:
            slot = n - 2
            for r in range(dil):
                for c in range(ref.shape[-1] // LANES):
                    stage_ref[slot, c, pl.ds(r, tm // dil, stride=dil), :] = ref[
                        0, r, :, c * LANES:(c + 1) * LANES].astype(F32)
            vals.append(jnp.concatenate(
                [stage_ref[slot, c] for c in range(ref.shape[-1] // LANES)], axis=1))
    o1, l1, o2, l2, o3, l3 = vals
    lm = jnp.maximum(jnp.maximum(l1, l2), l3)
    w1, w2, w3 = jnp.exp(l1 - lm), jnp.exp(l2 - lm), jnp.exp(l3 - lm)
    mixed_b = (w1 * o1 + w2 * o2 + w3 * o3) / (w1 + w2 + w3)
    half = MLA_HEADS * MLA_V
    acc = jnp.dot(oa_ref[...], w_ref[:half, :], preferred_element_type=F32)
    acc += jnp.dot(mixed_b.astype(BF16), w_ref[half:, :], preferred_element_type=F32)
    out_ref[...] = h_ref[...] + acc


def _even_out(h2d, oa, branches, w_out):
    t = h2d.shape[0]
    tm = ROW_TILE
    tpb = SEQ // tm
    dw = DIL_HEADS * HEAD_DIM
    full = pl.BlockSpec((tm, D_MODEL), lambda i: (i, 0))
    specs, args = [], []
    for (_, dil), pair in zip(DIL_PATTERNS, branches):
        for a in pair:
            specs.append(pl.BlockSpec((1, dil, tm // dil, dw), lambda i: (i // tpb, 0, i % tpb, 0)))
            args.append(a)
    return pl.pallas_call(
        _even_out_kernel,
        grid=(t // tm,),
        in_specs=[full, pl.BlockSpec((tm, dw), lambda i: (i, 0))] + specs
        + [_const_spec(w_out.shape)],
        out_specs=full,
        out_shape=jax.ShapeDtypeStruct((t, D_MODEL), F32),
        scratch_shapes=[pltpu.VMEM((4, dw // LANES, tm, LANES), F32)],
        compiler_params=_params(("parallel",)),
        name="even_out_proj",
    )(h2d, oa, *args, w_out)


def _tail_kernel(*refs, with_out_proj):
    if with_out_proj:
        h_ref, o_ref, wout_ref, *refs = refs
    else:
        h_ref, *refs = refs
    p_ref, gff_ref, w1_ref, w2_ref, gple_ref, wg_ref, wp_ref, out_ref = refs
    h = h_ref[...]
    if with_out_proj:
        h = h + jnp.dot(o_ref[...], wout_ref[...], preferred_element_type=F32)
    n = _rms(h, gff_ref[...]).astype(BF16)
    chunk = D_MODEL
    for c in range(D_FF // chunk):
        u = jnp.maximum(jnp.dot(n, w1_ref[:, c * chunk:(c + 1) * chunk],
                                preferred_element_type=F32), 0.0)
        h = h + jnp.dot((u * u).astype(BF16), w2_ref[c * chunk:(c + 1) * chunk, :],
                        preferred_element_type=F32)
    n = _rms(h, gple_ref[...]).astype(BF16)
    gate = jax.nn.sigmoid(jnp.dot(n, wg_ref[...], preferred_element_type=F32))
    proj = jnp.dot(p_ref[...].astype(BF16), wp_ref[...], preferred_element_type=F32)
    out_ref[...] = h + gate * proj


def _tail(h2d, p2d, gff, w1, w2, gple, wg, wp, mix=None, w_out=None):
    t = h2d.shape[0]
    tm = ROW_TILE
    full = pl.BlockSpec((tm, D_MODEL), lambda i: (i, 0))
    with_out = mix is not None
    head_specs = [full, full, _const_spec(w_out.shape, single=True)] if with_out else [full]
    head_args = (h2d, mix, w_out) if with_out else (h2d,)
    consts = (gff, w1, w2, gple, wg, wp)
    return pl.pallas_call(
        functools.partial(_tail_kernel, with_out_proj=with_out),
        grid=(t // tm,),
        in_specs=head_specs + [pl.BlockSpec((tm, PLE_DIM), lambda i: (i, 0))]
        + [_const_spec(c.shape, single=True) for c in consts],
        out_specs=full,
        out_shape=jax.ShapeDtypeStruct((t, D_MODEL), F32),
        compiler_params=_params(("parallel",)),
        name="tail_with_out_proj" if with_out else "tail",
    )(*head_args, p2d, *consts)


def _odd_proj_kernel(x_ref, gmix_ref, win_ref, gq_ref, gk_ref, gm64_ref,
                     qt_ref, k_ref, vt_ref, km_ref):
    hn = _rms(x_ref[...], gmix_ref[...]).astype(BF16)
    z = jnp.dot(hn, win_ref[...], preferred_element_type=F32)
    dw = MOBA_HEADS * HEAD_DIM
    gq = gq_ref[...]
    gk = gk_ref[...]
    tm = x_ref.shape[0]
    for c in range(dw // MXU_DIM):
        lo = c * MXU_DIM
        qt_ref[0, lo:lo + MXU_DIM, :] = (
            _group_rms(z[:, lo:lo + MXU_DIM], gm64_ref[...]) * gq[:, lo:lo + MXU_DIM]
        ).T.astype(BF16)
        k = _group_rms(z[:, dw + lo:dw + lo + MXU_DIM], gm64_ref[...]) * gk[:, lo:lo + MXU_DIM]
        k_ref[:, lo:lo + MXU_DIM] = k.astype(BF16)
        for r in range(tm // MOBA_BLOCK):
            km_ref[r, :, lo:lo + MXU_DIM] = jnp.mean(
                k[r * MOBA_BLOCK:(r + 1) * MOBA_BLOCK], axis=0, keepdims=True)
    vt_ref[0] = z[:, 2 * dw:].T.astype(BF16)


def _odd_proj(h2d, gmix, win, gq, gk, gm64):
    t = h2d.shape[0]
    tm = ROW_TILE
    dw = MOBA_HEADS * HEAD_DIM
    consts = (gmix, win, gq, gk, gm64)
    full = pl.BlockSpec((tm, dw), lambda i: (i, 0))
    tpb = SEQ // tm
    tspec = pl.BlockSpec((1, dw, tm), lambda i: (i // tpb, 0, i % tpb))
    return pl.pallas_call(
        _odd_proj_kernel,
        grid=(t // tm,),
        in_specs=[pl.BlockSpec((tm, D_MODEL), lambda i: (i, 0))] + [_const_spec(c.shape) for c in consts],
        out_specs=[tspec, full, tspec,
                   pl.BlockSpec((tm // MOBA_BLOCK, 1, dw), lambda i: (i, 0, 0))],
        out_shape=[jax.ShapeDtypeStruct((BATCH, dw, SEQ), BF16), jax.ShapeDtypeStruct((t, dw), BF16),
                   jax.ShapeDtypeStruct((BATCH, dw, SEQ), BF16),
                   jax.ShapeDtypeStruct((t // MOBA_BLOCK, 1, dw), F32)],
        compiler_params=_params(("parallel",)),
        name="odd_proj",
    )(h2d, *consts)


def _moba_kernel(qt_ref, k_ref, vt_ref, km_ref, sl_ref, kb_ref, o_ref, *bufs):
    tq, tk = ATT_TILE_Q, MOBA_BLOCK
    nb = SEQ // MOBA_BLOCK
    visible = (lax.broadcasted_iota(jnp.int32, (tk, tq), 0)
               <= lax.broadcasted_iota(jnp.int32, (tk, tq), 1))
    visible_half = (lax.broadcasted_iota(jnp.int32, (tk, tk), 0)
                    <= lax.broadcasted_iota(jnp.int32, (tk, tk), 1))
    row = lax.broadcasted_iota(jnp.int32, (LANES, tq), 0)
    blk_id = lax.broadcasted_iota(jnp.int32, (nb, tq), 0)
    half = lax.broadcasted_iota(jnp.int32, (1, tq), 1) // tk
    kbias = kb_ref[0]
    slopes = [sl_ref[0, :, hh * HEAD_DIM:hh * HEAD_DIM + 1] for hh in range(2)]
    km_terms, resid = [], km_ref[0]
    for _ in range(KM_PARTS):
        km_terms.append(resid.astype(BF16))
        resid = resid - km_terms[-1].astype(F32)
    km_cat = jnp.concatenate(km_terms, axis=0)
    km_cat = jnp.concatenate([km_cat, jnp.zeros_like(km_cat)], axis=1)
    empty_half = (jnp.full((1, tk), MAX_FLOOR, F32), jnp.zeros((1, tk), F32),
                  jnp.zeros((HEAD_DIM, tk), F32))

    def prepare(i):
        own = 2 * i + half
        q_t = qt_ref[0, :, i * tq:(i + 1) * tq]
        heads = []
        for hh in range(2):
            qt = jnp.where((row >= HEAD_DIM * hh) & (row < HEAD_DIM * (hh + 1)), q_t,
                           jnp.zeros_like(q_t))
            ones = jnp.where((row >= KB_PARTS * hh) & (row < KB_PARTS * (hh + 1)), 1.0, 0.0)
            qt = jnp.concatenate([qt, ones.astype(BF16)], axis=0)
            gate_terms = jnp.dot(km_cat, qt, preferred_element_type=F32)
            gate = sum(gate_terms[n * nb:(n + 1) * nb] for n in range(KM_PARTS))
            g = jnp.where(blk_id < own, gate, -jnp.inf)
            sel = []
            for _ in range(MOBA_TOPK):
                mx = jnp.max(g, axis=0, keepdims=True)
                idx = jnp.min(jnp.where(g == mx, blk_id, nb), axis=0, keepdims=True)
                sel.append(jnp.where(mx > -jnp.inf, idx, -1))
                g = jnp.where(blk_id == idx, -jnp.inf, g)
            heads.append((qt, sel))
        return heads, own

    def scores(tile, j, nk, slot, causal=False, right_half=False):
        heads, _ = tile
        kt = jnp.concatenate([k_ref[0, j * tk:(j + nk) * tk, :],
                              jnp.concatenate([kbias] * nk, axis=0)], axis=1)
        cms = []
        for hh in range(2):
            qt = heads[hh][0]
            s = jnp.dot(kt, qt[:, tk:] if right_half else qt,
                        preferred_element_type=F32)
            if causal:
                s = jnp.where(visible_half if right_half else visible, s, -jnp.inf)
            if right_half:
                slot[hh, :tk, :tk] = s
            else:
                slot[hh, :nk * tk] = s
            cms.append(tuple(jnp.max(s[b * tk:(b + 1) * tk], axis=0, keepdims=True)
                             for b in range(nk)))
        return tuple(cms)

    def start_right_half(i, j, slot, cms):
        vt = vt_ref[0, :, j * tk:(j + 1) * tk]
        ps = [jnp.exp2(slot[hh, :tk, :tk] - cms[hh][0]) for hh in range(2)]
        new = []
        for hh in range(2):
            acc = jnp.dot(vt[hh * HEAD_DIM:(hh + 1) * HEAD_DIM, :], ps[hh].astype(BF16),
                          preferred_element_type=F32)
            off = slopes[hh] * float((2 * i - j) * tk)
            seen = (cms[hh][0] - off, jnp.sum(ps[hh], axis=0, keepdims=True), acc)
            new.append(tuple(jnp.concatenate([e, s], axis=1) for e, s in zip(empty_half, seen)))
        return tuple(new)

    def consume(i, tile, j, nk, slot, cms, carry):
        heads, own = tile
        vt = vt_ref[0, :, j * tk:(j + nk) * tk]
        ps, stats = [], []
        for hh in range(2):
            sel = heads[hh][1]
            m, l, _ = carry[hh]
            blocks, m_new = [], m
            for b in range(nk):
                jb = j + b
                seen = (sel[0] == jb) | (sel[1] == jb) | (sel[2] == jb) | (own == jb)
                off = slopes[hh] * float((2 * i - jb) * tk)
                m_new = jnp.where(seen, jnp.maximum(m_new, cms[hh][b] - off), m_new)
                blocks.append((seen, off))
            alpha = jnp.exp2(m - m_new)
            p = jnp.concatenate(
                [jnp.exp2(slot[hh, b * tk:(b + 1) * tk] - jnp.where(seen, m_new + off, jnp.inf))
                 for b, (seen, off) in enumerate(blocks)], axis=0)
            stats.append((m_new, alpha * l + jnp.sum(p, axis=0, keepdims=True), alpha))
            ps.append(p.astype(BF16))
        new = []
        for hh in range(2):
            m_new, l, alpha = stats[hh]
            acc = alpha * carry[hh][2] + jnp.dot(vt[hh * HEAD_DIM:(hh + 1) * HEAD_DIM, :], ps[hh],
                                                 preferred_element_type=F32)
            new.append((m_new, l, acc))
        return tuple(new)

    def finish(i, tile, last, carry):
        carry = consume(i, tile, *last, carry)
        o_t = jnp.concatenate([acc / l for (_, l, acc) in carry], axis=0)
        o_ref[i * tq:(i + 1) * tq, :] = o_t.T.astype(BF16)

    pending = None
    for i in range(SEQ // tq):
        s_a, s_b = bufs[2 * (i % 2)], bufs[2 * (i % 2) + 1]
        tile = prepare(i)
        cms_a = scores(tile, 2 * i + 1, 1, s_a, causal=True, right_half=True)
        last = (2 * i, 1, s_b, scores(tile, 2 * i, 1, s_b, causal=True))
        if pending is not None:
            finish(*pending)
        carry = start_right_half(i, 2 * i + 1, s_a, cms_a)

        for n in range(i):
            slot = (s_a, s_b)[n % 2]
            cms = scores(tile, 2 * n, 2, slot)
            carry = consume(i, tile, *last, carry)
            last = (2 * n, 2, slot, cms)
        pending = (i, tile, last, carry)
    finish(*pending)


def _moba_attention(qt3, k, vt3, km, slopes, kbias):
    t = k.shape[0]
    nb = SEQ // MOBA_BLOCK
    dw = MOBA_HEADS * HEAD_DIM
    k3 = k.reshape(BATCH, SEQ, dw)
    km3 = km.reshape(BATCH, nb, dw)
    tspec = pl.BlockSpec((1, LANES, SEQ), lambda b, c: (b, c, 0))
    return pl.pallas_call(
        _moba_kernel,
        grid=(BATCH, MOBA_HEADS // 2),
        in_specs=[tspec,
                  pl.BlockSpec((1, SEQ, LANES), lambda b, c: (b, 0, c)),
                  tspec,
                  pl.BlockSpec((1, nb, LANES), lambda b, c: (b, 0, c)),
                  pl.BlockSpec((1, 1, LANES), lambda b, c: (c, 0, 0)),
                  pl.BlockSpec((1, MOBA_BLOCK, LANES), lambda b, c: (c, 0, 0))],
        out_specs=pl.BlockSpec((SEQ, LANES), lambda b, c: (b, c)),
        out_shape=jax.ShapeDtypeStruct((t, dw), BF16),
        scratch_shapes=[pltpu.VMEM((2, 2 * MOBA_BLOCK, ATT_TILE_Q), F32)] * 4,
        compiler_params=_params(("parallel", "parallel")),
        name="moba_attention",
    )(qt3, k3, vt3, km3, slopes, kbias)


def _group_matrix(sizes, width):
    g = np.zeros((width, width), np.float32)
    lo = 0
    for n in sizes:
        g[lo:lo + n, lo:lo + n] = 1.0 / n
        lo += n
    return jnp.asarray(g, BF16)


def _alibi_slopes_log2(n_heads):
    slopes = 2.0 ** (-8.0 * np.arange(1, n_heads + 1, dtype=np.float32) / n_heads)
    return slopes.astype(np.float64) * LOG2E


def _pair_slopes(n_heads):
    tab = np.repeat(_alibi_slopes_log2(n_heads).astype(np.float32), HEAD_DIM)
    return jnp.asarray(tab.reshape(n_heads // 2, 1, LANES))


def _pair_key_bias(n_heads, block):
    bias = _alibi_slopes_log2(n_heads)[:, None] * np.arange(block, dtype=np.float64)[None, :]
    tab = np.zeros((n_heads // 2, block, LANES), np.float32)
    for part in range(KB_PARTS):
        term = bias.astype(jnp.bfloat16).astype(np.float64)
        bias = bias - term
        for h in range(n_heads):
            tab[h // 2, :, KB_PARTS * (h % 2) + part] = term[h]
    return jnp.asarray(tab, BF16)


def _even_weights(w_in, w_uq, w_ukv, qn_nope, qn_rope, kn_nope, kn_rope, dil_qn, dil_kn):
    o1 = Q_LORA
    o2 = o1 + KV_LORA
    o3 = o2 + MLA_ROPE
    kr = w_in[:, o2:o3]
    pad64 = jnp.zeros((D_MODEL, LANES - 2 * MLA_ROPE), F32)
    win = jnp.concatenate([w_in[:, :o2], kr, kr, pad64, w_in[:, o3:]], axis=1).astype(BF16)

    qd = MLA_NOPE + MLA_ROPE
    a_scale = qd ** -0.5
    zq = jnp.zeros((Q_LORA, LANES - 2 * MLA_ROPE), F32)
    zg = jnp.zeros((LANES - 2 * MLA_ROPE,), F32)
    cols, gains = [], []
    for c in range(MLA_HEADS // 2):
        h0, h1 = 2 * c, 2 * c + 1
        cols += [w_uq[:, h0 * qd:h0 * qd + MLA_NOPE], w_uq[:, h1 * qd:h1 * qd + MLA_NOPE],
                 w_uq[:, h0 * qd + MLA_NOPE:(h0 + 1) * qd], w_uq[:, h1 * qd + MLA_NOPE:(h1 + 1) * qd], zq]
        gains += [qn_nope, qn_nope, qn_rope, qn_rope, zg]
    wuq = jnp.concatenate(cols, axis=1).astype(BF16)
    gq = (jnp.concatenate(gains) * (a_scale * LOG2E))[None, :]

    kvd = MLA_NOPE + MLA_V
    kcols = [w_ukv[:, h * kvd:h * kvd + MLA_NOPE] for h in range(MLA_HEADS)]
    vcols = [w_ukv[:, h * kvd + MLA_NOPE:(h + 1) * kvd] for h in range(MLA_HEADS)]
    wukv = jnp.concatenate(kcols + vcols, axis=1).astype(BF16)
    gkn = jnp.tile(kn_nope, MLA_HEADS)[None, :]
    gkr = jnp.concatenate([kn_rope, kn_rope, zg])[None, :]
    gqb = (jnp.tile(dil_qn, DIL_HEADS) * (HEAD_DIM ** -0.5 * LOG2E))[None, :]
    gkb = jnp.tile(dil_kn, DIL_HEADS)[None, :]
    return win, wuq, wukv, gq, gkn, gkr, gqb, gkb


def kernel(x, p, positions, e_w_in, e_cq_norm, e_ckv_norm, e_w_uq, e_w_ukv, e_qn_nope, e_qn_rope, e_kn_nope, e_kn_rope, e_dil_qn, e_dil_kn, e_w_out, o_w_in, o_qn, o_kn, o_w_out, mix_norm, ff_norm, w_ff1, w_ff2, ple_norm, w_ple_gate, w_ple_proj):
    t = BATCH * SEQ
    h = x.reshape(t, D_MODEL)
    pos2d = positions.reshape(t, 1)
    p2d = p.reshape(DEPTH, t, PLE_DIM)

    gmq = _group_matrix((MLA_NOPE, MLA_NOPE, MLA_ROPE, MLA_ROPE, LANES - 2 * MLA_ROPE), MXU_DIM)
    gm64 = _group_matrix((HEAD_DIM,) * (MXU_DIM // HEAD_DIM), MXU_DIM)
    gm32 = _group_matrix((MLA_ROPE,) * (LANES // MLA_ROPE), LANES)
    half = MLA_ROPE // 2
    invf = ROPE_THETA ** (-(jnp.arange(LANES) % half).astype(F32) / half)
    invf = invf[None, :]

    for i in range(DEPTH):
        j = i // 2
        gmix = mix_norm[i][None, :]
        if i % 2 == 0:
            win, wuq, wukv, gq, gkn, gkr, gqb, gkb = _even_weights(
                e_w_in[j], e_w_uq[j], e_w_ukv[j], e_qn_nope[j], e_qn_rope[j], e_kn_nope[j],
                e_kn_rope[j], e_dil_qn[j], e_dil_kn[j])
            qa, ka, vat, *qkv_b = _even_proj(
                h, pos2d, gmix, win, e_cq_norm[j][None, :], e_ckv_norm[j][None, :], wuq, wukv,
                gq, gkn, gkr, gqb, gkb, gmq, gm64, gm32, invf)
            oa = _mla_attention(qa, ka, vat)
            branches = [_dilated_branch(*qkv_b[3 * n:3 * n + 3], _dil_bias_table(d, w), d)
                        for n, (w, d) in enumerate(DIL_PATTERNS)]
            h = _even_out(h, oa, branches, e_w_out[j].astype(BF16))
            mix = {}
        else:
            gq = (jnp.tile(o_qn[j], MOBA_HEADS) * (HEAD_DIM ** -0.5 * LOG2E))[None, :]
            gk = jnp.tile(o_kn[j], MOBA_HEADS)[None, :]
            q, k, vt, km = _odd_proj(h, gmix, o_w_in[j].astype(BF16), gq, gk, gm64)
            o = _moba_attention(q, k, vt, km, _pair_slopes(MOBA_HEADS),
                                _pair_key_bias(MOBA_HEADS, MOBA_BLOCK))
            mix = {"mix": o, "w_out": o_w_out[j].astype(BF16)}
        h = _tail(h, p2d[i], ff_norm[i][None, :], w_ff1[i].astype(BF16), w_ff2[i].astype(BF16),
                  ple_norm[i][None, :], w_ple_gate[i].astype(BF16), w_ple_proj[i].astype(BF16),
                  **mix)
    return h.reshape(BATCH, SEQ, D_MODEL)
```

```python
import functools

import numpy as np
import jax
import jax.numpy as jnp
from jax import lax
from jax.experimental import pallas as pl
from jax.experimental.pallas import tpu as pltpu

D_MODEL = 1024
BATCH = 8
SEQ = 4096
DEPTH = 2
HEAD_DIM = 64
EPS = 1e-6
MLA_HEADS = 8
MLA_NOPE = 64
MLA_ROPE = 32
MLA_V = 64
Q_LORA = 384
KV_LORA = 256
ROPE_THETA = 10000.0
DIL_HEADS = 8
DIL_PATTERNS = ((128, 1), (512, 4), (2048, 16))
DIL_BLOCK = 128
MOBA_HEADS = 16
MOBA_BLOCK = 256
MOBA_TOPK = 3
D_FF = 4 * D_MODEL
PLE_DIM = 256

LANES = 128
MXU_DIM = 256
VMEM_LIMIT = 56 * 1024 * 1024
ROW_TILE = 512
ATT_TILE = 256
ATT_TILE_Q = 512
MAX_FLOOR = -1e30
KB_PARTS = 3
KM_PARTS = 3
LOG2E = float(np.log2(np.e))
LN2 = float(np.log(2.0))

F32 = jnp.float32
BF16 = jnp.bfloat16


def _const_spec(shape, single=False):
    nd = len(shape)
    mode = {"pipeline_mode": pl.Buffered(1)} if single else {}
    return pl.BlockSpec(shape, lambda *_: (0,) * nd, **mode)


def _params(sem):
    return pltpu.CompilerParams(dimension_semantics=sem, vmem_limit_bytes=VMEM_LIMIT)


def _rms(x, g):
    return x * lax.rsqrt(jnp.mean(x * x, axis=-1, keepdims=True) + EPS) * g


def _group_rms(x, gmat):
    ms = jnp.dot((x * x).astype(BF16), gmat, preferred_element_type=F32)
    return x * lax.rsqrt(ms + EPS)


def _even_proj_kernel(x_ref, pos_ref, gmix_ref, win_ref, gcq_ref, gckv_ref, wuq_ref, wukv_ref,
                      gq_ref, gkn_ref, gkr_ref, gqb_ref, gkb_ref, gmq_ref, gm64_ref, gm32_ref,
                      invf_ref, qat_ref, ka_ref, vat_ref, *rest):
    dil_refs, stage_ref = rest[:-1], rest[-1]
    hn = _rms(x_ref[...], gmix_ref[...]).astype(BF16)
    z = jnp.dot(hn, win_ref[...], preferred_element_type=F32)
    o1 = Q_LORA
    o2 = o1 + KV_LORA
    o3 = o2 + LANES
    dw = DIL_HEADS * HEAD_DIM
    c_q = _rms(z[:, :o1], gcq_ref[...]).astype(BF16)
    c_kv = _rms(z[:, o1:o2], gckv_ref[...]).astype(BF16)

    gqb = gqb_ref[...]
    gkb = gkb_ref[...]
    qkv = []
    for c in range(dw // MXU_DIM):
        lo = c * MXU_DIM
        qkv.append((
            _group_rms(z[:, o3 + lo:o3 + lo + MXU_DIM], gm64_ref[...]) * gqb[:, lo:lo + MXU_DIM],
            _group_rms(z[:, o3 + dw + lo:o3 + dw + lo + MXU_DIM], gm64_ref[...])
            * gkb[:, lo:lo + MXU_DIM],
            z[:, o3 + 2 * dw + lo:o3 + 2 * dw + lo + MXU_DIM]))
    tm = x_ref.shape[0]
    for a in range(3):
        for c in range(dw // LANES):
            half = (c % 2) * LANES
            stage_ref[a, c] = qkv[c // 2][a][:, half:half + LANES]
            for (_, dil), out_ref in zip(DIL_PATTERNS, dil_refs[a::3]):
                for r in range(dil):
                    out_ref[0, r, :, c * LANES:(c + 1) * LANES] = stage_ref[
                        a, c, pl.ds(r, tm // dil, stride=dil), :].astype(BF16)

    ang = pos_ref[...].astype(F32) * invf_ref[...]
    cos = jnp.cos(ang)
    sin = jnp.sin(ang)
    lane = lax.broadcasted_iota(jnp.int32, (1, LANES), 1)
    first_half = (lane % MLA_ROPE) < (MLA_ROPE // 2)

    def rope(xr):
        rot = jnp.where(first_half, -pltpu.roll(xr, LANES - MLA_ROPE // 2, 1),
                        pltpu.roll(xr, MLA_ROPE // 2, 1))
        return xr * cos + rot * sin

    q = jnp.dot(c_q, wuq_ref[...], preferred_element_type=F32)
    gq = gq_ref[...]
    for c in range(MLA_HEADS // 2):
        lo = c * MXU_DIM
        qc = _group_rms(q[:, lo:lo + MXU_DIM], gmq_ref[...]) * gq[:, lo:lo + MXU_DIM]
        qat_ref[0, lo:lo + LANES, :] = qc[:, :LANES].T.astype(BF16)
        qat_ref[0, lo + LANES:lo + MXU_DIM, :] = rope(qc[:, LANES:]).T.astype(BF16)

    kv = jnp.dot(c_kv, wukv_ref[...], preferred_element_type=F32)
    k_rope = rope(_group_rms(z[:, o2:o3], gm32_ref[...]) * gkr_ref[...]).astype(BF16)
    gkn = gkn_ref[...]
    for c in range(MLA_HEADS // 2):
        lo = c * LANES
        if c % 2 == 0:
            kn2 = _group_rms(kv[:, lo:lo + MXU_DIM], gm64_ref[...]) * gkn[:, lo:lo + MXU_DIM]
        kn = kn2[:, (c % 2) * LANES:(c % 2 + 1) * LANES]
        ka_ref[:, c * MXU_DIM:c * MXU_DIM + LANES] = kn.astype(BF16)
        ka_ref[:, c * MXU_DIM + LANES:(c + 1) * MXU_DIM] = k_rope
    vat_ref[0] = kv[:, dw:].T.astype(BF16)


def _even_proj(h2d, pos2d, gmix, win, gcq, gckv, wuq, wukv, gq, gkn, gkr, gqb, gkb,
               gmq, gm64, gm32, invf):
    t = h2d.shape[0]
    tm = ROW_TILE
    consts = (gmix, win, gcq, gckv, wuq, wukv, gq, gkn, gkr, gqb, gkb, gmq, gm64, gm32, invf)
    tpb = SEQ // tm
    dw = DIL_HEADS * HEAD_DIM
    out_specs = [pl.BlockSpec((1, 1024, tm), lambda i: (i // tpb, 0, i % tpb)),
                 pl.BlockSpec((tm, 1024), lambda i: (i, 0)),
                 pl.BlockSpec((1, MLA_HEADS * MLA_V, tm), lambda i: (i // tpb, 0, i % tpb))]
    out_shape = [jax.ShapeDtypeStruct((BATCH, 1024, SEQ), BF16),
                 jax.ShapeDtypeStruct((t, 1024), BF16),
                 jax.ShapeDtypeStruct((BATCH, MLA_HEADS * MLA_V, SEQ), BF16)]
    for (_, dil) in DIL_PATTERNS:
        for _ in range(3):
            out_specs.append(pl.BlockSpec((1, dil, tm // dil, dw),
                                          lambda i: (i // tpb, 0, i % tpb, 0)))
            out_shape.append(jax.ShapeDtypeStruct((BATCH, dil, SEQ // dil, dw), BF16))
    return pl.pallas_call(
        _even_proj_kernel,
        grid=(t // tm,),
        in_specs=[pl.BlockSpec((tm, D_MODEL), lambda i: (i, 0)),
                  pl.BlockSpec((tm, 1), lambda i: (i, 0))] + [_const_spec(c.shape) for c in consts],
        out_specs=out_specs,
        out_shape=out_shape,
        scratch_shapes=[pltpu.VMEM((3, dw // LANES, tm, LANES), F32)],
        compiler_params=_params(("parallel",)),
        name="even_proj",
    )(h2d, pos2d, *consts)


def _mla_kernel(qt_ref, k_ref, vt_ref, o_ref, *bufs):
    tq, tk = ATT_TILE_Q, ATT_TILE
    feat = lax.broadcasted_iota(jnp.int32, (MXU_DIM, 1), 0)
    visible = (lax.broadcasted_iota(jnp.int32, (tk, tq), 0)
               <= lax.broadcasted_iota(jnp.int32, (tk, tq), 1))
    visible_half = (lax.broadcasted_iota(jnp.int32, (tk, tk), 0)
                    <= lax.broadcasted_iota(jnp.int32, (tk, tk), 1))
    head_rows = []
    for hh in range(2):
        nope = (feat >= MLA_NOPE * hh) & (feat < MLA_NOPE * (hh + 1))
        rope = (feat >= LANES + MLA_ROPE * hh) & (feat < LANES + MLA_ROPE * (hh + 1))
        head_rows.append(nope | rope)
    empty_half = (jnp.full((1, tk), -jnp.inf, F32), jnp.zeros((1, tk), F32),
                  jnp.zeros((MLA_V, tk), F32))

    def consume(j, nk, slot, cms, carry):
        vt = vt_ref[0, :, j * tk:(j + nk) * tk]
        ps, stats = [], []
        for hh in range(2):
            m, l, _ = carry[hh]
            m_new = jnp.maximum(m, cms[hh])
            alpha = jnp.exp2(m - m_new)
            p = jnp.exp2(slot[hh, :nk * tk] - m_new)
            stats.append((m_new, alpha * l + jnp.sum(p, axis=0, keepdims=True), alpha))
            ps.append(p.astype(BF16))
        new = []
        for hh in range(2):
            m_new, l, alpha = stats[hh]
            acc = alpha * carry[hh][2] + jnp.dot(vt[hh * MLA_V:(hh + 1) * MLA_V, :], ps[hh],
                                                 preferred_element_type=F32)
            new.append((m_new, l, acc))
        return tuple(new)

    def start_right_half(j, slot, cms):
        vt = vt_ref[0, :, j * tk:(j + 1) * tk]
        ps = [jnp.exp2(slot[hh, :tk, :tk] - cms[hh]) for hh in range(2)]
        new = []
        for hh in range(2):
            acc = jnp.dot(vt[hh * MLA_V:(hh + 1) * MLA_V, :], ps[hh].astype(BF16),
                          preferred_element_type=F32)
            seen = (cms[hh], jnp.sum(ps[hh], axis=0, keepdims=True), acc)
            new.append(tuple(jnp.concatenate([e, s], axis=1) for e, s in zip(empty_half, seen)))
        return tuple(new)

    def finish(i, last, carry):
        carry = consume(*last, carry)
        o_t = jnp.concatenate([acc / l for (_, l, acc) in carry], axis=0)
        o_ref[i * tq:(i + 1) * tq, :] = o_t.T.astype(BF16)

    pending = None
    for i in range(SEQ // tq):
        s_a, s_b = bufs[2 * (i % 2)], bufs[2 * (i % 2) + 1]
        qt = qt_ref[0, :, i * tq:(i + 1) * tq]
        qts = [jnp.where(rows, qt, jnp.zeros_like(qt)) for rows in head_rows]

        def scores(j, nk, slot, qts=qts, causal=False, right_half=False):
            kt = k_ref[0, j * tk:(j + nk) * tk, :]
            cms = []
            for hh in range(2):
                s = jnp.dot(kt, qts[hh][:, tk:] if right_half else qts[hh],
                            preferred_element_type=F32)
                if causal:
                    s = jnp.where(visible_half if right_half else visible, s, -jnp.inf)
                if right_half:
                    slot[hh, :tk, :tk] = s
                else:
                    slot[hh, :nk * tk] = s
                cms.append(jnp.max(s, axis=0, keepdims=True))
            return tuple(cms)

        cms_a = scores(2 * i + 1, 1, s_a, causal=True, right_half=True)
        last = (2 * i, 1, s_b, scores(2 * i, 1, s_b, causal=True))
        if pending is not None:
            finish(*pending)
        carry = start_right_half(2 * i + 1, s_a, cms_a)

        for n in range(i):
            slot = (s_a, s_b)[n % 2]
            cms = scores(2 * n, 2, slot)
            carry = consume(*last, carry)
            last = (2 * n, 2, slot, cms)
        pending = (i, last, carry)
    finish(*pending)


def _mla_attention(qat, ka, vat):
    t = ka.shape[0]
    ka3 = ka.reshape(BATCH, SEQ, ka.shape[1])
    return pl.pallas_call(
        _mla_kernel,
        grid=(BATCH, MLA_HEADS // 2),
        in_specs=[pl.BlockSpec((1, MXU_DIM, SEQ), lambda b, c: (b, c, 0)),
                  pl.BlockSpec((1, SEQ, MXU_DIM), lambda b, c: (b, 0, c)),
                  pl.BlockSpec((1, LANES, SEQ), lambda b, c: (b, c, 0))],
        out_specs=pl.BlockSpec((SEQ, LANES), lambda b, c: (b, c)),
        out_shape=jax.ShapeDtypeStruct((t, MLA_HEADS * MLA_V), BF16),
        scratch_shapes=[pltpu.VMEM((2, 2 * ATT_TILE, ATT_TILE_Q), F32)] * 4,
        compiler_params=_params(("parallel", "parallel")),
        name="mla_attention",
    )(qat, ka3, vat)


DIL_QBLK = 2 * DIL_BLOCK
DIL_HALO = DIL_BLOCK
DIL_GROUP = 8


def _dil_kernel(q_ref, k_ref, v_ref, mb_ref, o_ref, lse_ref, vt_scr, *, nblk):
    qb, halo = DIL_QBLK, DIL_HALO
    ncls = q_ref.shape[1]
    lane = lax.broadcasted_iota(jnp.int32, (1, LANES), 1)
    low = lane < HEAD_DIM
    for g in range(ncls):
        vt_scr[g] = v_ref[0, g].astype(F32).T.astype(BF16)

    def blocks(items):
        work = []
        for g, r0, first in items:
            q = q_ref[0, g, pl.ds(r0, qb), :].astype(F32)
            if first:
                k0, nk = 0, qb
            else:
                k0 = r0 - halo if isinstance(r0, int) else pl.multiple_of(r0 - halo, halo)
                nk = qb + halo
            k = k_ref[0, g, pl.ds(k0, nk), :]
            for hh in range(2):
                qt = jnp.where(low if hh == 0 else ~low, q, 0.0).T.astype(BF16)
                s = jnp.dot(k, qt, preferred_element_type=F32) - (
                    mb_ref[0, hh, halo:, :] if first else mb_ref[0, hh])
                work.append((g, r0, k0, nk, hh, s))
        outs = []
        for g, r0, k0, nk, hh, s in work:
            m = jnp.max(s, axis=0, keepdims=True)
            p = jnp.exp2(s - m)
            den = jnp.sum(p, axis=0, keepdims=True)
            vt = vt_scr[g, hh * HEAD_DIM:(hh + 1) * HEAD_DIM, pl.ds(k0, nk)]
            o = jnp.dot(vt, p.astype(BF16), preferred_element_type=F32) / den
            lse = (m + jnp.log2(den)) * LN2
            outs.append((o, jnp.broadcast_to(lse, (HEAD_DIM, qb))))
        for n, (g, r0, _) in enumerate(items):
            pair = outs[2 * n:2 * n + 2]
            o_ref[0, g, pl.ds(r0, qb), :] = jnp.concatenate(
                [o for o, _ in pair], axis=0).T.astype(o_ref.dtype)
            lse_ref[0, g, pl.ds(r0, qb), :] = jnp.concatenate([l for _, l in pair], axis=0).T

    per = DIL_GROUP // ncls
    blocks([(g, n * qb, n == 0) for g in range(ncls) for n in range(per)])

    def body(n, carry):
        r0 = pl.multiple_of(per * n * qb, per * qb)
        blocks([(g, r0 + c * qb if c == 0 else pl.multiple_of(r0 + c * qb, qb), False)
                for g in range(ncls) for c in range(per)])
        return carry

    lax.fori_loop(1, nblk // per, body, 0)


def _dilated_branch(qc, kc, vc, mb, dil):
    l = SEQ // dil
    dw = DIL_HEADS * HEAD_DIM
    nblk = l // DIL_QBLK
    ncls = max(1, DIL_GROUP // nblk)
    assert dil % ncls == 0 and nblk % (DIL_GROUP // ncls) == 0
    blk = pl.BlockSpec((1, ncls, l, LANES), lambda c, b, r: (b, r, 0, c))
    return pl.pallas_call(
        functools.partial(_dil_kernel, nblk=nblk),
        grid=(DIL_HEADS // 2, BATCH, dil // ncls),
        in_specs=[blk, blk, blk,
                  pl.BlockSpec((1, 2, DIL_QBLK + DIL_HALO, DIL_QBLK), lambda c, b, r: (c, 0, 0, 0))],
        out_specs=[blk, blk],
        out_shape=[jax.ShapeDtypeStruct((BATCH, dil, l, dw), BF16),
                   jax.ShapeDtypeStruct((BATCH, dil, l, dw), F32)],
        scratch_shapes=[pltpu.VMEM((ncls, LANES, l), BF16)],
        compiler_params=_params(("parallel", "parallel", "parallel")),
        name=f"dilated_d{dil}",
    )(qc, kc, vc, mb)


def _dil_bias_table(dil, window):
    reach = window // dil
    ki = np.arange(DIL_QBLK + DIL_HALO)[:, None]
    qi = np.arange(DIL_QBLK)[None, :]
    step = DIL_HALO + qi - ki
    valid = (step >= 0) & (step <= reach)
    slopes = 2.0 ** (-8.0 * np.arange(1, DIL_HEADS + 1, dtype=np.float32) / DIL_HEADS)
    bias = slopes.astype(np.float32)[:, None, None] * (step * dil).astype(np.float32)[None]
    tab = np.where(valid[None], bias * LOG2E, np.inf).astype(np.float32)
    return jnp.asarray(tab.reshape(DIL_HEADS // 2, 2, DIL_QBLK + DIL_HALO, DIL_QBLK))


def _even_out_kernel(h_ref, oa_ref, *rest):
    branch_refs, w_ref, out_ref, stage_ref = rest[:-3], rest[-3], rest[-2], rest[-1]
    tm = h_ref.shape[0]
    vals = []
    for n, ((_, dil), ref) in enumerate(zip([pt for pt in DIL_PATTERNS for _ in range(2)],
                                            branch_refs)):
        if dil == 1:
            vals.append(ref[0, 0].astype(F32))
        else:
            slot = n - 2
            for r in range(dil):
                for c in range(ref.shape[-1] // LANES):
                    stage_ref[slot, c, pl.ds(r, tm // dil, stride=dil), :] = ref[
                        0, r, :, c * LANES:(c + 1) * LANES].astype(F32)
            vals.append(jnp.concatenate(
                [stage_ref[slot, c] for c in range(ref.shape[-1] // LANES)], axis=1))
    o1, l1, o2, l2, o3, l3 = vals
    lm = jnp.maximum(jnp.maximum(l1, l2), l3)
    w1, w2, w3 = jnp.exp(l1 - lm), jnp.exp(l2 - lm), jnp.exp(l3 - lm)
    mixed_b = (w1 * o1 + w2 * o2 + w3 * o3) / (w1 + w2 + w3)
    half = MLA_HEADS * MLA_V
    acc = jnp.dot(oa_ref[...], w_ref[:half, :], preferred_element_type=F32)
    acc += jnp.dot(mixed_b.astype(BF16), w_ref[half:, :], preferred_element_type=F32)
    out_ref[...] = h_ref[...] + acc


def _even_out(h2d, oa, branches, w_out):
    t = h2d.shape[0]
    tm = ROW_TILE
    tpb = SEQ // tm
    dw = DIL_HEADS * HEAD_DIM
    full = pl.BlockSpec((tm, D_MODEL), lambda i: (i, 0))
    specs, args = [], []
    for (_, dil), pair in zip(DIL_PATTERNS, branches):
        for a in pair:
            specs.append(pl.BlockSpec((1, dil, tm // dil, dw), lambda i: (i // tpb, 0, i % tpb, 0)))
            args.append(a)
    return pl.pallas_call(
        _even_out_kernel,
        grid=(t // tm,),
        in_specs=[full, pl.BlockSpec((tm, dw), lambda i: (i, 0))] + specs
        + [_const_spec(w_out.shape)],
        out_specs=full,
        out_shape=jax.ShapeDtypeStruct((t, D_MODEL), F32),
        scratch_shapes=[pltpu.VMEM((4, dw // LANES, tm, LANES), F32)],
        compiler_params=_params(("parallel",)),
        name="even_out_proj",
    )(h2d, oa, *args, w_out)


def _tail_kernel(*refs, with_out_proj):
    if with_out_proj:
        h_ref, o_ref, wout_ref, *refs = refs
    else:
        h_ref, *refs = refs
    p_ref, gff_ref, w1_ref, w2_ref, gple_ref, wg_ref, wp_ref, out_ref = refs
    h = h_ref[...]
    if with_out_proj:
        h = h + jnp.dot(o_ref[...], wout_ref[...], preferred_element_type=F32)
    n = _rms(h, gff_ref[...]).astype(BF16)
    chunk = D_MODEL
    for c in range(D_FF // chunk):
        u = jnp.maximum(jnp.dot(n, w1_ref[:, c * chunk:(c + 1) * chunk],
                                preferred_element_type=F32), 0.0)
        h = h + jnp.dot((u * u).astype(BF16), w2_ref[c * chunk:(c + 1) * chunk, :],
                        preferred_element_type=F32)
    n = _rms(h, gple_ref[...]).astype(BF16)
    gate = jax.nn.sigmoid(jnp.dot(n, wg_ref[...], preferred_element_type=F32))
    proj = jnp.dot(p_ref[...].astype(BF16), wp_ref[...], preferred_element_type=F32)
    out_ref[...] = h + gate * proj


def _tail(h2d, p2d, gff, w1, w2, gple, wg, wp, mix=None, w_out=None):
    t = h2d.shape[0]
    tm = ROW_TILE
    full = pl.BlockSpec((tm, D_MODEL), lambda i: (i, 0))
    with_out = mix is not None
    head_specs = [full, full, _const_spec(w_out.shape, single=True)] if with_out else [full]
    head_args = (h2d, mix, w_out) if with_out else (h2d,)
    consts = (gff, w1, w2, gple, wg, wp)
    return pl.pallas_call(
        functools.partial(_tail_kernel, with_out_proj=with_out),
        grid=(t // tm,),
        in_specs=head_specs + [pl.BlockSpec((tm, PLE_DIM), lambda i: (i, 0))]
        + [_const_spec(c.shape, single=True) for c in consts],
        out_specs=full,
        out_shape=jax.ShapeDtypeStruct((t, D_MODEL), F32),
        compiler_params=_params(("parallel",)),
        name="tail_with_out_proj" if with_out else "tail",
    )(*head_args, p2d, *consts)


def _odd_proj_kernel(x_ref, gmix_ref, win_ref, gq_ref, gk_ref, gm64_ref,
                     qt_ref, k_ref, vt_ref, km_ref):
    hn = _rms(x_ref[...], gmix_ref[...]).astype(BF16)
    z = jnp.dot(hn, win_ref[...], preferred_element_type=F32)
    dw = MOBA_HEADS * HEAD_DIM
    gq = gq_ref[...]
    gk = gk_ref[...]
    tm = x_ref.shape[0]
    for c in range(dw // MXU_DIM):
        lo = c * MXU_DIM
        qt_ref[0, lo:lo + MXU_DIM, :] = (
            _group_rms(z[:, lo:lo + MXU_DIM], gm64_ref[...]) * gq[:, lo:lo + MXU_DIM]
        ).T.astype(BF16)
        k = _group_rms(z[:, dw + lo:dw + lo + MXU_DIM], gm64_ref[...]) * gk[:, lo:lo + MXU_DIM]
        k_ref[:, lo:lo + MXU_DIM] = k.astype(BF16)
        for r in range(tm // MOBA_BLOCK):
            km_ref[r, :, lo:lo + MXU_DIM] = jnp.mean(
                k[r * MOBA_BLOCK:(r + 1) * MOBA_BLOCK], axis=0, keepdims=True)
    vt_ref[0] = z[:, 2 * dw:].T.astype(BF16)


def _odd_proj(h2d, gmix, win, gq, gk, gm64):
    t = h2d.shape[0]
    tm = ROW_TILE
    dw = MOBA_HEADS * HEAD_DIM
    consts = (gmix, win, gq, gk, gm64)
    full = pl.BlockSpec((tm, dw), lambda i: (i, 0))
    tpb = SEQ // tm
    tspec = pl.BlockSpec((1, dw, tm), lambda i: (i // tpb, 0, i % tpb))
    return pl.pallas_call(
        _odd_proj_kernel,
        grid=(t // tm,),
        in_specs=[pl.BlockSpec((tm, D_MODEL), lambda i: (i, 0))] + [_const_spec(c.shape) for c in consts],
        out_specs=[tspec, full, tspec,
                   pl.BlockSpec((tm // MOBA_BLOCK, 1, dw), lambda i: (i, 0, 0))],
        out_shape=[jax.ShapeDtypeStruct((BATCH, dw, SEQ), BF16), jax.ShapeDtypeStruct((t, dw), BF16),
                   jax.ShapeDtypeStruct((BATCH, dw, SEQ), BF16),
                   jax.ShapeDtypeStruct((t // MOBA_BLOCK, 1, dw), F32)],
        compiler_params=_params(("parallel",)),
        name="odd_proj",
    )(h2d, *consts)


def _moba_kernel(qt_ref, k_ref, vt_ref, km_ref, sl_ref, kb_ref, o_ref, *bufs):
    tq, tk = ATT_TILE_Q, MOBA_BLOCK
    nb = SEQ // MOBA_BLOCK
    visible = (lax.broadcasted_iota(jnp.int32, (tk, tq), 0)
               <= lax.broadcasted_iota(jnp.int32, (tk, tq), 1))
    visible_half = (lax.broadcasted_iota(jnp.int32, (tk, tk), 0)
                    <= lax.broadcasted_iota(jnp.int32, (tk, tk), 1))
    row = lax.broadcasted_iota(jnp.int32, (LANES, tq), 0)
    blk_id = lax.broadcasted_iota(jnp.int32, (nb, tq), 0)
    half = lax.broadcasted_iota(jnp.int32, (1, tq), 1) // tk
    kbias = kb_ref[0]
    slopes = [sl_ref[0, :, hh * HEAD_DIM:hh * HEAD_DIM + 1] for hh in range(2)]
    km_terms, resid = [], km_ref[0]
    for _ in range(KM_PARTS):
        km_terms.append(resid.astype(BF16))
        resid = resid - km_terms[-1].astype(F32)
    km_cat = jnp.concatenate(km_terms, axis=0)
    km_cat = jnp.concatenate([km_cat, jnp.zeros_like(km_cat)], axis=1)
    empty_half = (jnp.full((1, tk), MAX_FLOOR, F32), jnp.zeros((1, tk), F32),
                  jnp.zeros((HEAD_DIM, tk), F32))

    def prepare(i):
        own = 2 * i + half
        q_t = qt_ref[0, :, i * tq:(i + 1) * tq]
        heads = []
        for hh in range(2):
            qt = jnp.where((row >= HEAD_DIM * hh) & (row < HEAD_DIM * (hh + 1)), q_t,
                           jnp.zeros_like(q_t))
            ones = jnp.where((row >= KB_PARTS * hh) & (row < KB_PARTS * (hh + 1)), 1.0, 0.0)
            qt = jnp.concatenate([qt, ones.astype(BF16)], axis=0)
            gate_terms = jnp.dot(km_cat, qt, preferred_element_type=F32)
            gate = sum(gate_terms[n * nb:(n + 1) * nb] for n in range(KM_PARTS))
            g = jnp.where(blk_id < own, gate, -jnp.inf)
            sel = []
            for _ in range(MOBA_TOPK):
                mx = jnp.max(g, axis=0, keepdims=True)
                idx = jnp.min(jnp.where(g == mx, blk_id, nb), axis=0, keepdims=True)
                sel.append(jnp.where(mx > -jnp.inf, idx, -1))
                g = jnp.where(blk_id == idx, -jnp.inf, g)
            heads.append((qt, sel))
        return heads, own

    def scores(tile, j, nk, slot, causal=False, right_half=False):
        heads, _ = tile
        kt = jnp.concatenate([k_ref[0, j * tk:(j + nk) * tk, :],
                              jnp.concatenate([kbias] * nk, axis=0)], axis=1)
        cms = []
        for hh in range(2):
            qt = heads[hh][0]
            s = jnp.dot(kt, qt[:, tk:] if right_half else qt,
                        preferred_element_type=F32)
            if causal:
                s = jnp.where(visible_half if right_half else visible, s, -jnp.inf)
            if right_half:
                slot[hh, :tk, :tk] = s
            else:
                slot[hh, :nk * tk] = s
            cms.append(tuple(jnp.max(s[b * tk:(b + 1) * tk], axis=0, keepdims=True)
                             for b in range(nk)))
        return tuple(cms)

    def start_right_half(i, j, slot, cms):
        vt = vt_ref[0, :, j * tk:(j + 1) * tk]
        ps = [jnp.exp2(slot[hh, :tk, :tk] - cms[hh][0]) for hh in range(2)]
        new = []
        for hh in range(2):
            acc = jnp.dot(vt[hh * HEAD_DIM:(hh + 1) * HEAD_DIM, :], ps[hh].astype(BF16),
                          preferred_element_type=F32)
            off = slopes[hh] * float((2 * i - j) * tk)
            seen = (cms[hh][0] - off, jnp.sum(ps[hh], axis=0, keepdims=True), acc)
            new.append(tuple(jnp.concatenate([e, s], axis=1) for e, s in zip(empty_half, seen)))
        return tuple(new)

    def consume(i, tile, j, nk, slot, cms, carry):
        heads, own = tile
        vt = vt_ref[0, :, j * tk:(j + nk) * tk]
        ps, stats = [], []
        for hh in range(2):
            sel = heads[hh][1]
            m, l, _ = carry[hh]
            blocks, m_new = [], m
            for b in range(nk):
                jb = j + b
                seen = (sel[0] == jb) | (sel[1] == jb) | (sel[2] == jb) | (own == jb)
                off = slopes[hh] * float((2 * i - jb) * tk)
                m_new = jnp.where(seen, jnp.maximum(m_new, cms[hh][b] - off), m_new)
                blocks.append((seen, off))
            alpha = jnp.exp2(m - m_new)
            p = jnp.concatenate(
                [jnp.exp2(slot[hh, b * tk:(b + 1) * tk] - jnp.where(seen, m_new + off, jnp.inf))
                 for b, (seen, off) in enumerate(blocks)], axis=0)
            stats.append((m_new, alpha * l + jnp.sum(p, axis=0, keepdims=True), alpha))
            ps.append(p.astype(BF16))
        new = []
        for hh in range(2):
            m_new, l, alpha = stats[hh]
            acc = alpha * carry[hh][2] + jnp.dot(vt[hh * HEAD_DIM:(hh + 1) * HEAD_DIM, :], ps[hh],
                                                 preferred_element_type=F32)
            new.append((m_new, l, acc))
        return tuple(new)

    def finish(i, tile, last, carry):
        carry = consume(i, tile, *last, carry)
        o_t = jnp.concatenate([acc / l for (_, l, acc) in carry], axis=0)
        o_ref[i * tq:(i + 1) * tq, :] = o_t.T.astype(BF16)

    pending = None
    for i in range(SEQ // tq):
        s_a, s_b = bufs[2 * (i % 2)], bufs[2 * (i % 2) + 1]
        tile = prepare(i)
        cms_a = scores(tile, 2 * i + 1, 1, s_a, causal=True, right_half=True)
        last = (2 * i, 1, s_b, scores(tile, 2 * i, 1, s_b, causal=True))
        if pending is not None:
            finish(*pending)
        carry = start_right_half(i, 2 * i + 1, s_a, cms_a)

        for n in range(i):
            slot = (s_a, s_b)[n % 2]
            cms = scores(tile, 2 * n, 2, slot)
            carry = consume(i, tile, *last, carry)
            last = (2 * n, 2, slot, cms)
        pending = (i, tile, last, carry)
    finish(*pending)


def _moba_attention(qt3, k, vt3, km, slopes, kbias):
    t = k.shape[0]
    nb = SEQ // MOBA_BLOCK
    dw = MOBA_HEADS * HEAD_DIM
    k3 = k.reshape(BATCH, SEQ, dw)
    km3 = km.reshape(BATCH, nb, dw)
    tspec = pl.BlockSpec((1, LANES, SEQ), lambda b, c: (b, c, 0))
    return pl.pallas_call(
        _moba_kernel,
        grid=(BATCH, MOBA_HEADS // 2),
        in_specs=[tspec,
                  pl.BlockSpec((1, SEQ, LANES), lambda b, c: (b, 0, c)),
                  tspec,
                  pl.BlockSpec((1, nb, LANES), lambda b, c: (b, 0, c)),
                  pl.BlockSpec((1, 1, LANES), lambda b, c: (c, 0, 0)),
                  pl.BlockSpec((1, MOBA_BLOCK, LANES), lambda b, c: (c, 0, 0))],
        out_specs=pl.BlockSpec((SEQ, LANES), lambda b, c: (b, c)),
        out_shape=jax.ShapeDtypeStruct((t, dw), BF16),
        scratch_shapes=[pltpu.VMEM((2, 2 * MOBA_BLOCK, ATT_TILE_Q), F32)] * 4,
        compiler_params=_params(("parallel", "parallel")),
        name="moba_attention",
    )(qt3, k3, vt3, km3, slopes, kbias)


def _group_matrix(sizes, width):
    g = np.zeros((width, width), np.float32)
    lo = 0
    for n in sizes:
        g[lo:lo + n, lo:lo + n] = 1.0 / n
        lo += n
    return jnp.asarray(g, BF16)


def _alibi_slopes_log2(n_heads):
    slopes = 2.0 ** (-8.0 * np.arange(1, n_heads + 1, dtype=np.float32) / n_heads)
    return slopes.astype(np.float64) * LOG2E


def _pair_slopes(n_heads):
    tab = np.repeat(_alibi_slopes_log2(n_heads).astype(np.float32), HEAD_DIM)
    return jnp.asarray(tab.reshape(n_heads // 2, 1, LANES))


def _pair_key_bias(n_heads, block):
    bias = _alibi_slopes_log2(n_heads)[:, None] * np.arange(block, dtype=np.float64)[None, :]
    tab = np.zeros((n_heads // 2, block, LANES), np.float32)
    for part in range(KB_PARTS):
        term = bias.astype(jnp.bfloat16).astype(np.float64)
        bias = bias - term
        for h in range(n_heads):
            tab[h // 2, :, KB_PARTS * (h % 2) + part] = term[h]
    return jnp.asarray(tab, BF16)


def _even_weights(w_in, w_uq, w_ukv, qn_nope, qn_rope, kn_nope, kn_rope, dil_qn, dil_kn):
    o1 = Q_LORA
    o2 = o1 + KV_LORA
    o3 = o2 + MLA_ROPE
    kr = w_in[:, o2:o3]
    pad64 = jnp.zeros((D_MODEL, LANES - 2 * MLA_ROPE), F32)
    win = jnp.concatenate([w_in[:, :o2], kr, kr, pad64, w_in[:, o3:]], axis=1).astype(BF16)

    qd = MLA_NOPE + MLA_ROPE
    a_scale = qd ** -0.5
    zq = jnp.zeros((Q_LORA, LANES - 2 * MLA_ROPE), F32)
    zg = jnp.zeros((LANES - 2 * MLA_ROPE,), F32)
    cols, gains = [], []
    for c in range(MLA_HEADS // 2):
        h0, h1 = 2 * c, 2 * c + 1
        cols += [w_uq[:, h0 * qd:h0 * qd + MLA_NOPE], w_uq[:, h1 * qd:h1 * qd + MLA_NOPE],
                 w_uq[:, h0 * qd + MLA_NOPE:(h0 + 1) * qd], w_uq[:, h1 * qd + MLA_NOPE:(h1 + 1) * qd], zq]
        gains += [qn_nope, qn_nope, qn_rope, qn_rope, zg]
    wuq = jnp.concatenate(cols, axis=1).astype(BF16)
    gq = (jnp.concatenate(gains) * (a_scale * LOG2E))[None, :]

    kvd = MLA_NOPE + MLA_V
    kcols = [w_ukv[:, h * kvd:h * kvd + MLA_NOPE] for h in range(MLA_HEADS)]
    vcols = [w_ukv[:, h * kvd + MLA_NOPE:(h + 1) * kvd] for h in range(MLA_HEADS)]
    wukv = jnp.concatenate(kcols + vcols, axis=1).astype(BF16)
    gkn = jnp.tile(kn_nope, MLA_HEADS)[None, :]
    gkr = jnp.concatenate([kn_rope, kn_rope, zg])[None, :]
    gqb = (jnp.tile(dil_qn, DIL_HEADS) * (HEAD_DIM ** -0.5 * LOG2E))[None, :]
    gkb = jnp.tile(dil_kn, DIL_HEADS)[None, :]
    return win, wuq, wukv, gq, gkn, gkr, gqb, gkb


def kernel(x, p, positions, e_w_in, e_cq_norm, e_ckv_norm, e_w_uq, e_w_ukv, e_qn_nope, e_qn_rope, e_kn_nope, e_kn_rope, e_dil_qn, e_dil_kn, e_w_out, o_w_in, o_qn, o_kn, o_w_out, mix_norm, ff_norm, w_ff1, w_ff2, ple_norm, w_ple_gate, w_ple_proj):
    t = BATCH * SEQ
    h = x.reshape(t, D_MODEL)
    pos2d = positions.reshape(t, 1)
    p2d = p.reshape(DEPTH, t, PLE_DIM)

    gmq = _group_matrix((MLA_NOPE, MLA_NOPE, MLA_ROPE, MLA_ROPE, LANES - 2 * MLA_ROPE), MXU_DIM)
    gm64 = _group_matrix((HEAD_DIM,) * (MXU_DIM // HEAD_DIM), MXU_DIM)
    gm32 = _group_matrix((MLA_ROPE,) * (LANES // MLA_ROPE), LANES)
    half = MLA_ROPE // 2
    invf = ROPE_THETA ** (-(jnp.arange(LANES) % half).astype(F32) / half)
    invf = invf[None, :]

    for i in range(DEPTH):
        j = i // 2
        gmix = mix_norm[i][None, :]
        if i % 2 == 0:
            win, wuq, wukv, gq, gkn, gkr, gqb, gkb = _even_weights(
                e_w_in[j], e_w_uq[j], e_w_ukv[j], e_qn_nope[j], e_qn_rope[j], e_kn_nope[j],
                e_kn_rope[j], e_dil_qn[j], e_dil_kn[j])
            qa, ka, vat, *qkv_b = _even_proj(
                h, pos2d, gmix, win, e_cq_norm[j][None, :], e_ckv_norm[j][None, :], wuq, wukv,
                gq, gkn, gkr, gqb, gkb, gmq, gm64, gm32, invf)
            oa = _mla_attention(qa, ka, vat)
            branches = [_dilated_branch(*qkv_b[3 * n:3 * n + 3], _dil_bias_table(d, w), d)
                        for n, (w, d) in enumerate(DIL_PATTERNS)]
            h = _even_out(h, oa, branches, e_w_out[j].astype(BF16))
            mix = {}
        else:
            gq = (jnp.tile(o_qn[j], MOBA_HEADS) * (HEAD_DIM ** -0.5 * LOG2E))[None, :]
            gk = jnp.tile(o_kn[j], MOBA_HEADS)[None, :]
            q, k, vt, km = _odd_proj(h, gmix, o_w_in[j].astype(BF16), gq, gk, gm64)
            o = _moba_attention(q, k, vt, km, _pair_slopes(MOBA_HEADS),
                                _pair_key_bias(MOBA_HEADS, MOBA_BLOCK))
            mix = {"mix": o, "w_out": o_w_out[j].astype(BF16)}
        h = _tail(h, p2d[i], ff_norm[i][None, :], w_ff1[i].astype(BF16), w_ff2[i].astype(BF16),
                  ple_norm[i][None, :], w_ple_gate[i].astype(BF16), w_ple_proj[i].astype(BF16),
                  **mix)
    return h.reshape(BATCH, SEQ, D_MODEL)
```

```python
import functools

import numpy as np
import jax
import jax.numpy as jnp
from jax import lax
from jax.experimental import pallas as pl
from jax.experimental.pallas import tpu as pltpu

D_MODEL = 1024
BATCH = 8
SEQ = 4096
DEPTH = 2
HEAD_DIM = 64
EPS = 1e-6
MLA_HEADS = 8
MLA_NOPE = 64
MLA_ROPE = 32
MLA_V = 64
Q_LORA = 384
KV_LORA = 256
ROPE_THETA = 10000.0
DIL_HEADS = 8
DIL_PATTERNS = ((128, 1), (512, 4), (2048, 16))
DIL_BLOCK = 128
MOBA_HEADS = 16
MOBA_BLOCK = 256
MOBA_TOPK = 3
D_FF = 4 * D_MODEL
PLE_DIM = 256

LANES = 128
MXU_DIM = 256
VMEM_LIMIT = 56 * 1024 * 1024
ROW_TILE = 512
ATT_TILE = 256
ATT_TILE_Q = 512
MAX_FLOOR = -1e30
KB_PARTS = 3
KM_PARTS = 3
LOG2E = float(np.log2(np.e))
LN2 = float(np.log(2.0))

F32 = jnp.float32
BF16 = jnp.bfloat16


def _const_spec(shape, single=False):
    nd = len(shape)
    mode = {"pipeline_mode": pl.Buffered(1)} if single else {}
    return pl.BlockSpec(shape, lambda *_: (0,) * nd, **mode)


def _params(sem):
    return pltpu.CompilerParams(dimension_semantics=sem, vmem_limit_bytes=VMEM_LIMIT)


def _rms(x, g):
    return x * lax.rsqrt(jnp.mean(x * x, axis=-1, keepdims=True) + EPS) * g


def _group_rms(x, gmat):
    ms = jnp.dot((x * x).astype(BF16), gmat, preferred_element_type=F32)
    return x * lax.rsqrt(ms + EPS)


def _even_proj_kernel(x_ref, pos_ref, gmix_ref, win_ref, gcq_ref, gckv_ref, wuq_ref, wukv_ref,
                      gq_ref, gkn_ref, gkr_ref, gqb_ref, gkb_ref, gmq_ref, gm64_ref, gm32_ref,
                      invf_ref, qat_ref, ka_ref, vat_ref, *rest):
    dil_refs, stage_ref = rest[:-1], rest[-1]
    hn = _rms(x_ref[...], gmix_ref[...]).astype(BF16)
    z = jnp.dot(hn, win_ref[...], preferred_element_type=F32)
    o1 = Q_LORA
    o2 = o1 + KV_LORA
    o3 = o2 + LANES
    dw = DIL_HEADS * HEAD_DIM
    c_q = _rms(z[:, :o1], gcq_ref[...]).astype(BF16)
    c_kv = _rms(z[:, o1:o2], gckv_ref[...]).astype(BF16)

    ang = pos_ref[...].astype(F32) * invf_ref[...]
    cos = jnp.cos(ang)
    sin = jnp.sin(ang)
    lane = lax.broadcasted_iota(jnp.int32, (1, LANES), 1)
    first_half = (lane % MLA_ROPE) < (MLA_ROPE // 2)

    def rope(xr):
        rot = jnp.where(first_half, -pltpu.roll(xr, LANES - MLA_ROPE // 2, 1),
                        pltpu.roll(xr, MLA_ROPE // 2, 1))
        return xr * cos + rot * sin

    q = jnp.dot(c_q, wuq_ref[...], preferred_element_type=F32)
    gq = gq_ref[...]
    for c in range(MLA_HEADS // 2):
        lo = c * MXU_DIM
        qc = _group_rms(q[:, lo:lo + MXU_DIM], gmq_ref[...]) * gq[:, lo:lo + MXU_DIM]
        qat_ref[0, lo:lo + LANES, :] = qc[:, :LANES].T.astype(BF16)
        qat_ref[0, lo + LANES:lo + MXU_DIM, :] = rope(qc[:, LANES:]).T.astype(BF16)

    kv = jnp.dot(c_kv, wukv_ref[...], preferred_element_type=F32)
    k_rope = rope(_group_rms(z[:, o2:o3], gm32_ref[...]) * gkr_ref[...]).astype(BF16)
    gkn = gkn_ref[...]
    for c in range(MLA_HEADS // 2):
        lo = c * LANES
        if c % 2 == 0:
            kn2 = _group_rms(kv[:, lo:lo + MXU_DIM], gm64_ref[...]) * gkn[:, lo:lo + MXU_DIM]
        kn = kn2[:, (c % 2) * LANES:(c % 2 + 1) * LANES]
        ka_ref[:, c * MXU_DIM:c * MXU_DIM + LANES] = kn.astype(BF16)
        ka_ref[:, c * MXU_DIM + LANES:(c + 1) * MXU_DIM] = k_rope
    vat_ref[0] = kv[:, dw:].T.astype(BF16)

    gqb = gqb_ref[...]
    gkb = gkb_ref[...]
    qkv = []
    for c in range(dw // MXU_DIM):
        lo = c * MXU_DIM
        qkv.append((
            _group_rms(z[:, o3 + lo:o3 + lo + MXU_DIM], gm64_ref[...]) * gqb[:, lo:lo + MXU_DIM],
            _group_rms(z[:, o3 + dw + lo:o3 + dw + lo + MXU_DIM], gm64_ref[...])
            * gkb[:, lo:lo + MXU_DIM],
            z[:, o3 + 2 * dw + lo:o3 + 2 * dw + lo + MXU_DIM]))
    tm = x_ref.shape[0]
    for a in range(3):
        for c in range(dw // LANES):
            half = (c % 2) * LANES
            stage_ref[a, c] = qkv[c // 2][a][:, half:half + LANES]
            for (_, dil), out_ref in zip(DIL_PATTERNS, dil_refs[a::3]):
                for r in range(dil):
                    out_ref[0, r, :, c * LANES:(c + 1) * LANES] = stage_ref[
                        a, c, pl.ds(r, tm // dil, stride=dil), :].astype(BF16)


def _even_proj(h2d, pos2d, gmix, win, gcq, gckv, wuq, wukv, gq, gkn, gkr, gqb, gkb,
               gmq, gm64, gm32, invf):
    t = h2d.shape[0]
    tm = ROW_TILE
    consts = (gmix, win, gcq, gckv, wuq, wukv, gq, gkn, gkr, gqb, gkb, gmq, gm64, gm32, invf)
    tpb = SEQ // tm
    dw = DIL_HEADS * HEAD_DIM
    out_specs = [pl.BlockSpec((1, 1024, tm), lambda i: (i // tpb, 0, i % tpb)),
                 pl.BlockSpec((tm, 1024), lambda i: (i, 0)),
                 pl.BlockSpec((1, MLA_HEADS * MLA_V, tm), lambda i: (i // tpb, 0, i % tpb))]
    out_shape = [jax.ShapeDtypeStruct((BATCH, 1024, SEQ), BF16),
                 jax.ShapeDtypeStruct((t, 1024), BF16),
                 jax.ShapeDtypeStruct((BATCH, MLA_HEADS * MLA_V, SEQ), BF16)]
    for (_, dil) in DIL_PATTERNS:
        for _ in range(3):
            out_specs.append(pl.BlockSpec((1, dil, tm // dil, dw),
                                          lambda i: (i // tpb, 0, i % tpb, 0)))
            out_shape.append(jax.ShapeDtypeStruct((BATCH, dil, SEQ // dil, dw), BF16))
    return pl.pallas_call(
        _even_proj_kernel,
        grid=(t // tm,),
        in_specs=[pl.BlockSpec((tm, D_MODEL), lambda i: (i, 0)),
                  pl.BlockSpec((tm, 1), lambda i: (i, 0))] + [_const_spec(c.shape) for c in consts],
        out_specs=out_specs,
        out_shape=out_shape,
        scratch_shapes=[pltpu.VMEM((3, dw // LANES, tm, LANES), F32)],
        compiler_params=_params(("parallel",)),
        name="even_proj",
    )(h2d, pos2d, *consts)


def _mla_kernel(qt_ref, k_ref, vt_ref, o_ref, *bufs):
    tq, tk = ATT_TILE_Q, ATT_TILE
    feat = lax.broadcasted_iota(jnp.int32, (MXU_DIM, 1), 0)
    visible = (lax.broadcasted_iota(jnp.int32, (tk, tq), 0)
               <= lax.broadcasted_iota(jnp.int32, (tk, tq), 1))
    visible_half = (lax.broadcasted_iota(jnp.int32, (tk, tk), 0)
                    <= lax.broadcasted_iota(jnp.int32, (tk, tk), 1))
    head_rows = []
    for hh in range(2):
        nope = (feat >= MLA_NOPE * hh) & (feat < MLA_NOPE * (hh + 1))
        rope = (feat >= LANES + MLA_ROPE * hh) & (feat < LANES + MLA_ROPE * (hh + 1))
        head_rows.append(nope | rope)
    empty_half = (jnp.full((1, tk), -jnp.inf, F32), jnp.zeros((1, tk), F32),
                  jnp.zeros((MLA_V, tk), F32))

    def consume(j, nk, slot, cms, carry):
        vt = vt_ref[0, :, j * tk:(j + nk) * tk]
        ps, stats = [], []
        for hh in range(2):
            m, l, _ = carry[hh]
            m_new = jnp.maximum(m, cms[hh])
            alpha = jnp.exp2(m - m_new)
            p = jnp.exp2(slot[hh, :nk * tk] - m_new)
            stats.append((m_new, alpha * l + jnp.sum(p, axis=0, keepdims=True), alpha))
            ps.append(p.astype(BF16))
        new = []
        for hh in range(2):
            m_new, l, alpha = stats[hh]
            acc = alpha * carry[hh][2] + jnp.dot(vt[hh * MLA_V:(hh + 1) * MLA_V, :], ps[hh],
                                                 preferred_element_type=F32)
            new.append((m_new, l, acc))
        return tuple(new)

    def start_right_half(j, slot, cms):
        vt = vt_ref[0, :, j * tk:(j + 1) * tk]
        ps = [jnp.exp2(slot[hh, :tk, :tk] - cms[hh]) for hh in range(2)]
        new = []
        for hh in range(2):
            acc = jnp.dot(vt[hh * MLA_V:(hh + 1) * MLA_V, :], ps[hh].astype(BF16),
                          preferred_element_type=F32)
            seen = (cms[hh], jnp.sum(ps[hh], axis=0, keepdims=True), acc)
            new.append(tuple(jnp.concatenate([e, s], axis=1) for e, s in zip(empty_half, seen)))
        return tuple(new)

    def finish(i, last, carry):
        carry = consume(*last, carry)
        o_t = jnp.concatenate([acc / l for (_, l, acc) in carry], axis=0)
        o_ref[i * tq:(i + 1) * tq, :] = o_t.T.astype(BF16)

    pending = None
    for i in range(SEQ // tq):
        s_a, s_b = bufs[2 * (i % 2)], bufs[2 * (i % 2) + 1]
        qt = qt_ref[0, :, i * tq:(i + 1) * tq]
        qts = [jnp.where(rows, qt, jnp.zeros_like(qt)) for rows in head_rows]

        def scores(j, nk, slot, qts=qts, causal=False, right_half=False):
            kt = k_ref[0, j * tk:(j + nk) * tk, :]
            cms = []
            for hh in range(2):
                s = jnp.dot(kt, qts[hh][:, tk:] if right_half else qts[hh],
                            preferred_element_type=F32)
                if causal:
                    s = jnp.where(visible_half if right_half else visible, s, -jnp.inf)
                if right_half:
                    slot[hh, :tk, :tk] = s
                else:
                    slot[hh, :nk * tk] = s
                cms.append(jnp.max(s, axis=0, keepdims=True))
            return tuple(cms)

        cms_a = scores(2 * i + 1, 1, s_a, causal=True, right_half=True)
        last = (2 * i, 1, s_b, scores(2 * i, 1, s_b, causal=True))
        if pending is not None:
            finish(*pending)
        carry = start_right_half(2 * i + 1, s_a, cms_a)

        for n in range(i):
            slot = (s_a, s_b)[n % 2]
            cms = scores(2 * n, 2, slot)
            carry = consume(*last, carry)
            last = (2 * n, 2, slot, cms)
        pending = (i, last, carry)
    finish(*pending)


def _mla_attention(qat, ka, vat):
    t = ka.shape[0]
    ka3 = ka.reshape(BATCH, SEQ, ka.shape[1])
    return pl.pallas_call(
        _mla_kernel,
        grid=(BATCH, MLA_HEADS // 2),
        in_specs=[pl.BlockSpec((1, MXU_DIM, SEQ), lambda b, c: (b, c, 0)),
                  pl.BlockSpec((1, SEQ, MXU_DIM), lambda b, c: (b, 0, c)),
                  pl.BlockSpec((1, LANES, SEQ), lambda b, c: (b, c, 0))],
        out_specs=pl.BlockSpec((SEQ, LANES), lambda b, c: (b, c)),
        out_shape=jax.ShapeDtypeStruct((t, MLA_HEADS * MLA_V), BF16),
        scratch_shapes=[pltpu.VMEM((2, 2 * ATT_TILE, ATT_TILE_Q), F32)] * 4,
        compiler_params=_params(("parallel", "parallel")),
        name="mla_attention",
    )(qat, ka3, vat)


DIL_QBLK = 2 * DIL_BLOCK
DIL_HALO = DIL_BLOCK
DIL_GROUP = 8


def _dil_kernel(q_ref, k_ref, v_ref, mb_ref, o_ref, lse_ref, vt_scr, *, nblk):
    qb, halo = DIL_QBLK, DIL_HALO
    ncls = q_ref.shape[1]
    lane = lax.broadcasted_iota(jnp.int32, (1, LANES), 1)
    low = lane < HEAD_DIM
    for g in range(ncls):
        vt_scr[g] = v_ref[0, g].astype(F32).T.astype(BF16)

    def blocks(items):
        work = []
        for g, r0, first in items:
            q = q_ref[0, g, pl.ds(r0, qb), :].astype(F32)
            if first:
                k0, nk = 0, qb
            else:
                k0 = r0 - halo if isinstance(r0, int) else pl.multiple_of(r0 - halo, halo)
                nk = qb + halo
            k = k_ref[0, g, pl.ds(k0, nk), :]
            for hh in range(2):
                qt = jnp.where(low if hh == 0 else ~low, q, 0.0).T.astype(BF16)
                s = jnp.dot(k, qt, preferred_element_type=F32) - (
                    mb_ref[0, hh, halo:, :] if first else mb_ref[0, hh])
                work.append((g, r0, k0, nk, hh, s))
        outs = []
        for g, r0, k0, nk, hh, s in work:
            m = jnp.max(s, axis=0, keepdims=True)
            p = jnp.exp2(s - m)
            den = jnp.sum(p, axis=0, keepdims=True)
            vt = vt_scr[g, hh * HEAD_DIM:(hh + 1) * HEAD_DIM, pl.ds(k0, nk)]
            o = jnp.dot(vt, p.astype(BF16), preferred_element_type=F32) / den
            lse = (m + jnp.log2(den)) * LN2
            outs.append((o, jnp.broadcast_to(lse, (HEAD_DIM, qb))))
        for n, (g, r0, _) in enumerate(items):
            pair = outs[2 * n:2 * n + 2]
            o_ref[0, g, pl.ds(r0, qb), :] = jnp.concatenate(
                [o for o, _ in pair], axis=0).T.astype(o_ref.dtype)
            lse_ref[0, g, pl.ds(r0, qb), :] = jnp.concatenate([l for _, l in pair], axis=0).T

    per = DIL_GROUP // ncls
    blocks([(g, n * qb, n == 0) for g in range(ncls) for n in range(per)])

    def body(n, carry):
        r0 = pl.multiple_of(per * n * qb, per * qb)
        blocks([(g, r0 + c * qb if c == 0 else pl.multiple_of(r0 + c * qb, qb), False)
                for g in range(ncls) for c in range(per)])
        return carry

    lax.fori_loop(1, nblk // per, body, 0)


def _dilated_branch(qc, kc, vc, mb, dil):
    l = SEQ // dil
    dw = DIL_HEADS * HEAD_DIM
    nblk = l // DIL_QBLK
    ncls = max(1, DIL_GROUP // nblk)
    assert dil % ncls == 0 and nblk % (DIL_GROUP // ncls) == 0
    blk = pl.BlockSpec((1, ncls, l, LANES), lambda c, b, r: (b, r, 0, c))
    return pl.pallas_call(
        functools.partial(_dil_kernel, nblk=nblk),
        grid=(DIL_HEADS // 2, BATCH, dil // ncls),
        in_specs=[blk, blk, blk,
                  pl.BlockSpec((1, 2, DIL_QBLK + DIL_HALO, DIL_QBLK), lambda c, b, r: (c, 0, 0, 0))],
        out_specs=[blk, blk],
        out_shape=[jax.ShapeDtypeStruct((BATCH, dil, l, dw), BF16),
                   jax.ShapeDtypeStruct((BATCH, dil, l, dw), F32)],
        scratch_shapes=[pltpu.VMEM((ncls, LANES, l), BF16)],
        compiler_params=_params(("parallel", "parallel", "parallel")),
        name=f"dilated_d{dil}",
    )(qc, kc, vc, mb)


def _dil_bias_table(dil, window):
    reach = window // dil
    ki = np.arange(DIL_QBLK + DIL_HALO)[:, None]
    qi = np.arange(DIL_QBLK)[None, :]
    step = DIL_HALO + qi - ki
    valid = (step >= 0) & (step <= reach)
    slopes = 2.0 ** (-8.0 * np.arange(1, DIL_HEADS + 1, dtype=np.float32) / DIL_HEADS)
    bias = slopes.astype(np.float32)[:, None, None] * (step * dil).astype(np.float32)[None]
    tab = np.where(valid[None], bias * LOG2E, np.inf).astype(np.float32)
    return jnp.asarray(tab.reshape(DIL_HEADS // 2, 2, DIL_QBLK + DIL_HALO, DIL_QBLK))


def _even_out_kernel(h_ref, oa_ref, *rest):
    branch_refs, w_ref, out_ref, stage_ref = rest[:-3], rest[-3], rest[-2], rest[-1]
    tm = h_ref.shape[0]
    vals = []
    for n, ((_, dil), ref) in enumerate(zip([pt for pt in DIL_PATTERNS for _ in range(2)],
                                            branch_refs)):
        if dil == 1:
            vals.append(ref[0, 0].astype(F32))
        else:
            slot = n - 2
            for r in range(dil):
                for c in range(ref.shape[-1] // LANES):
                    stage_ref[slot, c, pl.ds(r, tm // dil, stride=dil), :] = ref[
                        0, r, :, c * LANES:(c + 1) * LANES].astype(F32)
            vals.append(jnp.concatenate(
                [stage_ref[slot, c] for c in range(ref.shape[-1] // LANES)], axis=1))
    o1, l1, o2, l2, o3, l3 = vals
    lm = jnp.maximum(jnp.maximum(l1, l2), l3)
    w1, w2, w3 = jnp.exp(l1 - lm), jnp.exp(l2 - lm), jnp.exp(l3 - lm)
    mixed_b = (w1 * o1 + w2 * o2 + w3 * o3) / (w1 + w2 + w3)
    half = MLA_HEADS * MLA_V
    acc = jnp.dot(oa_ref[...], w_ref[:half, :], preferred_element_type=F32)
    acc += jnp.dot(mixed_b.astype(BF16), w_ref[half:, :], preferred_element_type=F32)
    out_ref[...] = h_ref[...] + acc


def _even_out(h2d, oa, branches, w_out):
    t = h2d.shape[0]
    tm = ROW_TILE
    tpb = SEQ // tm
    dw = DIL_HEADS * HEAD_DIM
    full = pl.BlockSpec((tm, D_MODEL), lambda i: (i, 0))
    specs, args = [], []
    for (_, dil), pair in zip(DIL_PATTERNS, branches):
        for a in pair:
            specs.append(pl.BlockSpec((1, dil, tm // dil, dw), lambda i: (i // tpb, 0, i % tpb, 0)))
            args.append(a)
    return pl.pallas_call(
        _even_out_kernel,
        grid=(t // tm,),
        in_specs=[full, pl.BlockSpec((tm, dw), lambda i: (i, 0))] + specs
        + [_const_spec(w_out.shape)],
        out_specs=full,
        out_shape=jax.ShapeDtypeStruct((t, D_MODEL), F32),
        scratch_shapes=[pltpu.VMEM((4, dw // LANES, tm, LANES), F32)],
        compiler_params=_params(("parallel",)),
        name="even_out_proj",
    )(h2d, oa, *args, w_out)


def _tail_kernel(*refs, with_out_proj):
    if with_out_proj:
        h_ref, o_ref, wout_ref, *refs = refs
    else:
        h_ref, *refs = refs
    p_ref, gff_ref, w1_ref, w2_ref, gple_ref, wg_ref, wp_ref, out_ref = refs
    h = h_ref[...]
    if with_out_proj:
        h = h + jnp.dot(o_ref[...], wout_ref[...], preferred_element_type=F32)
    n = _rms(h, gff_ref[...]).astype(BF16)
    chunk = D_MODEL
    for c in range(D_FF // chunk):
        u = jnp.maximum(jnp.dot(n, w1_ref[:, c * chunk:(c + 1) * chunk],
                                preferred_element_type=F32), 0.0)
        h = h + jnp.dot((u * u).astype(BF16), w2_ref[c * chunk:(c + 1) * chunk, :],
                        preferred_element_type=F32)
    n = _rms(h, gple_ref[...]).astype(BF16)
    gate = jax.nn.sigmoid(jnp.dot(n, wg_ref[...], preferred_element_type=F32))
    proj = jnp.dot(p_ref[0].astype(BF16), wp_ref[...], preferred_element_type=F32)
    out_ref[...] = h + gate * proj


def _tail(h2d, p3d, layer, gff, w1, w2, gple, wg, wp, mix=None, w_out=None):
    t = h2d.shape[0]
    tm = ROW_TILE
    full = pl.BlockSpec((tm, D_MODEL), lambda i: (i, 0))
    with_out = mix is not None
    head_specs = [full, full, _const_spec(w_out.shape, single=True)] if with_out else [full]
    head_args = (h2d, mix, w_out) if with_out else (h2d,)
    consts = (gff, w1, w2, gple, wg, wp)
    return pl.pallas_call(
        functools.partial(_tail_kernel, with_out_proj=with_out),
        grid=(t // tm,),
        in_specs=head_specs + [pl.BlockSpec((1, tm, PLE_DIM), lambda i: (layer, i, 0))]
        + [_const_spec(c.shape, single=True) for c in consts],
        out_specs=full,
        out_shape=jax.ShapeDtypeStruct((t, D_MODEL), F32),
        compiler_params=_params(("parallel",)),
        name="tail_with_out_proj" if with_out else "tail",
    )(*head_args, p3d, *consts)


def _odd_proj_kernel(x_ref, gmix_ref, win_ref, gq_ref, gk_ref, gm64_ref,
                     qt_ref, k_ref, vt_ref, km_ref):
    hn = _rms(x_ref[...], gmix_ref[...]).astype(BF16)
    z = jnp.dot(hn, win_ref[...], preferred_element_type=F32)
    dw = MOBA_HEADS * HEAD_DIM
    gq = gq_ref[...]
    gk = gk_ref[...]
    tm = x_ref.shape[0]
    for c in range(dw // MXU_DIM):
        lo = c * MXU_DIM
        qt_ref[0, lo:lo + MXU_DIM, :] = (
            _group_rms(z[:, lo:lo + MXU_DIM], gm64_ref[...]) * gq[:, lo:lo + MXU_DIM]
        ).T.astype(BF16)
        k = _group_rms(z[:, dw + lo:dw + lo + MXU_DIM], gm64_ref[...]) * gk[:, lo:lo + MXU_DIM]
        k_ref[:, lo:lo + MXU_DIM] = k.astype(BF16)
        for r in range(tm // MOBA_BLOCK):
            km_ref[r, :, lo:lo + MXU_DIM] = jnp.mean(
                k[r * MOBA_BLOCK:(r + 1) * MOBA_BLOCK], axis=0, keepdims=True)
    vt_ref[0] = z[:, 2 * dw:].T.astype(BF16)


def _odd_proj(h2d, gmix, win, gq, gk, gm64):
    t = h2d.shape[0]
    tm = ROW_TILE
    dw = MOBA_HEADS * HEAD_DIM
    consts = (gmix, win, gq, gk, gm64)
    full = pl.BlockSpec((tm, dw), lambda i: (i, 0))
    tpb = SEQ // tm
    tspec = pl.BlockSpec((1, dw, tm), lambda i: (i // tpb, 0, i % tpb))
    return pl.pallas_call(
        _odd_proj_kernel,
        grid=(t // tm,),
        in_specs=[pl.BlockSpec((tm, D_MODEL), lambda i: (i, 0))] + [_const_spec(c.shape) for c in consts],
        out_specs=[tspec, full, tspec,
                   pl.BlockSpec((tm // MOBA_BLOCK, 1, dw), lambda i: (i, 0, 0))],
        out_shape=[jax.ShapeDtypeStruct((BATCH, dw, SEQ), BF16), jax.ShapeDtypeStruct((t, dw), BF16),
                   jax.ShapeDtypeStruct((BATCH, dw, SEQ), BF16),
                   jax.ShapeDtypeStruct((t // MOBA_BLOCK, 1, dw), F32)],
        compiler_params=_params(("parallel",)),
        name="odd_proj",
    )(h2d, *consts)


def _moba_kernel(qt_ref, k_ref, vt_ref, km_ref, sl_ref, kb_ref, o_ref, *bufs):
    tq, tk = ATT_TILE_Q, MOBA_BLOCK
    nb = SEQ // MOBA_BLOCK
    visible = (lax.broadcasted_iota(jnp.int32, (tk, tq), 0)
               <= lax.broadcasted_iota(jnp.int32, (tk, tq), 1))
    visible_half = (lax.broadcasted_iota(jnp.int32, (tk, tk), 0)
                    <= lax.broadcasted_iota(jnp.int32, (tk, tk), 1))
    row = lax.broadcasted_iota(jnp.int32, (LANES, tq), 0)
    blk_id = lax.broadcasted_iota(jnp.int32, (nb, tq), 0)
    half = lax.broadcasted_iota(jnp.int32, (1, tq), 1) // tk
    kbias = kb_ref[0]
    slopes = [sl_ref[0, :, hh * HEAD_DIM:hh * HEAD_DIM + 1] for hh in range(2)]
    km_terms, resid = [], km_ref[0]
    for _ in range(KM_PARTS):
        km_terms.append(resid.astype(BF16))
        resid = resid - km_terms[-1].astype(F32)
    km_cat = jnp.concatenate(km_terms, axis=0)
    km_cat = jnp.concatenate([km_cat, jnp.zeros_like(km_cat)], axis=1)
    empty_half = (jnp.full((1, tk), MAX_FLOOR, F32), jnp.zeros((1, tk), F32),
                  jnp.zeros((HEAD_DIM, tk), F32))

    def prepare(i):
        own = 2 * i + half
        q_t = qt_ref[0, :, i * tq:(i + 1) * tq]
        heads = []
        for hh in range(2):
            qt = jnp.where((row >= HEAD_DIM * hh) & (row < HEAD_DIM * (hh + 1)), q_t,
                           jnp.zeros_like(q_t))
            ones = jnp.where((row >= KB_PARTS * hh) & (row < KB_PARTS * (hh + 1)), 1.0, 0.0)
            qt = jnp.concatenate([qt, ones.astype(BF16)], axis=0)
            gate_terms = jnp.dot(km_cat, qt, preferred_element_type=F32)
            gate = sum(gate_terms[n * nb:(n + 1) * nb] for n in range(KM_PARTS))
            g = jnp.where(blk_id < own, gate, -jnp.inf)
            sel = []
            for _ in range(MOBA_TOPK):
                mx = jnp.max(g, axis=0, keepdims=True)
                idx = jnp.min(jnp.where(g == mx, blk_id, nb), axis=0, keepdims=True)
                sel.append(jnp.where(mx > -jnp.inf, idx, -1))
                g = jnp.where(blk_id == idx, -jnp.inf, g)
            heads.append((qt, sel))
        return heads, own

    def scores(tile, j, nk, slot, causal=False, right_half=False):
        heads, _ = tile
        kt = jnp.concatenate([k_ref[0, j * tk:(j + nk) * tk, :],
                              jnp.concatenate([kbias] * nk, axis=0)], axis=1)
        cms = []
        for hh in range(2):
            qt = heads[hh][0]
            s = jnp.dot(kt, qt[:, tk:] if right_half else qt,
                        preferred_element_type=F32)
            if causal:
                s = jnp.where(visible_half if right_half else visible, s, -jnp.inf)
            if right_half:
                slot[hh, :tk, :tk] = s
            else:
                slot[hh, :nk * tk] = s
            cms.append(tuple(jnp.max(s[b * tk:(b + 1) * tk], axis=0, keepdims=True)
                             for b in range(nk)))
        return tuple(cms)

    def start_right_half(i, j, slot, cms):
        vt = vt_ref[0, :, j * tk:(j + 1) * tk]
        ps = [jnp.exp2(slot[hh, :tk, :tk] - cms[hh][0]) for hh in range(2)]
        new = []
        for hh in range(2):
            acc = jnp.dot(vt[hh * HEAD_DIM:(hh + 1) * HEAD_DIM, :], ps[hh].astype(BF16),
                          preferred_element_type=F32)
            off = slopes[hh] * float((2 * i - j) * tk)
            seen = (cms[hh][0] - off, jnp.sum(ps[hh], axis=0, keepdims=True), acc)
            new.append(tuple(jnp.concatenate([e, s], axis=1) for e, s in zip(empty_half, seen)))
        return tuple(new)

    def consume(i, tile, j, nk, slot, cms, carry):
        heads, own = tile
        vt = vt_ref[0, :, j * tk:(j + nk) * tk]
        ps, stats = [], []
        for hh in range(2):
            sel = heads[hh][1]
            m, l, _ = carry[hh]
            blocks, m_new = [], m
            for b in range(nk):
                jb = j + b
                seen = (sel[0] == jb) | (sel[1] == jb) | (sel[2] == jb) | (own == jb)
                off = slopes[hh] * float((2 * i - jb) * tk)
                m_new = jnp.where(seen, jnp.maximum(m_new, cms[hh][b] - off), m_new)
                blocks.append((seen, off))
            alpha = jnp.exp2(m - m_new)
            p = jnp.concatenate(
                [jnp.exp2(slot[hh, b * tk:(b + 1) * tk] - jnp.where(seen, m_new + off, jnp.inf))
                 for b, (seen, off) in enumerate(blocks)], axis=0)
            stats.append((m_new, alpha * l + jnp.sum(p, axis=0, keepdims=True), alpha))
            ps.append(p.astype(BF16))
        new = []
        for hh in range(2):
            m_new, l, alpha = stats[hh]
            acc = alpha * carry[hh][2] + jnp.dot(vt[hh * HEAD_DIM:(hh + 1) * HEAD_DIM, :], ps[hh],
                                                 preferred_element_type=F32)
            new.append((m_new, l, acc))
        return tuple(new)

    def finish(i, tile, last, carry):
        carry = consume(i, tile, *last, carry)
        o_t = jnp.concatenate([acc / l for (_, l, acc) in carry], axis=0)
        o_ref[i * tq:(i + 1) * tq, :] = o_t.T.astype(BF16)

    pending = None
    for i in range(SEQ // tq):
        s_a, s_b = bufs[2 * (i % 2)], bufs[2 * (i % 2) + 1]
        tile = prepare(i)
        cms_a = scores(tile, 2 * i + 1, 1, s_a, causal=True, right_half=True)
        last = (2 * i, 1, s_b, scores(tile, 2 * i, 1, s_b, causal=True))
        if pending is not None:
            finish(*pending)
        carry = start_right_half(i, 2 * i + 1, s_a, cms_a)

        for n in range(i):
            slot = (s_a, s_b)[n % 2]
            cms = scores(tile, 2 * n, 2, slot)
            carry = consume(i, tile, *last, carry)
            last = (2 * n, 2, slot, cms)
        pending = (i, tile, last, carry)
    finish(*pending)


def _moba_attention(qt3, k, vt3, km, slopes, kbias):
    t = k.shape[0]
    nb = SEQ // MOBA_BLOCK
    dw = MOBA_HEADS * HEAD_DIM
    k3 = k.reshape(BATCH, SEQ, dw)
    km3 = km.reshape(BATCH, nb, dw)
    tspec = pl.BlockSpec((1, LANES, SEQ), lambda b, c: (b, c, 0))
    return pl.pallas_call(
        _moba_kernel,
        grid=(BATCH, MOBA_HEADS // 2),
        in_specs=[tspec,
                  pl.BlockSpec((1, SEQ, LANES), lambda b, c: (b, 0, c)),
                  tspec,
                  pl.BlockSpec((1, nb, LANES), lambda b, c: (b, 0, c)),
                  pl.BlockSpec((1, 1, LANES), lambda b, c: (c, 0, 0)),
                  pl.BlockSpec((1, MOBA_BLOCK, LANES), lambda b, c: (c, 0, 0))],
        out_specs=pl.BlockSpec((SEQ, LANES), lambda b, c: (b, c)),
        out_shape=jax.ShapeDtypeStruct((t, dw), BF16),
        scratch_shapes=[pltpu.VMEM((2, 2 * MOBA_BLOCK, ATT_TILE_Q), F32)] * 4,
        compiler_params=_params(("parallel", "parallel")),
        name="moba_attention",
    )(qt3, k3, vt3, km3, slopes, kbias)


def _group_matrix(sizes, width):
    g = np.zeros((width, width), np.float32)
    lo = 0
    for n in sizes:
        g[lo:lo + n, lo:lo + n] = 1.0 / n
        lo += n
    return jnp.asarray(g, BF16)


def _alibi_slopes_log2(n_heads):
    slopes = 2.0 ** (-8.0 * np.arange(1, n_heads + 1, dtype=np.float32) / n_heads)
    return slopes.astype(np.float64) * LOG2E


def _pair_slopes(n_heads):
    tab = np.repeat(_alibi_slopes_log2(n_heads).astype(np.float32), HEAD_DIM)
    return jnp.asarray(tab.reshape(n_heads // 2, 1, LANES))


def _pair_key_bias(n_heads, block):
    bias = _alibi_slopes_log2(n_heads)[:, None] * np.arange(block, dtype=np.float64)[None, :]
    tab = np.zeros((n_heads // 2, block, LANES), np.float32)
    for part in range(KB_PARTS):
        term = bias.astype(jnp.bfloat16).astype(np.float64)
        bias = bias - term
        for h in range(n_heads):
            tab[h // 2, :, KB_PARTS * (h % 2) + part] = term[h]
    return jnp.asarray(tab, BF16)


def _even_weights(w_in, w_uq, w_ukv, qn_nope, qn_rope, kn_nope, kn_rope, dil_qn, dil_kn):
    o1 = Q_LORA
    o2 = o1 + KV_LORA
    o3 = o2 + MLA_ROPE
    kr = w_in[:, o2:o3]
    pad64 = jnp.zeros((D_MODEL, LANES - 2 * MLA_ROPE), F32)
    win = jnp.concatenate([w_in[:, :o2], kr, kr, pad64, w_in[:, o3:]], axis=1).astype(BF16)

    qd = MLA_NOPE + MLA_ROPE
    a_scale = qd ** -0.5
    zq = jnp.zeros((Q_LORA, LANES - 2 * MLA_ROPE), F32)
    zg = jnp.zeros((LANES - 2 * MLA_ROPE,), F32)
    cols, gains = [], []
    for c in range(MLA_HEADS // 2):
        h0, h1 = 2 * c, 2 * c + 1
        cols += [w_uq[:, h0 * qd:h0 * qd + MLA_NOPE], w_uq[:, h1 * qd:h1 * qd + MLA_NOPE],
                 w_uq[:, h0 * qd + MLA_NOPE:(h0 + 1) * qd], w_uq[:, h1 * qd + MLA_NOPE:(h1 + 1) * qd], zq]
        gains += [qn_nope, qn_nope, qn_rope, qn_rope, zg]
    wuq = jnp.concatenate(cols, axis=1).astype(BF16)
    gq = (jnp.concatenate(gains) * (a_scale * LOG2E))[None, :]

    kvd = MLA_NOPE + MLA_V
    kcols = [w_ukv[:, h * kvd:h * kvd + MLA_NOPE] for h in range(MLA_HEADS)]
    vcols = [w_ukv[:, h * kvd + MLA_NOPE:(h + 1) * kvd] for h in range(MLA_HEADS)]
    wukv = jnp.concatenate(kcols + vcols, axis=1).astype(BF16)
    gkn = jnp.tile(kn_nope, MLA_HEADS)[None, :]
    gkr = jnp.concatenate([kn_rope, kn_rope, zg])[None, :]
    gqb = (jnp.tile(dil_qn, DIL_HEADS) * (HEAD_DIM ** -0.5 * LOG2E))[None, :]
    gkb = jnp.tile(dil_kn, DIL_HEADS)[None, :]
    return win, wuq, wukv, gq, gkn, gkr, gqb, gkb


def kernel(x, p, positions, e_w_in, e_cq_norm, e_ckv_norm, e_w_uq, e_w_ukv, e_qn_nope, e_qn_rope, e_kn_nope, e_kn_rope, e_dil_qn, e_dil_kn, e_w_out, o_w_in, o_qn, o_kn, o_w_out, mix_norm, ff_norm, w_ff1, w_ff2, ple_norm, w_ple_gate, w_ple_proj):
    t = BATCH * SEQ
    h = x.reshape(t, D_MODEL)
    pos2d = positions.reshape(t, 1)
    p3d = p.reshape(DEPTH, t, PLE_DIM)

    gmq = _group_matrix((MLA_NOPE, MLA_NOPE, MLA_ROPE, MLA_ROPE, LANES - 2 * MLA_ROPE), MXU_DIM)
    gm64 = _group_matrix((HEAD_DIM,) * (MXU_DIM // HEAD_DIM), MXU_DIM)
    gm32 = _group_matrix((MLA_ROPE,) * (LANES // MLA_ROPE), LANES)
    half = MLA_ROPE // 2
    invf = ROPE_THETA ** (-(jnp.arange(LANES) % half).astype(F32) / half)
    invf = invf[None, :]

    for i in range(DEPTH):
        j = i // 2
        gmix = mix_norm[i][None, :]
        if i % 2 == 0:
            win, wuq, wukv, gq, gkn, gkr, gqb, gkb = _even_weights(
                e_w_in[j], e_w_uq[j], e_w_ukv[j], e_qn_nope[j], e_qn_rope[j], e_kn_nope[j],
                e_kn_rope[j], e_dil_qn[j], e_dil_kn[j])
            qa, ka, vat, *qkv_b = _even_proj(
                h, pos2d, gmix, win, e_cq_norm[j][None, :], e_ckv_norm[j][None, :], wuq, wukv,
                gq, gkn, gkr, gqb, gkb, gmq, gm64, gm32, invf)
            oa = _mla_attention(qa, ka, vat)
            branches = [_dilated_branch(*qkv_b[3 * n:3 * n + 3], _dil_bias_table(d, w), d)
                        for n, (w, d) in enumerate(DIL_PATTERNS)]
            h = _even_out(h, oa, branches, e_w_out[j].astype(BF16))
            mix = {}
        else:
            gq = (jnp.tile(o_qn[j], MOBA_HEADS) * (HEAD_DIM ** -0.5 * LOG2E))[None, :]
            gk = jnp.tile(o_kn[j], MOBA_HEADS)[None, :]
            q, k, vt, km = _odd_proj(h, gmix, o_w_in[j].astype(BF16), gq, gk, gm64)
            o = _moba_attention(q, k, vt, km, _pair_slopes(MOBA_HEADS),
                                _pair_key_bias(MOBA_HEADS, MOBA_BLOCK))
            mix = {"mix": o, "w_out": o_w_out[j].astype(BF16)}
        h = _tail(h, p3d, i, ff_norm[i][None, :], w_ff1[i].astype(BF16), w_ff2[i].astype(BF16),
                  ple_norm[i][None, :], w_ple_gate[i].astype(BF16), w_ple_proj[i].astype(BF16),
                  **mix)
    return h.reshape(BATCH, SEQ, D_MODEL)
```

```python
import functools

import numpy as np
import jax
import jax.numpy as jnp
from jax import lax
from jax.experimental import pallas as pl
from jax.experimental.pallas import tpu as pltpu

D_MODEL = 1024
BATCH = 8
SEQ = 4096
DEPTH = 2
HEAD_DIM = 64
EPS = 1e-6
MLA_HEADS = 8
MLA_NOPE = 64
MLA_ROPE = 32
MLA_V = 64
Q_LORA = 384
KV_LORA = 256
ROPE_THETA = 10000.0
DIL_HEADS = 8
DIL_PATTERNS = ((128, 1), (512, 4), (2048, 16))
DIL_BLOCK = 128
MOBA_HEADS = 16
MOBA_BLOCK = 256
MOBA_TOPK = 3
D_FF = 4 * D_MODEL
PLE_DIM = 256

LANES = 128
MXU_DIM = 256
VMEM_LIMIT = 56 * 1024 * 1024
ROW_TILE = 512
FFN_CHUNK = 2048
MLA_QK_WIDTH = (MLA_HEADS // 2) * MXU_DIM
ATT_TILE = 256
ATT_TILE_Q = 512
MAX_FLOOR = -1e30
KB_PARTS = 3
KM_PARTS = 3
LOG2E = float(np.log2(np.e))
LN2 = float(np.log(2.0))

F32 = jnp.float32
BF16 = jnp.bfloat16


def _const_spec(shape, single=False):
    nd = len(shape)
    mode = {"pipeline_mode": pl.Buffered(1)} if single else {}
    return pl.BlockSpec(shape, lambda *_: (0,) * nd, **mode)


def _params(sem):
    return pltpu.CompilerParams(dimension_semantics=sem, vmem_limit_bytes=VMEM_LIMIT)


def _rms(x, g):
    return x * lax.rsqrt(jnp.mean(x * x, axis=-1, keepdims=True) + EPS) * g


def _group_rms(x, gmat):
    ms = jnp.dot((x * x).astype(BF16), gmat, preferred_element_type=F32)
    return x * lax.rsqrt(ms + EPS)


def _even_proj_kernel(x_ref, pos_ref, gmix_ref, win_ref, gcq_ref, gckv_ref, wuq_ref, wukv_ref,
                      gq_ref, gkn_ref, gkr_ref, gqb_ref, gkb_ref, gmq_ref, gm64_ref, gm32_ref,
                      invf_ref, qat_ref, ka_ref, vat_ref, *rest):
    dil_refs, stage_ref = rest[:-1], rest[-1]
    hn = _rms(x_ref[...], gmix_ref[...]).astype(BF16)
    z = jnp.dot(hn, win_ref[...], preferred_element_type=F32)
    o1 = Q_LORA
    o2 = o1 + KV_LORA
    o3 = o2 + LANES
    dw = DIL_HEADS * HEAD_DIM
    c_q = _rms(z[:, :o1], gcq_ref[...]).astype(BF16)
    c_kv = _rms(z[:, o1:o2], gckv_ref[...]).astype(BF16)

    ang = pos_ref[...].astype(F32) * invf_ref[...]
    cos = jnp.cos(ang)
    sin = jnp.sin(ang)
    lane = lax.broadcasted_iota(jnp.int32, (1, LANES), 1)
    first_half = (lane % MLA_ROPE) < (MLA_ROPE // 2)

    def rope(xr):
        rot = jnp.where(first_half, -pltpu.roll(xr, LANES - MLA_ROPE // 2, 1),
                        pltpu.roll(xr, MLA_ROPE // 2, 1))
        return xr * cos + rot * sin

    q = jnp.dot(c_q, wuq_ref[...], preferred_element_type=F32)
    gq = gq_ref[...]
    for c in range(MLA_HEADS // 2):
        lo = c * MXU_DIM
        qc = _group_rms(q[:, lo:lo + MXU_DIM], gmq_ref[...]) * gq[:, lo:lo + MXU_DIM]
        qat_ref[0, lo:lo + LANES, :] = qc[:, :LANES].T.astype(BF16)
        qat_ref[0, lo + LANES:lo + MXU_DIM, :] = rope(qc[:, LANES:]).T.astype(BF16)

    kv = jnp.dot(c_kv, wukv_ref[...], preferred_element_type=F32)
    k_rope = rope(_group_rms(z[:, o2:o3], gm32_ref[...]) * gkr_ref[...]).astype(BF16)
    gkn = gkn_ref[...]
    for c in range(MLA_HEADS // 2):
        lo = c * LANES
        if c % 2 == 0:
            kn2 = _group_rms(kv[:, lo:lo + MXU_DIM], gm64_ref[...]) * gkn[:, lo:lo + MXU_DIM]
        kn = kn2[:, (c % 2) * LANES:(c % 2 + 1) * LANES]
        ka_ref[:, c * MXU_DIM:c * MXU_DIM + LANES] = kn.astype(BF16)
        ka_ref[:, c * MXU_DIM + LANES:(c + 1) * MXU_DIM] = k_rope
    vat_ref[0] = kv[:, dw:].T.astype(BF16)

    gqb = gqb_ref[...]
    gkb = gkb_ref[...]
    qkv = []
    for c in range(dw // MXU_DIM):
        lo = c * MXU_DIM
        qkv.append((
            _group_rms(z[:, o3 + lo:o3 + lo + MXU_DIM], gm64_ref[...]) * gqb[:, lo:lo + MXU_DIM],
            _group_rms(z[:, o3 + dw + lo:o3 + dw + lo + MXU_DIM], gm64_ref[...])
            * gkb[:, lo:lo + MXU_DIM],
            z[:, o3 + 2 * dw + lo:o3 + 2 * dw + lo + MXU_DIM]))
    tm = x_ref.shape[0]
    for a in range(3):
        for c in range(dw // LANES):
            half = (c % 2) * LANES
            stage_ref[a, c] = qkv[c // 2][a][:, half:half + LANES]
            for (_, dil), out_ref in zip(DIL_PATTERNS, dil_refs[a::3]):
                for r in range(dil):
                    out_ref[0, r, :, c * LANES:(c + 1) * LANES] = stage_ref[
                        a, c, pl.ds(r, tm // dil, stride=dil), :].astype(BF16)


def _even_proj(h2d, pos2d, gmix, win, gcq, gckv, wuq, wukv, gq, gkn, gkr, gqb, gkb,
               gmq, gm64, gm32, invf):
    t = h2d.shape[0]
    tm = ROW_TILE
    consts = (gmix, win, gcq, gckv, wuq, wukv, gq, gkn, gkr, gqb, gkb, gmq, gm64, gm32, invf)
    tpb = SEQ // tm
    dw = DIL_HEADS * HEAD_DIM
    out_specs = [pl.BlockSpec((1, MLA_QK_WIDTH, tm), lambda i: (i // tpb, 0, i % tpb)),
                 pl.BlockSpec((tm, MLA_QK_WIDTH), lambda i: (i, 0)),
                 pl.BlockSpec((1, MLA_HEADS * MLA_V, tm), lambda i: (i // tpb, 0, i % tpb))]
    out_shape = [jax.ShapeDtypeStruct((BATCH, MLA_QK_WIDTH, SEQ), BF16),
                 jax.ShapeDtypeStruct((t, MLA_QK_WIDTH), BF16),
                 jax.ShapeDtypeStruct((BATCH, MLA_HEADS * MLA_V, SEQ), BF16)]
    for (_, dil) in DIL_PATTERNS:
        for _ in range(3):
            out_specs.append(pl.BlockSpec((1, dil, tm // dil, dw),
                                          lambda i: (i // tpb, 0, i % tpb, 0)))
            out_shape.append(jax.ShapeDtypeStruct((BATCH, dil, SEQ // dil, dw), BF16))
    return pl.pallas_call(
        _even_proj_kernel,
        grid=(t // tm,),
        in_specs=[pl.BlockSpec((tm, D_MODEL), lambda i: (i, 0)),
                  pl.BlockSpec((tm, 1), lambda i: (i, 0))] + [_const_spec(c.shape) for c in consts],
        out_specs=out_specs,
        out_shape=out_shape,
        scratch_shapes=[pltpu.VMEM((3, dw // LANES, tm, LANES), F32)],
        compiler_params=_params(("parallel",)),
        name="even_proj",
    )(h2d, pos2d, *consts)


def _mla_kernel(qt_ref, k_ref, vt_ref, o_ref, *bufs):
    tq, tk = ATT_TILE_Q, ATT_TILE
    feat = lax.broadcasted_iota(jnp.int32, (MXU_DIM, 1), 0)
    visible = (lax.broadcasted_iota(jnp.int32, (tk, tq), 0)
               <= lax.broadcasted_iota(jnp.int32, (tk, tq), 1))
    visible_half = (lax.broadcasted_iota(jnp.int32, (tk, tk), 0)
                    <= lax.broadcasted_iota(jnp.int32, (tk, tk), 1))
    head_rows = []
    for hh in range(2):
        nope = (feat >= MLA_NOPE * hh) & (feat < MLA_NOPE * (hh + 1))
        rope = (feat >= LANES + MLA_ROPE * hh) & (feat < LANES + MLA_ROPE * (hh + 1))
        head_rows.append(nope | rope)
    empty_half = (jnp.full((1, tk), -jnp.inf, F32), jnp.zeros((1, tk), F32),
                  jnp.zeros((MLA_V, tk), F32))

    def consume(j, nk, slot, cms, carry):
        vt = vt_ref[0, :, j * tk:(j + nk) * tk]
        ps, stats = [], []
        for hh in range(2):
            m, l, _ = carry[hh]
            m_new = jnp.maximum(m, cms[hh])
            alpha = jnp.exp2(m - m_new)
            p = jnp.exp2(slot[hh, :nk * tk] - m_new)
            stats.append((m_new, alpha * l + jnp.sum(p, axis=0, keepdims=True), alpha))
            ps.append(p.astype(BF16))
        new = []
        for hh in range(2):
            m_new, l, alpha = stats[hh]
            acc = alpha * carry[hh][2] + jnp.dot(vt[hh * MLA_V:(hh + 1) * MLA_V, :], ps[hh],
                                                 preferred_element_type=F32)
            new.append((m_new, l, acc))
        return tuple(new)

    def start_right_half(j, slot, cms):
        vt = vt_ref[0, :, j * tk:(j + 1) * tk]
        ps = [jnp.exp2(slot[hh, :tk, :tk] - cms[hh]) for hh in range(2)]
        new = []
        for hh in range(2):
            acc = jnp.dot(vt[hh * MLA_V:(hh + 1) * MLA_V, :], ps[hh].astype(BF16),
                          preferred_element_type=F32)
            seen = (cms[hh], jnp.sum(ps[hh], axis=0, keepdims=True), acc)
            new.append(tuple(jnp.concatenate([e, s], axis=1) for e, s in zip(empty_half, seen)))
        return tuple(new)

    def finish(i, last, carry):
        carry = consume(*last, carry)
        o_t = jnp.concatenate([acc / l for (_, l, acc) in carry], axis=0)
        o_ref[i * tq:(i + 1) * tq, :] = o_t.T.astype(BF16)

    pending = None
    for i in range(SEQ // tq):
        s_a, s_b = bufs[2 * (i % 2)], bufs[2 * (i % 2) + 1]
        qt = qt_ref[0, :, i * tq:(i + 1) * tq]
        qts = [jnp.where(rows, qt, jnp.zeros_like(qt)) for rows in head_rows]

        def scores(j, nk, slot, qts=qts, causal=False, right_half=False):
            kt = k_ref[0, j * tk:(j + nk) * tk, :]
            cms = []
            for hh in range(2):
                s = jnp.dot(kt, qts[hh][:, tk:] if right_half else qts[hh],
                            preferred_element_type=F32)
                if causal:
                    s = jnp.where(visible_half if right_half else visible, s, -jnp.inf)
                if right_half:
                    slot[hh, :tk, :tk] = s
                else:
                    slot[hh, :nk * tk] = s
                cms.append(jnp.max(s, axis=0, keepdims=True))
            return tuple(cms)

        cms_a = scores(2 * i + 1, 1, s_a, causal=True, right_half=True)
        last = (2 * i, 1, s_b, scores(2 * i, 1, s_b, causal=True))
        if pending is not None:
            finish(*pending)
        carry = start_right_half(2 * i + 1, s_a, cms_a)

        for n in range(i):
            slot = (s_a, s_b)[n % 2]
            cms = scores(2 * n, 2, slot)
            carry = consume(*last, carry)
            last = (2 * n, 2, slot, cms)
        pending = (i, last, carry)
    finish(*pending)


def _mla_attention(qat, ka, vat):
    t = ka.shape[0]
    ka3 = ka.reshape(BATCH, SEQ, ka.shape[1])
    return pl.pallas_call(
        _mla_kernel,
        grid=(BATCH, MLA_HEADS // 2),
        in_specs=[pl.BlockSpec((1, MXU_DIM, SEQ), lambda b, c: (b, c, 0)),
                  pl.BlockSpec((1, SEQ, MXU_DIM), lambda b, c: (b, 0, c)),
                  pl.BlockSpec((1, LANES, SEQ), lambda b, c: (b, c, 0))],
        out_specs=pl.BlockSpec((SEQ, LANES), lambda b, c: (b, c)),
        out_shape=jax.ShapeDtypeStruct((t, MLA_HEADS * MLA_V), BF16),
        scratch_shapes=[pltpu.VMEM((2, 2 * ATT_TILE, ATT_TILE_Q), F32)] * 4,
        compiler_params=_params(("parallel", "parallel")),
        name="mla_attention",
    )(qat, ka3, vat)


DIL_QBLK = 2 * DIL_BLOCK
DIL_HALO = DIL_BLOCK
DIL_GROUP = 16


def _dil_kernel(q_ref, k_ref, v_ref, mb_ref, o_ref, lse_ref, vt_scr, *, nblk):
    qb, halo = DIL_QBLK, DIL_HALO
    ncls = q_ref.shape[1]
    lane = lax.broadcasted_iota(jnp.int32, (1, LANES), 1)
    low = lane < HEAD_DIM
    for g in range(ncls):
        vt_scr[g] = v_ref[0, g].astype(F32).T.astype(BF16)

    def blocks(items):
        work = []
        for g, r0, first in items:
            q = q_ref[0, g, pl.ds(r0, qb), :].astype(F32)
            if first:
                k0, nk = 0, qb
            else:
                k0 = r0 - halo if isinstance(r0, int) else pl.multiple_of(r0 - halo, halo)
                nk = qb + halo
            k = k_ref[0, g, pl.ds(k0, nk), :]
            for hh in range(2):
                qt = jnp.where(low if hh == 0 else ~low, q, 0.0).T.astype(BF16)
                s = jnp.dot(k, qt, preferred_element_type=F32) - (
                    mb_ref[0, hh, halo:, :] if first else mb_ref[0, hh])
                work.append((g, r0, k0, nk, hh, s))
        outs = []
        for g, r0, k0, nk, hh, s in work:
            m = jnp.max(s, axis=0, keepdims=True)
            p = jnp.exp2(s - m)
            den = jnp.sum(p, axis=0, keepdims=True)
            vt = vt_scr[g, hh * HEAD_DIM:(hh + 1) * HEAD_DIM, pl.ds(k0, nk)]
            o = jnp.dot(vt, p.astype(BF16), preferred_element_type=F32) / den
            lse = (m + jnp.log2(den)) * LN2
            outs.append((o, jnp.broadcast_to(lse, (HEAD_DIM, qb))))
        for n, (g, r0, _) in enumerate(items):
            pair = outs[2 * n:2 * n + 2]
            o_ref[0, g, pl.ds(r0, qb), :] = jnp.concatenate(
                [o for o, _ in pair], axis=0).T.astype(o_ref.dtype)
            lse_ref[0, g, pl.ds(r0, qb), :] = jnp.concatenate([l for _, l in pair], axis=0).T

    per = DIL_GROUP // ncls
    blocks([(g, n * qb, n == 0) for g in range(ncls) for n in range(per)])

    def body(n, carry):
        r0 = pl.multiple_of(per * n * qb, per * qb)
        blocks([(g, r0 + c * qb if c == 0 else pl.multiple_of(r0 + c * qb, qb), False)
                for g in range(ncls) for c in range(per)])
        return carry

    lax.fori_loop(1, nblk // per, body, 0)


def _dilated_branch(qc, kc, vc, mb, dil):
    l = SEQ // dil
    dw = DIL_HEADS * HEAD_DIM
    nblk = l // DIL_QBLK
    ncls = max(1, DIL_GROUP // nblk)
    assert dil % ncls == 0 and nblk % (DIL_GROUP // ncls) == 0
    blk = pl.BlockSpec((1, ncls, l, LANES), lambda c, b, r: (b, r, 0, c))
    return pl.pallas_call(
        functools.partial(_dil_kernel, nblk=nblk),
        grid=(DIL_HEADS // 2, BATCH, dil // ncls),
        in_specs=[blk, blk, blk,
                  pl.BlockSpec((1, 2, DIL_QBLK + DIL_HALO, DIL_QBLK), lambda c, b, r: (c, 0, 0, 0))],
        out_specs=[blk, blk],
        out_shape=[jax.ShapeDtypeStruct((BATCH, dil, l, dw), BF16),
                   jax.ShapeDtypeStruct((BATCH, dil, l, dw), F32)],
        scratch_shapes=[pltpu.VMEM((ncls, LANES, l), BF16)],
        compiler_params=_params(("parallel", "parallel", "parallel")),
        name=f"dilated_d{dil}",
    )(qc, kc, vc, mb)


def _dil_bias_table(dil, window):
    reach = window // dil
    ki = np.arange(DIL_QBLK + DIL_HALO)[:, None]
    qi = np.arange(DIL_QBLK)[None, :]
    step = DIL_HALO + qi - ki
    valid = (step >= 0) & (step <= reach)
    slopes = 2.0 ** (-8.0 * np.arange(1, DIL_HEADS + 1, dtype=np.float32) / DIL_HEADS)
    bias = slopes.astype(np.float32)[:, None, None] * (step * dil).astype(np.float32)[None]
    tab = np.where(valid[None], bias * LOG2E, np.inf).astype(np.float32)
    return jnp.asarray(tab.reshape(DIL_HEADS // 2, 2, DIL_QBLK + DIL_HALO, DIL_QBLK))


def _even_out_kernel(h_ref, oa_ref, *rest):
    branch_refs, w_ref, out_ref, stage_ref = rest[:-3], rest[-3], rest[-2], rest[-1]
    tm = h_ref.shape[0]
    vals = []
    for n, ((_, dil), ref) in enumerate(zip([pt for pt in DIL_PATTERNS for _ in range(2)],
                                            branch_refs)):
        if dil == 1:
            vals.append(ref[0, 0].astype(F32))
        else:
            slot = n - 2
            for r in range(dil):
                for c in range(ref.shape[-1] // LANES):
                    stage_ref[slot, c, pl.ds(r, tm // dil, stride=dil), :] = ref[
                        0, r, :, c * LANES:(c + 1) * LANES].astype(F32)
            vals.append(jnp.concatenate(
                [stage_ref[slot, c] for c in range(ref.shape[-1] // LANES)], axis=1))
    o1, l1, o2, l2, o3, l3 = vals
    lm = jnp.maximum(jnp.maximum(l1, l2), l3)
    w1, w2, w3 = jnp.exp(l1 - lm), jnp.exp(l2 - lm), jnp.exp(l3 - lm)
    mixed_b = (w1 * o1 + w2 * o2 + w3 * o3) / (w1 + w2 + w3)
    half = MLA_HEADS * MLA_V
    acc = jnp.dot(oa_ref[...], w_ref[:half, :], preferred_element_type=F32)
    acc += jnp.dot(mixed_b.astype(BF16), w_ref[half:, :], preferred_element_type=F32)
    out_ref[...] = h_ref[...] + acc


def _even_out(h2d, oa, branches, w_out):
    t = h2d.shape[0]
    tm = ROW_TILE
    tpb = SEQ // tm
    dw = DIL_HEADS * HEAD_DIM
    full = pl.BlockSpec((tm, D_MODEL), lambda i: (i, 0))
    specs, args = [], []
    for (_, dil), pair in zip(DIL_PATTERNS, branches):
        for a in pair:
            specs.append(pl.BlockSpec((1, dil, tm // dil, dw), lambda i: (i // tpb, 0, i % tpb, 0)))
            args.append(a)
    return pl.pallas_call(
        _even_out_kernel,
        grid=(t // tm,),
        in_specs=[full, pl.BlockSpec((tm, dw), lambda i: (i, 0))] + specs
        + [_const_spec(w_out.shape)],
        out_specs=full,
        out_shape=jax.ShapeDtypeStruct((t, D_MODEL), F32),
        scratch_shapes=[pltpu.VMEM((4, dw // LANES, tm, LANES), F32)],
        compiler_params=_params(("parallel",)),
        name="even_out_proj",
    )(h2d, oa, *args, w_out)


def _tail_kernel(*refs, with_out_proj):
    if with_out_proj:
        h_ref, o_ref, wout_ref, *refs = refs
    else:
        h_ref, *refs = refs
    p_ref, gff_ref, w1_ref, w2_ref, gple_ref, wg_ref, wp_ref, out_ref = refs
    h = h_ref[...]
    if with_out_proj:
        h = h + jnp.dot(o_ref[...], wout_ref[...], preferred_element_type=F32)
    n = _rms(h, gff_ref[...]).astype(BF16)
    chunk = FFN_CHUNK
    for c in range(D_FF // chunk):
        u = jnp.maximum(jnp.dot(n, w1_ref[:, c * chunk:(c + 1) * chunk],
                                preferred_element_type=F32), 0.0)
        h = h + jnp.dot((u * u).astype(BF16), w2_ref[c * chunk:(c + 1) * chunk, :],
                        preferred_element_type=F32)
    n = _rms(h, gple_ref[...]).astype(BF16)
    gate = jax.nn.sigmoid(jnp.dot(n, wg_ref[...], preferred_element_type=F32))
    proj = jnp.dot(p_ref[0].astype(BF16), wp_ref[...], preferred_element_type=F32)
    out_ref[...] = h + gate * proj


def _tail(h2d, p3d, layer, gff, w1, w2, gple, wg, wp, mix=None, w_out=None):
    t = h2d.shape[0]
    tm = ROW_TILE
    full = pl.BlockSpec((tm, D_MODEL), lambda i: (i, 0))
    with_out = mix is not None
    head_specs = [full, full, _const_spec(w_out.shape, single=True)] if with_out else [full]
    head_args = (h2d, mix, w_out) if with_out else (h2d,)
    consts = (gff, w1, w2, gple, wg, wp)
    return pl.pallas_call(
        functools.partial(_tail_kernel, with_out_proj=with_out),
        grid=(t // tm,),
        in_specs=head_specs + [pl.BlockSpec((1, tm, PLE_DIM), lambda i: (layer, i, 0))]
        + [_const_spec(c.shape, single=True) for c in consts],
        out_specs=full,
        out_shape=jax.ShapeDtypeStruct((t, D_MODEL), F32),
        compiler_params=_params(("parallel",)),
        name="tail_with_out_proj" if with_out else "tail",
    )(*head_args, p3d, *consts)


def _odd_proj_kernel(x_ref, gmix_ref, win_ref, gq_ref, gk_ref, gm64_ref,
                     qt_ref, k_ref, vt_ref, km_ref):
    hn = _rms(x_ref[...], gmix_ref[...]).astype(BF16)
    z = jnp.dot(hn, win_ref[...], preferred_element_type=F32)
    dw = MOBA_HEADS * HEAD_DIM
    gq = gq_ref[...]
    gk = gk_ref[...]
    tm = x_ref.shape[0]
    for c in range(dw // MXU_DIM):
        lo = c * MXU_DIM
        qt_ref[0, lo:lo + MXU_DIM, :] = (
            _group_rms(z[:, lo:lo + MXU_DIM], gm64_ref[...]) * gq[:, lo:lo + MXU_DIM]
        ).T.astype(BF16)
        k = _group_rms(z[:, dw + lo:dw + lo + MXU_DIM], gm64_ref[...]) * gk[:, lo:lo + MXU_DIM]
        k_ref[:, lo:lo + MXU_DIM] = k.astype(BF16)
        for r in range(tm // MOBA_BLOCK):
            km_ref[r, :, lo:lo + MXU_DIM] = jnp.mean(
                k[r * MOBA_BLOCK:(r + 1) * MOBA_BLOCK], axis=0, keepdims=True)
    vt_ref[0] = z[:, 2 * dw:].T.astype(BF16)


def _odd_proj(h2d, gmix, win, gq, gk, gm64):
    t = h2d.shape[0]
    tm = ROW_TILE
    dw = MOBA_HEADS * HEAD_DIM
    consts = (gmix, win, gq, gk, gm64)
    full = pl.BlockSpec((tm, dw), lambda i: (i, 0))
    tpb = SEQ // tm
    tspec = pl.BlockSpec((1, dw, tm), lambda i: (i // tpb, 0, i % tpb))
    return pl.pallas_call(
        _odd_proj_kernel,
        grid=(t // tm,),
        in_specs=[pl.BlockSpec((tm, D_MODEL), lambda i: (i, 0))] + [_const_spec(c.shape) for c in consts],
        out_specs=[tspec, full, tspec,
                   pl.BlockSpec((tm // MOBA_BLOCK, 1, dw), lambda i: (i, 0, 0))],
        out_shape=[jax.ShapeDtypeStruct((BATCH, dw, SEQ), BF16), jax.ShapeDtypeStruct((t, dw), BF16),
                   jax.ShapeDtypeStruct((BATCH, dw, SEQ), BF16),
                   jax.ShapeDtypeStruct((t // MOBA_BLOCK, 1, dw), F32)],
        compiler_params=_params(("parallel",)),
        name="odd_proj",
    )(h2d, *consts)


def _moba_kernel(qt_ref, k_ref, vt_ref, km_ref, sl_ref, kb_ref, o_ref, *bufs):
    tq, tk = ATT_TILE_Q, MOBA_BLOCK
    nb = SEQ // MOBA_BLOCK
    visible = (lax.broadcasted_iota(jnp.int32, (tk, tq), 0)
               <= lax.broadcasted_iota(jnp.int32, (tk, tq), 1))
    visible_half = (lax.broadcasted_iota(jnp.int32, (tk, tk), 0)
                    <= lax.broadcasted_iota(jnp.int32, (tk, tk), 1))
    row = lax.broadcasted_iota(jnp.int32, (LANES, tq), 0)
    blk_id = lax.broadcasted_iota(jnp.int32, (nb, tq), 0)
    half = lax.broadcasted_iota(jnp.int32, (1, tq), 1) // tk
    kbias = kb_ref[0]
    slopes = [sl_ref[0, :, hh * HEAD_DIM:hh * HEAD_DIM + 1] for hh in range(2)]
    km_terms, resid = [], km_ref[0]
    for _ in range(KM_PARTS):
        km_terms.append(resid.astype(BF16))
        resid = resid - km_terms[-1].astype(F32)
    km_cat = jnp.concatenate(km_terms, axis=0)
    km_cat = jnp.concatenate([km_cat, jnp.zeros_like(km_cat)], axis=1)
    empty_half = (jnp.full((1, tk), MAX_FLOOR, F32), jnp.zeros((1, tk), F32),
                  jnp.zeros((HEAD_DIM, tk), F32))

    def prepare(i):
        own = 2 * i + half
        q_t = qt_ref[0, :, i * tq:(i + 1) * tq]
        heads = []
        for hh in range(2):
            qt = jnp.where((row >= HEAD_DIM * hh) & (row < HEAD_DIM * (hh + 1)), q_t,
                           jnp.zeros_like(q_t))
            ones = jnp.where((row >= KB_PARTS * hh) & (row < KB_PARTS * (hh + 1)), 1.0, 0.0)
            qt = jnp.concatenate([qt, ones.astype(BF16)], axis=0)
            gate_terms = jnp.dot(km_cat, qt, preferred_element_type=F32)
            gate = sum(gate_terms[n * nb:(n + 1) * nb] for n in range(KM_PARTS))
            g = jnp.where(blk_id < own, gate, -jnp.inf)
            sel = []
            for _ in range(MOBA_TOPK):
                mx = jnp.max(g, axis=0, keepdims=True)
                idx = jnp.min(jnp.where(g == mx, blk_id, nb), axis=0, keepdims=True)
                sel.append(jnp.where(mx > -jnp.inf, idx, -1))
                g = jnp.where(blk_id == idx, -jnp.inf, g)
            heads.append((qt, sel))
        return heads, own

    def scores(tile, j, nk, slot, causal=False, right_half=False):
        heads, _ = tile
        kt = jnp.concatenate([k_ref[0, j * tk:(j + nk) * tk, :],
                              jnp.concatenate([kbias] * nk, axis=0)], axis=1)
        cms = []
        for hh in range(2):
            qt = heads[hh][0]
            s = jnp.dot(kt, qt[:, tk:] if right_half else qt,
                        preferred_element_type=F32)
            if causal:
                s = jnp.where(visible_half if right_half else visible, s, -jnp.inf)
            if right_half:
                slot[hh, :tk, :tk] = s
            else:
                slot[hh, :nk * tk] = s
            cms.append(tuple(jnp.max(s[b * tk:(b + 1) * tk], axis=0, keepdims=True)
                             for b in range(nk)))
        return tuple(cms)

    def start_right_half(i, j, slot, cms):
        vt = vt_ref[0, :, j * tk:(j + 1) * tk]
        ps = [jnp.exp2(slot[hh, :tk, :tk] - cms[hh][0]) for hh in range(2)]
        new = []
        for hh in range(2):
            acc = jnp.dot(vt[hh * HEAD_DIM:(hh + 1) * HEAD_DIM, :], ps[hh].astype(BF16),
                          preferred_element_type=F32)
            off = slopes[hh] * float((2 * i - j) * tk)
            seen = (cms[hh][0] - off, jnp.sum(ps[hh], axis=0, keepdims=True), acc)
            new.append(tuple(jnp.concatenate([e, s], axis=1) for e, s in zip(empty_half, seen)))
        return tuple(new)

    def consume(i, tile, j, nk, slot, cms, carry):
        heads, own = tile
        vt = vt_ref[0, :, j * tk:(j + nk) * tk]
        ps, stats = [], []
        for hh in range(2):
            sel = heads[hh][1]
            m, l, _ = carry[hh]
            blocks, m_new = [], m
            for b in range(nk):
                jb = j + b
                seen = (sel[0] == jb) | (sel[1] == jb) | (sel[2] == jb) | (own == jb)
                off = slopes[hh] * float((2 * i - jb) * tk)
                m_new = jnp.where(seen, jnp.maximum(m_new, cms[hh][b] - off), m_new)
                blocks.append((seen, off))
            alpha = jnp.exp2(m - m_new)
            p = jnp.concatenate(
                [jnp.exp2(slot[hh, b * tk:(b + 1) * tk] - jnp.where(seen, m_new + off, jnp.inf))
                 for b, (seen, off) in enumerate(blocks)], axis=0)
            stats.append((m_new, alpha * l + jnp.sum(p, axis=0, keepdims=True), alpha))
            ps.append(p.astype(BF16))
        new = []
        for hh in range(2):
            m_new, l, alpha = stats[hh]
            acc = alpha * carry[hh][2] + jnp.dot(vt[hh * HEAD_DIM:(hh + 1) * HEAD_DIM, :], ps[hh],
                                                 preferred_element_type=F32)
            new.append((m_new, l, acc))
        return tuple(new)

    def finish(i, tile, last, carry):
        carry = consume(i, tile, *last, carry)
        o_t = jnp.concatenate([acc / l for (_, l, acc) in carry], axis=0)
        o_ref[i * tq:(i + 1) * tq, :] = o_t.T.astype(BF16)

    pending = None
    for i in range(SEQ // tq):
        s_a, s_b = bufs[2 * (i % 2)], bufs[2 * (i % 2) + 1]
        tile = prepare(i)
        cms_a = scores(tile, 2 * i + 1, 1, s_a, causal=True, right_half=True)
        last = (2 * i, 1, s_b, scores(tile, 2 * i, 1, s_b, causal=True))
        if pending is not None:
            finish(*pending)
        carry = start_right_half(i, 2 * i + 1, s_a, cms_a)

        for n in range(i):
            slot = (s_a, s_b)[n % 2]
            cms = scores(tile, 2 * n, 2, slot)
            carry = consume(i, tile, *last, carry)
            last = (2 * n, 2, slot, cms)
        pending = (i, tile, last, carry)
    finish(*pending)


def _moba_attention(qt3, k, vt3, km, slopes, kbias):
    t = k.shape[0]
    nb = SEQ // MOBA_BLOCK
    dw = MOBA_HEADS * HEAD_DIM
    k3 = k.reshape(BATCH, SEQ, dw)
    km3 = km.reshape(BATCH, nb, dw)
    tspec = pl.BlockSpec((1, LANES, SEQ), lambda b, c: (b, c, 0))
    return pl.pallas_call(
        _moba_kernel,
        grid=(BATCH, MOBA_HEADS // 2),
        in_specs=[tspec,
                  pl.BlockSpec((1, SEQ, LANES), lambda b, c: (b, 0, c)),
                  tspec,
                  pl.BlockSpec((1, nb, LANES), lambda b, c: (b, 0, c)),
                  pl.BlockSpec((1, 1, LANES), lambda b, c: (c, 0, 0)),
                  pl.BlockSpec((1, MOBA_BLOCK, LANES), lambda b, c: (c, 0, 0))],
        out_specs=pl.BlockSpec((SEQ, LANES), lambda b, c: (b, c)),
        out_shape=jax.ShapeDtypeStruct((t, dw), BF16),
        scratch_shapes=[pltpu.VMEM((2, 2 * MOBA_BLOCK, ATT_TILE_Q), F32)] * 4,
        compiler_params=_params(("parallel", "parallel")),
        name="moba_attention",
    )(qt3, k3, vt3, km3, slopes, kbias)


def _group_matrix(sizes, width):
    g = np.zeros((width, width), np.float32)
    lo = 0
    for n in sizes:
        g[lo:lo + n, lo:lo + n] = 1.0 / n
        lo += n
    return jnp.asarray(g, BF16)


def _alibi_slopes_log2(n_heads):
    slopes = 2.0 ** (-8.0 * np.arange(1, n_heads + 1, dtype=np.float32) / n_heads)
    return slopes.astype(np.float64) * LOG2E


def _pair_slopes(n_heads):
    tab = np.repeat(_alibi_slopes_log2(n_heads).astype(np.float32), HEAD_DIM)
    return jnp.asarray(tab.reshape(n_heads // 2, 1, LANES))


def _pair_key_bias(n_heads, block):
    bias = _alibi_slopes_log2(n_heads)[:, None] * np.arange(block, dtype=np.float64)[None, :]
    tab = np.zeros((n_heads // 2, block, LANES), np.float32)
    for part in range(KB_PARTS):
        term = bias.astype(jnp.bfloat16).astype(np.float64)
        bias = bias - term
        for h in range(n_heads):
            tab[h // 2, :, KB_PARTS * (h % 2) + part] = term[h]
    return jnp.asarray(tab, BF16)


def _even_weights(w_in, w_uq, w_ukv, qn_nope, qn_rope, kn_nope, kn_rope, dil_qn, dil_kn):
    o1 = Q_LORA
    o2 = o1 + KV_LORA
    o3 = o2 + MLA_ROPE
    kr = w_in[:, o2:o3]
    pad64 = jnp.zeros((D_MODEL, LANES - 2 * MLA_ROPE), F32)
    win = jnp.concatenate([w_in[:, :o2], kr, kr, pad64, w_in[:, o3:]], axis=1).astype(BF16)

    qd = MLA_NOPE + MLA_ROPE
    a_scale = qd ** -0.5
    zq = jnp.zeros((Q_LORA, LANES - 2 * MLA_ROPE), F32)
    zg = jnp.zeros((LANES - 2 * MLA_ROPE,), F32)
    cols, gains = [], []
    for c in range(MLA_HEADS // 2):
        h0, h1 = 2 * c, 2 * c + 1
        cols += [w_uq[:, h0 * qd:h0 * qd + MLA_NOPE], w_uq[:, h1 * qd:h1 * qd + MLA_NOPE],
                 w_uq[:, h0 * qd + MLA_NOPE:(h0 + 1) * qd], w_uq[:, h1 * qd + MLA_NOPE:(h1 + 1) * qd], zq]
        gains += [qn_nope, qn_nope, qn_rope, qn_rope, zg]
    wuq = jnp.concatenate(cols, axis=1).astype(BF16)
    gq = (jnp.concatenate(gains) * (a_scale * LOG2E))[None, :]

    kvd = MLA_NOPE + MLA_V
    kcols = [w_ukv[:, h * kvd:h * kvd + MLA_NOPE] for h in range(MLA_HEADS)]
    vcols = [w_ukv[:, h * kvd + MLA_NOPE:(h + 1) * kvd] for h in range(MLA_HEADS)]
    wukv = jnp.concatenate(kcols + vcols, axis=1).astype(BF16)
    gkn = jnp.tile(kn_nope, MLA_HEADS)[None, :]
    gkr = jnp.concatenate([kn_rope, kn_rope, zg])[None, :]
    gqb = (jnp.tile(dil_qn, DIL_HEADS) * (HEAD_DIM ** -0.5 * LOG2E))[None, :]
    gkb = jnp.tile(dil_kn, DIL_HEADS)[None, :]
    return win, wuq, wukv, gq, gkn, gkr, gqb, gkb


def kernel(x, p, positions, e_w_in, e_cq_norm, e_ckv_norm, e_w_uq, e_w_ukv, e_qn_nope, e_qn_rope, e_kn_nope, e_kn_rope, e_dil_qn, e_dil_kn, e_w_out, o_w_in, o_qn, o_kn, o_w_out, mix_norm, ff_norm, w_ff1, w_ff2, ple_norm, w_ple_gate, w_ple_proj):
    t = BATCH * SEQ
    h = x.reshape(t, D_MODEL)
    pos2d = positions.reshape(t, 1)
    p3d = p.reshape(DEPTH, t, PLE_DIM)

    gmq = _group_matrix((MLA_NOPE, MLA_NOPE, MLA_ROPE, MLA_ROPE, LANES - 2 * MLA_ROPE), MXU_DIM)
    gm64 = _group_matrix((HEAD_DIM,) * (MXU_DIM // HEAD_DIM), MXU_DIM)
    gm32 = _group_matrix((MLA_ROPE,) * (LANES // MLA_ROPE), LANES)
    half = MLA_ROPE // 2
    invf = ROPE_THETA ** (-(jnp.arange(LANES) % half).astype(F32) / half)
    invf = invf[None, :]

    for i in range(DEPTH):
        j = i // 2
        gmix = mix_norm[i][None, :]
        if i % 2 == 0:
            win, wuq, wukv, gq, gkn, gkr, gqb, gkb = _even_weights(
                e_w_in[j], e_w_uq[j], e_w_ukv[j], e_qn_nope[j], e_qn_rope[j], e_kn_nope[j],
                e_kn_rope[j], e_dil_qn[j], e_dil_kn[j])
            qa, ka, vat, *qkv_b = _even_proj(
                h, pos2d, gmix, win, e_cq_norm[j][None, :], e_ckv_norm[j][None, :], wuq, wukv,
                gq, gkn, gkr, gqb, gkb, gmq, gm64, gm32, invf)
            oa = _mla_attention(qa, ka, vat)
            branches = [_dilated_branch(*qkv_b[3 * n:3 * n + 3], _dil_bias_table(d, w), d)
                        for n, (w, d) in enumerate(DIL_PATTERNS)]
            h = _even_out(h, oa, branches, e_w_out[j].astype(BF16))
            mix = {}
        else:
            gq = (jnp.tile(o_qn[j], MOBA_HEADS) * (HEAD_DIM ** -0.5 * LOG2E))[None, :]
            gk = jnp.tile(o_kn[j], MOBA_HEADS)[None, :]
            q, k, vt, km = _odd_proj(h, gmix, o_w_in[j].astype(BF16), gq, gk, gm64)
            o = _moba_attention(q, k, vt, km, _pair_slopes(MOBA_HEADS),
                                _pair_key_bias(MOBA_HEADS, MOBA_BLOCK))
            mix = {"mix": o, "w_out": o_w_out[j].astype(BF16)}
        h = _tail(h, p3d, i, ff_norm[i][None, :], w_ff1[i].astype(BF16), w_ff2[i].astype(BF16),
                  ple_norm[i][None, :], w_ple_gate[i].astype(BF16), w_ple_proj[i].astype(BF16),
                  **mix)
    return h.reshape(BATCH, SEQ, D_MODEL)
```
